```python
import math
import jax, jax.numpy as jnp
from jax import lax
import numpy as np

D_MODEL = 1024
BATCH = 2
SEQ = 8192
DEPTH = 2

GRID_W = 64
CTX_LEN = 256

DN_HEADS = 4
DN_DK = 128
DN_DV = 256
DN_QKW = DN_HEADS * DN_DK
DN_VW = DN_HEADS * DN_DV
DN_CONV_CH = 2 * DN_QKW + DN_VW
CONV_W = 5
RET_HEADS = 4
RET_DK = 128
RET_DV = 256
RET_QKW = RET_HEADS * RET_DK
RET_VW = RET_HEADS * RET_DV
ROPE_BASE = 10000.0
HG_HEADS = 8
HG_DK = 128
HG_DV = 128
HG_KW = HG_HEADS * HG_DK
HG_VW = HG_HEADS * HG_DV

N_BRANCH = 3
BRANCH_W = 1024
CHUNK = 64
HG_CHUNK = 32
NORM_EPS = 1e-6

SPLIT_SIZES = (DN_CONV_CH, DN_VW, 2 * DN_HEADS, 2 * DN_HEADS,
               RET_QKW, RET_QKW, RET_VW, RET_VW,
               HG_KW, 2 * HG_KW, HG_VW, HG_VW,
               N_BRANCH * D_MODEL)
IN_WIDTH = sum(SPLIT_SIZES)

kernel_name = "hybrid_deltanet_retention_hgrn2_block"


def rms_norm(x, w):
    xf = x.astype(jnp.float32)
    y = xf * lax.rsqrt(jnp.mean(xf * xf, axis=-1, keepdims=True) + NORM_EPS)
    return (y * w.astype(jnp.float32)).astype(x.dtype)


def l2norm(x):
    return x * lax.rsqrt(jnp.sum(x * x, axis=-1, keepdims=True) + NORM_EPS)


def short_conv(x, w):
    pad = CONV_W // 2
    return lax.conv_general_dilated(x, w[:, None, :].astype(x.dtype), window_strides=(1,),
                                    padding=[(pad, pad)], dimension_numbers=('NWC', 'WIO', 'NWC'),
                                    feature_group_count=x.shape[-1])


def axial_rotary(x, row, col):
    half = x.shape[-1] // 2
    inv = ROPE_BASE ** (-jnp.arange(0, half, 2, dtype=jnp.float32) / half)
    ang = jnp.concatenate([row[:, None] * inv, col[:, None] * inv], axis=-1)[:, None, :]
    cos, sin = jnp.cos(ang), jnp.sin(ang)
    x1, x2 = x[..., 0::2], x[..., 1::2]
    return jnp.stack([x1 * cos - x2 * sin, x1 * sin + x2 * cos], axis=-1).reshape(x.shape)


def _heads(t, n_heads):
    B, T, _ = t.shape
    return t.reshape(B, T, n_heads, -1).transpose(0, 2, 1, 3)


def _chunks(t, c):
    B, H, T = t.shape[:3]
    return jnp.moveaxis(t.reshape(B, H, T // c, c, *t.shape[3:]), 2, 0)


def _unchunk(o):
    n, B, H, c, d = o.shape
    return jnp.moveaxis(o, 0, 2).reshape(B, H, n * c, d)


def gated_delta_scan(q, k, v, g, beta, s0):
    dk = q.shape[-1]
    qc, kc, vc, gc, bc = [_chunks(t, CHUNK) for t in (q, k, v, g, beta)]
    gcum = jnp.cumsum(gc, axis=-1)
    idx = jnp.arange(CHUNK)
    tril = idx[:, None] >= idx[None, :]
    strict = idx[:, None] > idx[None, :]
    dec = jnp.exp(jnp.where(tril, gcum[..., :, None] - gcum[..., None, :], -jnp.inf))
    kb = kc * bc[..., None]
    lower = jnp.where(strict, jnp.einsum('nbhid,nbhjd->nbhij', kb, kc) * dec, 0.0)
    a_mat = lower + jnp.eye(CHUNK, dtype=lower.dtype)
    rhs = jnp.concatenate([kb * jnp.exp(gcum)[..., None], vc * bc[..., None]], axis=-1)
    sol = lax.linalg.triangular_solve(a_mat, rhs, left_side=True, lower=True)
    w, u = sol[..., :dk], sol[..., dk:]
    attn = jnp.einsum('nbhid,nbhjd->nbhij', qc, kc) * dec
    qd = qc * jnp.exp(gcum)[..., None]
    kd = kc * jnp.exp(gcum[..., -1:] - gcum)[..., None]
    gl = jnp.exp(gcum[..., -1])

    def step(S, inp):
        wi, ui, ai, qi, ki, gi = inp
        vn = ui - jnp.einsum('bhid,bhdv->bhiv', wi, S)
        o = jnp.einsum('bhid,bhdv->bhiv', qi, S) + jnp.einsum('bhij,bhjv->bhiv', ai, vn)
        S = S * gi[..., None, None] + jnp.einsum('bhjd,bhjv->bhdv', ki, vn)
        return S, o

    S, o = lax.scan(step, s0, (w, u, attn, qd, kd, gl))
    return _unchunk(o), S


def retention_scan(q, k, v, g, s0):
    qc, kc, vc, gc = [_chunks(t, CHUNK) for t in (q, k, v, g)]
    gcum = jnp.cumsum(gc, axis=-1)
    idx = jnp.arange(CHUNK)
    tril = idx[:, None] >= idx[None, :]

    def step(S, inp):
        qi, ki, vi, gi = inp
        dec = jnp.exp(jnp.where(tril, gi[..., :, None] - gi[..., None, :], -jnp.inf))
        attn = jnp.einsum('bhid,bhjd->bhij', qi, ki) * dec
        o = (jnp.einsum('bhij,bhjv->bhiv', attn, vi)
             + jnp.einsum('bhid,bhdv->bhiv', qi * jnp.exp(gi)[..., None], S))
        S = (S * jnp.exp(gi[..., -1])[..., None, None]
             + jnp.einsum('bhjd,bhjv->bhdv', ki * jnp.exp(gi[..., -1:] - gi)[..., None], vi))
        return S, o

    S, o = lax.scan(step, s0, (qc, kc, vc, gcum))
    return _unchunk(o), S


def hgrn2_scan(q, k, v, g, s0):
    qc, kc, vc, gc = [_chunks(t, HG_CHUNK) for t in (q, k, v, g)]
    gcum = jnp.cumsum(gc, axis=-2)
    idx = jnp.arange(HG_CHUNK)
    tril = (idx[:, None] >= idx[None, :])[:, :, None]

    def step(S, inp):
        qi, ki, vi, gi = inp
        rel = jnp.exp(jnp.where(tril, gi[..., :, None, :] - gi[..., None, :, :], -jnp.inf))
        attn = jnp.einsum('bhid,bhjd,bhijd->bhij', qi, ki, rel)
        o = (jnp.einsum('bhij,bhjv->bhiv', attn, vi)
             + jnp.einsum('bhid,bhdv->bhiv', qi * jnp.exp(gi), S))
        S = (S * jnp.exp(gi[..., -1, :])[..., :, None]
             + jnp.einsum('bhjd,bhjv->bhdv', ki * jnp.exp(gi[..., -1:, :] - gi), vi))
        return S, o

    S, o = lax.scan(step, s0, (qc, kc, vc, gcum))
    return _unchunk(o), S


def bidirectional(scan_fn, ctx_args, lat_args, s0):
    flip = lambda t: jnp.flip(t, axis=2)
    (cf, cb), (lf, lb) = ctx_args, lat_args
    o_cf, s_f = scan_fn(*cf, s0)
    o_cb, s_b = scan_fn(*map(flip, cb), s0)
    o_lf, _ = scan_fn(*lf, s_f)
    o_lb, _ = scan_fn(*map(flip, lb), s_b)
    return o_cf + flip(o_cb), o_lf + flip(o_lb)


def mixer_inputs(h, w_in, conv_w, dn_a_log, dn_dt_bias, ret_log_gamma, hg_lower, rot):
    f32 = jnp.float32
    B, T, _ = h.shape
    p = jnp.einsum('btd,de->bte', h, w_in)
    bounds = np.cumsum(SPLIT_SIZES)[:-1].tolist()
    (dn_qkv, dn_gate, dn_a, dn_b, r_q, r_k, r_v, r_gate,
     hg_q, hg_f, hg_i, hg_gate, merge) = jnp.split(p, bounds, axis=-1)

    qkv = jax.nn.silu(short_conv(dn_qkv, conv_w)).astype(f32)
    q, k, v = jnp.split(qkv, [DN_QKW, 2 * DN_QKW], axis=-1)
    q = l2norm(_heads(q, DN_HEADS)) * DN_DK ** -0.5
    k = l2norm(_heads(k, DN_HEADS))
    v = _heads(v, DN_HEADS)
    a = dn_a.astype(f32).reshape(B, T, 2, DN_HEADS).transpose(2, 0, 3, 1)
    b = dn_b.astype(f32).reshape(B, T, 2, DN_HEADS).transpose(2, 0, 3, 1)
    g = (-jnp.exp(dn_a_log.astype(f32))[:, None, :, None]
         * jax.nn.softplus(a + dn_dt_bias.astype(f32)[:, None, :, None]))
    beta = jax.nn.sigmoid(b)
    dn = tuple((q, k, v, g[d], beta[d]) for d in range(2))

    rq = r_q.astype(f32).reshape(B, T, RET_HEADS, RET_DK)
    rk = r_k.astype(f32).reshape(B, T, RET_HEADS, RET_DK)
    if rot is not None:
        rq, rk = axial_rotary(rq, *rot), axial_rotary(rk, *rot)
    rq = rq.transpose(0, 2, 1, 3)
    rk = rk.transpose(0, 2, 1, 3) * RET_DK ** -0.5
    rv = _heads(r_v, RET_HEADS).astype(f32)
    lam = jnp.broadcast_to(ret_log_gamma[:, None, :, None], (2, B, RET_HEADS, T))
    ret = tuple((rq, rk, rv, lam[d]) for d in range(2))

    hq = jax.nn.silu(_heads(hg_q, HG_HEADS).astype(f32))
    hi = _heads(hg_i, HG_HEADS).astype(f32)
    f = jax.nn.sigmoid(hg_f.astype(f32).reshape(B, T, 2, HG_KW))
    f = hg_lower + (1.0 - hg_lower) * f
    f = f.reshape(B, T, 2, HG_HEADS, HG_DK).transpose(2, 0, 3, 1, 4)
    hg = tuple((hq, 1.0 - f[d], hi, jnp.log(f[d])) for d in range(2))

    return dn, ret, hg, (dn_gate, r_gate, hg_gate), merge


def branch_out(o, gate, norm_w):
    B, H, T, dv = o.shape
    o = rms_norm(o.transpose(0, 2, 1, 3), norm_w).reshape(B, T, H * dv)
    return o.astype(gate.dtype) * jax.nn.silu(gate)


def merge_branches(branches, merge, w_branch, w_out):
    B, T, _ = merge.shape
    gates = jax.nn.sigmoid(merge.reshape(B, T, N_BRANCH, D_MODEL))
    y = jnp.einsum('btw,wd->btd', branches[0], w_branch[0]) * gates[:, :, 0]
    for i in range(1, N_BRANCH):
        y = y + jnp.einsum('btw,wd->btd', branches[i], w_branch[i]) * gates[:, :, i]
    return jnp.einsum('btd,de->bte', y, w_out)


def setup_inputs(seed: int = 0) -> dict:
    key = jax.random.key(seed)
    ks = jax.random.split(key, 20)
    f32 = jnp.float32
    nrm = lambda k, shape, s: jax.random.normal(k, shape, f32) * s
    x = nrm(ks[0], (BATCH, SEQ, D_MODEL), 1.0)
    c = nrm(ks[1], (BATCH, D_MODEL), 1.0)
    ctx = nrm(ks[2], (BATCH, CTX_LEN, D_MODEL), 1.0)
    c_ctx = nrm(ks[3], (D_MODEL,), 1.0)
    norm_w = 1.0 + nrm(ks[4], (DEPTH, D_MODEL), 0.02)
    ada_w = nrm(ks[5], (DEPTH, D_MODEL, 3 * D_MODEL), 0.5 * D_MODEL ** -0.5)
    ada_b = nrm(ks[6], (DEPTH, 3 * D_MODEL), 0.02)
    w_in = nrm(ks[7], (DEPTH, D_MODEL, IN_WIDTH), D_MODEL ** -0.5)
    dn_conv = nrm(ks[8], (DEPTH, CONV_W, DN_CONV_CH), CONV_W ** -0.5)
    dn_a_log = jnp.log(jax.random.uniform(ks[9], (DEPTH, 2, DN_HEADS), f32, 1.0, 16.0))
    dt = jnp.exp(jax.random.uniform(ks[10], (DEPTH, 2, DN_HEADS), f32, math.log(1e-3), math.log(1e-1)))
    dn_dt_bias = dt + jnp.log(-jnp.expm1(-dt))
    dn_norm_w = 1.0 + nrm(ks[11], (DEPTH, DN_DV), 0.02)
    gamma_logit = jnp.log(2.0 ** (5.0 + jnp.arange(RET_HEADS, dtype=f32)) - 1.0)
    ret_decay = gamma_logit + nrm(ks[12], (DEPTH, 2, RET_HEADS), 0.01)
    ret_norm_w = 1.0 + nrm(ks[13], (DEPTH, RET_DV), 0.02)
    hg_lb = nrm(ks[14], (DEPTH, 2, HG_KW), 0.1)
    hg_norm_w = 1.0 + nrm(ks[15], (DEPTH, HG_DV), 0.02)
    w_branch = nrm(ks[16], (DEPTH, N_BRANCH, BRANCH_W, D_MODEL), BRANCH_W ** -0.5)
    w_out = nrm(ks[17], (DEPTH, D_MODEL, D_MODEL), D_MODEL ** -0.5)
    final_norm_w = 1.0 + nrm(ks[18], (D_MODEL,), 0.02)
    return {"x": x, "c": c, "ctx": ctx, "c_ctx": c_ctx, "norm_w": norm_w, "ada_w": ada_w,
            "ada_b": ada_b, "w_in": w_in, "dn_conv": dn_conv, "dn_a_log": dn_a_log,
            "dn_dt_bias": dn_dt_bias, "dn_norm_w": dn_norm_w, "ret_decay": ret_decay,
            "ret_norm_w": ret_norm_w, "hg_lb": hg_lb, "hg_norm_w": hg_norm_w,
            "w_branch": w_branch, "w_out": w_out, "final_norm_w": final_norm_w}


def reference(x, c, ctx, c_ctx, norm_w, ada_w, ada_b, w_in, dn_conv, dn_a_log, dn_dt_bias, dn_norm_w,
              ret_decay, ret_norm_w, hg_lb, hg_norm_w, w_branch, w_out, final_norm_w):
    f32 = jnp.float32
    B, n_tok, _ = x.shape
    rows = n_tok // GRID_W
    row = jnp.repeat(jnp.arange(rows, dtype=f32), GRID_W)
    col = jnp.tile(jnp.arange(GRID_W, dtype=f32), rows)
    lb_sm = jax.nn.softmax(hg_lb.astype(f32), axis=0)
    lower = jnp.cumsum(lb_sm, axis=0) - lb_sm[0:1]
    silu_c = jax.nn.silu(c)
    silu_cc = jax.nn.silu(c_ctx)

    for l in range(DEPTH):
        last = l == DEPTH - 1
        sh, sc, gt = jnp.split(silu_c @ ada_w[l] + ada_b[l], 3, axis=-1)
        sh_c, sc_c, gt_c = jnp.split(silu_cc @ ada_w[l] + ada_b[l], 3, axis=-1)
        h = rms_norm(x, norm_w[l]) * (1.0 + sc[:, None]) + sh[:, None]
        hc = rms_norm(ctx, norm_w[l]) * (1.0 + sc_c) + sh_c
        log_gamma = jax.nn.log_sigmoid(ret_decay[l].astype(f32))
        lat_in = mixer_inputs(h, w_in[l], dn_conv[l], dn_a_log[l], dn_dt_bias[l], log_gamma, lower[l],
                              (row, col))
        ctx_in = mixer_inputs(hc, w_in[l], dn_conv[l], dn_a_log[l], dn_dt_bias[l], log_gamma, lower[l],
                              None)
        mixers = ((gated_delta_scan, (DN_HEADS, DN_DK, DN_DV), dn_norm_w[l]),
                  (retention_scan, (RET_HEADS, RET_DK, RET_DV), ret_norm_w[l]),
                  (hgrn2_scan, (HG_HEADS, HG_DK, HG_DV), hg_norm_w[l]))
        o_ctx, o_lat = [], []
        for i, (fn, sdim, _) in enumerate(mixers):
            oc, ol = bidirectional(fn, ctx_in[i], lat_in[i], jnp.zeros((B,) + sdim, f32))
            o_ctx.append(oc)
            o_lat.append(ol)
        lat_br = [branch_out(o, g, m[2]) for o, g, m in zip(o_lat, lat_in[3], mixers)]
        x = x + gt[:, None, :] * merge_branches(lat_br, lat_in[4], w_branch[l], w_out[l]).astype(x.dtype)
        if not last:
            ctx_br = [branch_out(o, g, m[2]) for o, g, m in zip(o_ctx, ctx_in[3], mixers)]
            ctx = ctx + gt_c * merge_branches(ctx_br, ctx_in[4], w_branch[l], w_out[l]).astype(ctx.dtype)

    return rms_norm(x, final_norm_w)
```

```python
import functools
import math

import jax
import jax.numpy as jnp
import numpy as np
from jax import lax
from jax.experimental import pallas as pl
from jax.experimental.pallas import tpu as pltpu

F32 = jnp.float32
BF16 = jnp.bfloat16

D_MODEL = 1024
GRID_W = 64
DN_HEADS, DN_DK, DN_DV = 4, 128, 256
RET_HEADS, RET_DK, RET_DV = 4, 128, 256
HG_HEADS, HG_DK, HG_DV = 8, 128, 128
CONV_W = 5
ROPE_BASE = 10000.0
NORM_EPS = 1e-6
N_BRANCH = 3

LANES = 128
TOK_BLK = 256
DN_CHUNK = 64
HG_CHUNK = 64
NEG_BIG = -1e30

C_DNQKV = 0
C_DNGATE = 2048
C_RQ = 3072
C_RK = 3584
C_RV = 4096
C_RGATE = 5120
C_HQ = 6144
C_HF = 7168
C_HI = 9216
C_HGATE = 10240
C_MERGE = 11264
P_WIDTH = 14336
VMEM_LIMIT = 56 * 1024 * 1024


def _dot(a, b):
    return jnp.dot(a.astype(BF16), b.astype(BF16), preferred_element_type=F32)


def _dot_nt(a, b):
    return lax.dot_general(a.astype(BF16), b.astype(BF16), (((1,), (1,)), ((), ())),
                           preferred_element_type=F32)


def _dot_tn(a, b):
    return lax.dot_general(a.astype(BF16), b.astype(BF16), (((0,), (0,)), ((), ())),
                           preferred_element_type=F32)


def _sigmoid(x):
    return 1.0 / (1.0 + jnp.exp(-x))


def _silu(x):
    return x * _sigmoid(x)


def _softplus(x):
    return jnp.maximum(x, 0.0) + jnp.log(1.0 + jnp.exp(-jnp.abs(x)))


def _iota(shape, dim):
    return lax.broadcasted_iota(jnp.int32, shape, dim)


def _shr(x, s):
    return lax.shift_right_logical(x, int(s).bit_length() - 1)


def _cparams(sem):
    return pltpu.CompilerParams(dimension_semantics=sem, vmem_limit_bytes=VMEM_LIMIT)


def _mod_kernel(cin_ref, w_ref, b_ref, o_ref):
    s = _silu(cin_ref[...])
    o_ref[0] = _dot(s, w_ref[0]) + b_ref[0]


def _modulation(cin, ada_w, ada_b):
    depth, d, d3 = ada_w.shape
    tn = 1024
    return pl.pallas_call(
        _mod_kernel,
        grid=(depth, d3 // tn),
        in_specs=[pl.BlockSpec((8, d), lambda l, j: (0, 0)),
                  pl.BlockSpec((1, d, tn), lambda l, j: (l, 0, j)),
                  pl.BlockSpec((1, 1, tn), lambda l, j: (l, 0, j))],
        out_specs=pl.BlockSpec((1, 8, tn), lambda l, j: (l, 0, j)),
        out_shape=jax.ShapeDtypeStruct((depth, 8, d3), F32),
        compiler_params=_cparams(("arbitrary", "arbitrary")),
        name="ada_mod",
    )(cin, ada_w, ada_b.reshape(depth, 1, d3))


def _lower_kernel(lb_ref, o_ref, *, depth):
    xs = [lb_ref[l] for l in range(depth)]
    m = xs[0]
    for l in range(1, depth):
        m = jnp.maximum(m, xs[l])
    es = [jnp.exp(x - m) for x in xs]
    tot = es[0]
    for l in range(1, depth):
        tot = tot + es[l]
    sm = [e / tot for e in es]
    acc = sm[0]
    o_ref[0] = acc - sm[0]
    for l in range(1, depth):
        acc = acc + sm[l]
        o_ref[l] = acc - sm[0]


def _hg_lower(hg_lb):
    depth = hg_lb.shape[0]
    lb = hg_lb.astype(F32).reshape(depth, 2, HG_HEADS * HG_DK)
    return pl.pallas_call(
        functools.partial(_lower_kernel, depth=depth),
        out_shape=jax.ShapeDtypeStruct(lb.shape, F32),
        name="hg_lower",
    )(lb)


def _proj_kernel(x_ref, mod_ref, nw_ref, w_ref, wab_ref, p_ref, pab_ref, hb_ref, *, tc, tm, tn, n_b):
    b = pl.program_id(0)
    i = pl.program_id(1)
    j = pl.program_id(2)
    d = D_MODEL

    @pl.when(j == 0)
    def _():
        x = x_ref[0]
        ms = jnp.mean(x * x, axis=-1, keepdims=True)
        y = x * lax.rsqrt(ms + NORM_EPS) * nw_ref[...]
        row = i * tm + _iota((tm, 1), 0)
        is_ctx = row < tc
        sh = jnp.where(is_ctx, mod_ref[n_b:n_b + 1, 0:d], mod_ref[pl.ds(b, 1), 0:d])
        sc = jnp.where(is_ctx, mod_ref[n_b:n_b + 1, d:2 * d], mod_ref[pl.ds(b, 1), d:2 * d])
        hb = (y * (1.0 + sc) + sh).astype(BF16)
        hb_ref[...] = hb
        pab_ref[0] = jnp.dot(hb, wab_ref[...], preferred_element_type=F32)

    sub = 512
    for s in range(tn // sub):
        p_ref[0, :, s * sub:(s + 1) * sub] = jnp.dot(
            hb_ref[...], w_ref[:, s * sub:(s + 1) * sub], preferred_element_type=F32).astype(BF16)


def _row_tile(ta):
    best = 16
    for t in range(16, 1101, 16):
        if ta % t == 0:
            best = t
    return best


def _project(xa, mod_l, nw, wp, wab, tc):
    n_b, ta, d = xa.shape
    tm = _row_tile(ta)
    tn = 2048
    return pl.pallas_call(
        functools.partial(_proj_kernel, tc=tc, tm=tm, tn=tn, n_b=n_b),
        grid=(n_b, ta // tm, P_WIDTH // tn),
        in_specs=[pl.BlockSpec((1, tm, d), lambda b, i, j: (b, i, 0)),
                  pl.BlockSpec((8, 3 * d), lambda b, i, j: (0, 0)),
                  pl.BlockSpec((1, d), lambda b, i, j: (0, 0)),
                  pl.BlockSpec((d, tn), lambda b, i, j: (0, j)),
                  pl.BlockSpec((d, LANES), lambda b, i, j: (0, 0))],
        out_specs=[pl.BlockSpec((1, tm, tn), lambda b, i, j: (b, i, j)),
                   pl.BlockSpec((1, tm, LANES), lambda b, i, j: (b, i, 0))],
        out_shape=[jax.ShapeDtypeStruct((n_b, ta, P_WIDTH), BF16),
                   jax.ShapeDtypeStruct((n_b, ta, LANES), F32)],
        scratch_shapes=[pltpu.VMEM((tm, d), BF16)],
        compiler_params=_cparams(("arbitrary", "arbitrary", "arbitrary")),
        name="in_proj",
    )(xa, mod_l, nw, wp, wab)


def _prep_kernel(main_ref, prev_ref, next_ref, rqk_ref, pab_ref, cw_ref, cos_ref, sin_ref, alog_ref, dtb_ref,
                 dq_ref, dk_ref, dkt_ref, dv_ref, rq_ref, rkt_ref, gcol_ref, grow_ref, xs_ref, *, tc, n_blk):
    i = pl.program_id(1)
    tb = TOK_BLK
    nctx = tc // tb
    has_prev = jnp.logical_and(i != 0, i != nctx)
    has_next = jnp.logical_and(i != nctx - 1, i != n_blk - 1)
    pm = jnp.where(has_prev, 1.0, 0.0)
    nm = jnp.where(has_next, 1.0, 0.0)
    n_ch = tb // DN_CHUNK

    for g in range(16):
        ls = slice(g * LANES, (g + 1) * LANES)
        xs_ref[8:8 + tb, :] = main_ref[0, :, ls].astype(F32)
        xs_ref[0:8, :] = prev_ref[0, 8:16, ls].astype(F32) * pm
        xs_ref[8 + tb:16 + tb, :] = next_ref[0, 0:8, ls].astype(F32) * nm
        acc = cw_ref[0:1, ls] * xs_ref[6:6 + tb, :]
        for t in range(1, CONV_W):
            acc = acc + cw_ref[t:t + 1, ls] * xs_ref[6 + t:6 + t + tb, :]
        y = _silu(acc)
        if g < 8:
            y = y * lax.rsqrt(jnp.sum(y * y, axis=-1, keepdims=True) + NORM_EPS)
        if g < 4:
            dq_ref[0, :, ls] = (y * (DN_DK ** -0.5)).astype(BF16)
        elif g < 8:
            h = g - 4
            hs = slice(h * LANES, (h + 1) * LANES)
            dk_ref[0, :, hs] = y.astype(BF16)
            for c in range(n_ch):
                dkt_ref[0, h, c] = y[c * DN_CHUNK:(c + 1) * DN_CHUNK, :].T.astype(BF16)
        else:
            vs = slice((g - 8) * LANES, (g - 7) * LANES)
            dv_ref[0, :, vs] = y.astype(BF16)

    pab = pab_ref[0]
    g_log = -jnp.exp(alog_ref[...]) * _softplus(pab + dtb_ref[...])
    beta = _sigmoid(pab)
    r = _iota((tb, tb), 0)
    c_ = _iota((tb, tb), 1)
    same = _shr(r, DN_CHUNK) == _shr(c_, DN_CHUNK)
    m_f = jnp.where(jnp.logical_and(same, r >= c_), 1.0, 0.0)
    m_b = jnp.where(jnp.logical_and(same, r <= c_), 1.0, 0.0)
    cs_f = jnp.dot(m_f, g_log, precision=lax.Precision.HIGHEST, preferred_element_type=F32)
    cs_b = jnp.dot(m_b, g_log, precision=lax.Precision.HIGHEST, preferred_element_type=F32)
    lane = _iota((tb, LANES), 1)
    gcol = jnp.where(lane < 4, cs_f, jnp.where(lane < 8, cs_b, beta))
    gcol_ref[0] = gcol
    for c in range(n_ch):
        grow_ref[0, c] = gcol[c * DN_CHUNK:(c + 1) * DN_CHUNK, :].T[0:8, :]

    cosv = cos_ref[...]
    sinv = sin_ref[...]
    for g in range(8):
        ls = slice(g * LANES, (g + 1) * LANES)
        x = rqk_ref[0, :, ls].astype(F32)
        y = x * cosv + pltpu.roll(x, LANES // 2, 1) * sinv
        if g < 4:
            rq_ref[0, :, ls] = y.astype(BF16)
        else:
            rkt_ref[0, g - 4] = (y * (RET_DK ** -0.5)).T.astype(BF16)


def _prep(p, pab, conv_w, cos_t, sin_t, alog_row, dtb_row, tc):
    n_b, ta, _ = p.shape
    tb = TOK_BLK
    n_blk = ta // tb
    n_ch = tb // DN_CHUNK
    hpb = tb // 16
    n16 = ta // 16
    out_shape = [
        jax.ShapeDtypeStruct((n_b, ta, 512), BF16),
        jax.ShapeDtypeStruct((n_b, ta, 512), BF16),
        jax.ShapeDtypeStruct((n_b, DN_HEADS, ta // DN_CHUNK, DN_DK, DN_CHUNK), BF16),
        jax.ShapeDtypeStruct((n_b, ta, 1024), BF16),
        jax.ShapeDtypeStruct((n_b, ta, 512), BF16),
        jax.ShapeDtypeStruct((n_b, RET_HEADS, RET_DK, ta), BF16),
        jax.ShapeDtypeStruct((n_b, ta, LANES), F32),
        jax.ShapeDtypeStruct((n_b, ta // DN_CHUNK, 8, DN_CHUNK), F32),
    ]
    out_specs = [
        pl.BlockSpec((1, tb, 512), lambda b, i: (b, i, 0)),
        pl.BlockSpec((1, tb, 512), lambda b, i: (b, i, 0)),
        pl.BlockSpec((1, DN_HEADS, n_ch, DN_DK, DN_CHUNK), lambda b, i: (b, 0, i, 0, 0)),
        pl.BlockSpec((1, tb, 1024), lambda b, i: (b, i, 0)),
        pl.BlockSpec((1, tb, 512), lambda b, i: (b, i, 0)),
        pl.BlockSpec((1, RET_HEADS, RET_DK, tb), lambda b, i: (b, 0, 0, i)),
        pl.BlockSpec((1, tb, LANES), lambda b, i: (b, i, 0)),
        pl.BlockSpec((1, n_ch, 8, DN_CHUNK), lambda b, i: (b, i, 0, 0)),
    ]
    in_specs = [
        pl.BlockSpec((1, tb, 2048), lambda b, i: (b, i, 0)),
        pl.BlockSpec((1, 16, 2048), lambda b, i: (b, jnp.maximum(i * hpb - 1, 0), 0)),
        pl.BlockSpec((1, 16, 2048), lambda b, i: (b, jnp.minimum((i + 1) * hpb, n16 - 1), 0)),
        pl.BlockSpec((1, tb, 1024), lambda b, i: (b, i, C_RQ // 1024)),
        pl.BlockSpec((1, tb, LANES), lambda b, i: (b, i, 0)),
        pl.BlockSpec((CONV_W, 2048), lambda b, i: (0, 0)),
        pl.BlockSpec((tb, LANES), lambda b, i: (i, 0)),
        pl.BlockSpec((tb, LANES), lambda b, i: (i, 0)),
        pl.BlockSpec((1, LANES), lambda b, i: (0, 0)),
        pl.BlockSpec((1, LANES), lambda b, i: (0, 0)),
    ]
    return pl.pallas_call(
        functools.partial(_prep_kernel, tc=tc, n_blk=n_blk),
        grid=(n_b, n_blk),
        in_specs=in_specs,
        out_specs=out_specs,
        out_shape=out_shape,
        scratch_shapes=[pltpu.VMEM((tb + 16, LANES), F32)],
        compiler_params=_cparams(("arbitrary", "arbitrary")),
        name="prep",
    )(p, p, p, p, pab, conv_w, cos_t, sin_t, alog_row, dtb_row)


def _blk_index(n, n_blk, nctx, reverse):
    if not reverse:
        return n
    return jnp.where(n < nctx, nctx - 1 - n, n_blk - 1 - (n - nctx))


def _tri_inverse(a_mat, ri, ci):
    n = DN_CHUNK
    eye = jnp.where(ri == ci, 1.0, 0.0)
    d = jnp.where(_shr(ri, 8) == _shr(ci, 8), a_mat, 0.0)
    d2 = _dot(d, d)
    d4 = _dot(d2, d2)
    p1 = eye - d + d2 - _dot(d, d2)
    t = p1 + _dot(p1, d4)
    s = 8
    while s < n:
        off = jnp.where(jnp.logical_and(_shr(ri, 2 * s) == _shr(ci, 2 * s), _shr(ri, s) != _shr(ci, s)), a_mat, 0.0)
        t = t - _dot(t, _dot(off, t))
        s *= 2
    return t


def _dn_kernel(q_ref, k_ref, kt_ref, v_ref, gcol_ref, grow_ref, o_ref, s_ref, *, reverse):
    n = pl.program_id(1)
    c_len = DN_CHUNK
    n_ch = TOK_BLK // c_len
    d_off = 4 if reverse else 0

    @pl.when(n == 0)
    def _():
        s_ref[...] = jnp.zeros_like(s_ref)

    ri = _iota((c_len, c_len), 0)
    ci = _iota((c_len, c_len), 1)
    if reverse:
        incl, strict = ri <= ci, ri < ci
    else:
        incl, strict = ri >= ci, ri > ci
    last = 0 if reverse else c_len - 1

    def chunk_body(cc, carry):
        c = (n_ch - 1 - cc) if reverse else cc
        rows = pl.ds(pl.multiple_of(c * c_len, c_len), c_len)
        gcol = gcol_ref[0, rows, :]
        grow = grow_ref[0, c]
        for h in range(DN_HEADS):
            hs = slice(h * DN_DK, (h + 1) * DN_DK)
            vs = slice(h * DN_DV, (h + 1) * DN_DV)
            q = q_ref[0, rows, hs]
            k = k_ref[0, rows, hs].astype(F32)
            kt = kt_ref[0, h, c]
            v = v_ref[0, rows, vs].astype(F32)
            gc = gcol[:, d_off + h:d_off + h + 1]
            bc = gcol[:, 8 + d_off + h:9 + d_off + h]
            gr = grow[d_off + h:d_off + h + 1, :]
            gl = gr[:, last:last + 1]
            dec = jnp.exp(jnp.where(incl, gc - gr, NEG_BIG))
            kb = k * bc
            a_mat = jnp.where(strict, _dot(kb, kt) * dec, 0.0)
            t = _tri_inverse(a_mat, ri, ci)
            egc = jnp.exp(gc)
            w = _dot(t, kb * egc)
            u = _dot(t, v * bc)
            attn = _dot(q, kt) * dec
            qd = q.astype(F32) * egc
            kdt = kt.astype(F32) * jnp.exp(gl - gr)
            s_old = s_ref[h]
            vn = u - _dot(w, s_old)
            o = _dot(qd, s_old) + _dot(attn, vn)
            s_ref[h] = s_old * jnp.exp(gl) + _dot(kdt, vn)
            o_ref[0, rows, vs] = o.astype(BF16)
        return carry

    lax.fori_loop(0, n_ch, chunk_body, 0)


def _dn_scan(dq, dk, dkt, dv, gcol, grow, tc, reverse):
    n_b, ta, _ = dq.shape
    tb = TOK_BLK
    n_blk = ta // tb
    nctx = tc // tb
    n_ch = tb // DN_CHUNK
    bi = functools.partial(_blk_index, n_blk=n_blk, nctx=nctx, reverse=reverse)
    return pl.pallas_call(
        functools.partial(_dn_kernel, reverse=reverse),
        grid=(n_b, n_blk),
        in_specs=[pl.BlockSpec((1, tb, 512), lambda b, n: (b, bi(n), 0)),
                  pl.BlockSpec((1, tb, 512), lambda b, n: (b, bi(n), 0)),
                  pl.BlockSpec((1, DN_HEADS, n_ch, DN_DK, DN_CHUNK), lambda b, n: (b, 0, bi(n), 0, 0)),
                  pl.BlockSpec((1, tb, 1024), lambda b, n: (b, bi(n), 0)),
                  pl.BlockSpec((1, tb, LANES), lambda b, n: (b, bi(n), 0)),
                  pl.BlockSpec((1, n_ch, 8, DN_CHUNK), lambda b, n: (b, bi(n), 0, 0))],
        out_specs=pl.BlockSpec((1, tb, 1024), lambda b, n: (b, bi(n), 0)),
        out_shape=jax.ShapeDtypeStruct((n_b, ta, 1024), BF16),
        scratch_shapes=[pltpu.VMEM((DN_HEADS, DN_DK, DN_DV), F32)],
        compiler_params=_cparams(("arbitrary", "arbitrary")),
        name="dn_bwd" if reverse else "dn_fwd",
    )(dq, dk, dkt, dv, gcol, grow)


def _ret_kernel(q_ref, kt_ref, v_ref, lam_ref, o_ref, s_ref, dec_ref, eq_ref, ek_ref, *, reverse):
    n = pl.program_id(1)
    c_len = TOK_BLK
    d_off = 4 if reverse else 0

    @pl.when(n == 0)
    def _():
        s_ref[...] = jnp.zeros_like(s_ref)
        ri = _iota((c_len, c_len), 0)
        ci = _iota((c_len, c_len), 1)
        dist = (ci - ri) if reverse else (ri - ci)
        rowpos = _iota((c_len, LANES), 0)
        colpos = _iota((RET_DK, c_len), 1)
        if reverse:
            qexp = (c_len - rowpos).astype(F32)
            kexp = colpos.astype(F32)
        else:
            qexp = (rowpos + 1).astype(F32)
            kexp = (c_len - 1 - colpos).astype(F32)
        for h in range(RET_HEADS):
            x = lam_ref[0:1, d_off + h:d_off + h + 1]
            lam = jnp.minimum(x, 0.0) - jnp.log(1.0 + jnp.exp(-jnp.abs(x)))
            dec_ref[h] = jnp.exp(jnp.where(dist >= 0, dist.astype(F32) * lam, NEG_BIG))
            eq_ref[h] = jnp.exp(qexp * lam)
            ek_ref[h] = jnp.exp(kexp * lam)

    for h in range(RET_HEADS):
        hs = slice(h * RET_DK, (h + 1) * RET_DK)
        vs = slice(h * RET_DV, (h + 1) * RET_DV)
        q = q_ref[0, :, hs]
        kt = kt_ref[0, h]
        v = v_ref[0, :, vs]
        x = lam_ref[0:1, d_off + h:d_off + h + 1]
        lam = jnp.minimum(x, 0.0) - jnp.log(1.0 + jnp.exp(-jnp.abs(x)))
        attn = _dot(q, kt) * dec_ref[h]
        s_old = s_ref[h]
        o = _dot(attn, v) + _dot(q.astype(F32) * eq_ref[h], s_old)
        s_ref[h] = s_old * jnp.exp(lam * float(c_len)) + _dot(kt.astype(F32) * ek_ref[h], v)
        o_ref[0, :, vs] = o.astype(BF16)


def _ret_scan(rq, rkt, p, lam_row, tc, reverse):
    n_b, ta, _ = rq.shape
    tb = TOK_BLK
    n_blk = ta // tb
    nctx = tc // tb
    bi = functools.partial(_blk_index, n_blk=n_blk, nctx=nctx, reverse=reverse)
    return pl.pallas_call(
        functools.partial(_ret_kernel, reverse=reverse),
        grid=(n_b, n_blk),
        in_specs=[pl.BlockSpec((1, tb, 512), lambda b, n: (b, bi(n), 0)),
                  pl.BlockSpec((1, RET_HEADS, RET_DK, tb), lambda b, n: (b, 0, 0, bi(n))),
                  pl.BlockSpec((1, tb, 1024), lambda b, n: (b, bi(n), C_RV // 1024)),
                  pl.BlockSpec((1, LANES), lambda b, n: (0, 0))],
        out_specs=pl.BlockSpec((1, tb, 1024), lambda b, n: (b, bi(n), 0)),
        out_shape=jax.ShapeDtypeStruct((n_b, ta, 1024), BF16),
        scratch_shapes=[pltpu.VMEM((RET_HEADS, RET_DK, RET_DV), F32),
                        pltpu.VMEM((RET_HEADS, tb, tb), F32),
                        pltpu.VMEM((RET_HEADS, tb, LANES), F32),
                        pltpu.VMEM((RET_HEADS, RET_DK, tb), F32)],
        compiler_params=_cparams(("arbitrary", "arbitrary")),
        name="ret_bwd" if reverse else "ret_fwd",
    )(rq, rkt, p, lam_row)


def _hg_scores(q, k, gc, ri, ci, rowi, reverse):
    c_len = HG_CHUNK
    a = jnp.where(ri == ci, _dot_nt(q, k), 0.0)
    ref = gc
    s = 1
    while s < c_len:
        if s > 1:
            h = s // 2
            if reverse:
                ref = jnp.where((rowi & (s - 1)) >= h, ref, pltpu.roll(ref, c_len - h, 0))
            else:
                ref = jnp.where((rowi & (s - 1)) < h, ref, pltpu.roll(ref, h, 0))
        if reverse:
            kref = pltpu.roll(ref, s, 0)
            qgrp_ok = (_shr(ri, s) & 1) == 0
        else:
            kref = pltpu.roll(ref, c_len - s, 0)
            qgrp_ok = (_shr(ri, s) & 1) == 1
        qt = q * jnp.exp(jnp.minimum(gc - ref, 0.0))
        kt = k * jnp.exp(jnp.minimum(kref - gc, 0.0))
        pair = jnp.logical_and(_shr(ri, 2 * s) == _shr(ci, 2 * s),
                               jnp.logical_and(qgrp_ok, _shr(ri, s) != _shr(ci, s)))
        a = a + jnp.where(pair, _dot_nt(qt, kt), 0.0)
        s *= 2
    return a


def _hg_kernel(xq_ref, xf_ref, xi_ref, low_ref, o_ref, st_ref, q_s, k_s, gc_s, *, reverse):
    n = pl.program_id(1)
    tb = TOK_BLK
    c_len = HG_CHUNK
    n_ch = tb // c_len

    @pl.when(n == 0)
    def _():
        st_ref[...] = jnp.zeros_like(st_ref)

    low = low_ref[...]
    f = low + (1.0 - low) * _sigmoid(xf_ref[0].astype(F32))
    g = jnp.log(f)
    r = _iota((tb, tb), 0)
    c_ = _iota((tb, tb), 1)
    same = _shr(r, c_len) == _shr(c_, c_len)
    tri = (r <= c_) if reverse else (r >= c_)
    m = jnp.where(jnp.logical_and(same, tri), 1.0, 0.0).astype(BF16)
    g0 = g.astype(BF16)
    r1 = g - g0.astype(F32)
    g1 = r1.astype(BF16)
    g2 = (r1 - g1.astype(F32)).astype(BF16)
    gc_s[...] = (jnp.dot(m, g0, preferred_element_type=F32) + jnp.dot(m, g1, preferred_element_type=F32)
                 + jnp.dot(m, g2, preferred_element_type=F32))
    k_s[...] = 1.0 - f
    q_s[...] = _silu(xq_ref[0].astype(F32))

    ri = _iota((c_len, c_len), 0)
    ci = _iota((c_len, c_len), 1)
    rowi = _iota((c_len, HG_DK), 0)
    last = 0 if reverse else c_len - 1

    def chunk_body(cc, carry):
        c = (n_ch - 1 - cc) if reverse else cc
        rows = pl.ds(pl.multiple_of(c * c_len, c_len), c_len)
        for h in range(HG_HEADS):
            hs = slice(h * HG_DK, (h + 1) * HG_DK)
            q = q_s[rows, hs]
            k = k_s[rows, hs]
            gc = gc_s[rows, hs]
            v = xi_ref[0, rows, hs]
            gl = gc[last:last + 1, :]
            a = _hg_scores(q, k, gc, ri, ci, rowi, reverse)
            st = st_ref[h]
            o = _dot(a, v) + _dot_nt(q * jnp.exp(gc), st)
            st_ref[h] = st * jnp.exp(gl) + _dot_tn(v, k * jnp.exp(gl - gc))
            o_ref[0, rows, hs] = o.astype(BF16)
        return carry

    lax.fori_loop(0, n_ch, chunk_body, 0)


def _hg_scan(p, low_row, tc, reverse):
    n_b, ta, _ = p.shape
    tb = TOK_BLK
    n_blk = ta // tb
    nctx = tc // tb
    d_i = 1 if reverse else 0
    bi = functools.partial(_blk_index, n_blk=n_blk, nctx=nctx, reverse=reverse)
    return pl.pallas_call(
        functools.partial(_hg_kernel, reverse=reverse),
        grid=(n_b, n_blk),
        in_specs=[pl.BlockSpec((1, tb, 1024), lambda b, n: (b, bi(n), C_HQ // 1024)),
                  pl.BlockSpec((1, tb, 1024), lambda b, n: (b, bi(n), C_HF // 1024 + d_i)),
                  pl.BlockSpec((1, tb, 1024), lambda b, n: (b, bi(n), C_HI // 1024)),
                  pl.BlockSpec((1, 1024), lambda b, n: (0, 0))],
        out_specs=pl.BlockSpec((1, tb, 1024), lambda b, n: (b, bi(n), 0)),
        out_shape=jax.ShapeDtypeStruct((n_b, ta, 1024), BF16),
        scratch_shapes=[pltpu.VMEM((HG_HEADS, HG_DV, HG_DK), F32),
                        pltpu.VMEM((tb, 1024), F32),
                        pltpu.VMEM((tb, 1024), F32),
                        pltpu.VMEM((tb, 1024), F32)],
        compiler_params=_cparams(("arbitrary", "arbitrary")),
        name="hg_bwd" if reverse else "hg_fwd",
    )(p, p, p, low_row)


def _branch(of_ref, ob_ref, gate_ref, nw_ref, dv):
    o = of_ref[0].astype(F32) + ob_ref[0].astype(F32)
    gate = gate_ref[0].astype(F32)
    nw = nw_ref[...]
    parts = []
    for h in range(1024 // dv):
        x = o[:, h * dv:(h + 1) * dv]
        y = x * lax.rsqrt(jnp.mean(x * x, axis=-1, keepdims=True) + NORM_EPS) * nw
        parts.append(y)
    return (jnp.concatenate(parts, axis=-1) * _silu(gate)).astype(BF16)


def _merge_kernel(dnf, dnb, rtf, rtb, hgf, hgb, g_dn, g_rt, g_hg, m0, m1, m2, x_ref, mod_ref,
                  nw_dn, nw_rt, nw_hg, wb_ref, wo_ref, fnw_ref, o_ref, *, tc, blk_off, n_b, final):
    b = pl.program_id(0)
    i = pl.program_id(1) + blk_off
    d = D_MODEL
    brs = (_branch(dnf, dnb, g_dn, nw_dn, DN_DV),
           _branch(rtf, rtb, g_rt, nw_rt, RET_DV),
           _branch(hgf, hgb, g_hg, nw_hg, HG_DV))
    y = None
    for idx, (br, mg) in enumerate(zip(brs, (m0, m1, m2))):
        t = jnp.dot(br, wb_ref[idx], preferred_element_type=F32) * _sigmoid(mg[0].astype(F32))
        y = t if y is None else y + t
    out = jnp.dot(y.astype(BF16), wo_ref[...], preferred_element_type=F32)
    is_ctx = (i * TOK_BLK) < tc
    gt = jnp.where(is_ctx, mod_ref[n_b:n_b + 1, 2 * d:3 * d], mod_ref[pl.ds(b, 1), 2 * d:3 * d])
    xn = x_ref[0] + gt * out
    if final:
        xn = xn * lax.rsqrt(jnp.mean(xn * xn, axis=-1, keepdims=True) + NORM_EPS) * fnw_ref[...]
    o_ref[0] = xn


def _merge(outs, p, xa, mod_l, nws, wb, wo, fnw, tc, final):
    n_b, ta, d = xa.shape
    tb = TOK_BLK
    blk_off = tc // tb if final else 0
    n_blk = ta // tb - blk_off
    rows = lambda b, i: (b, i + blk_off, 0)
    pcol = lambda cb: (lambda b, i: (b, i + blk_off, cb))
    const2 = lambda b, i: (0, 0)
    in_specs = ([pl.BlockSpec((1, tb, 1024), rows)] * 6
                + [pl.BlockSpec((1, tb, 1024), pcol(C_DNGATE // 1024)),
                   pl.BlockSpec((1, tb, 1024), pcol(C_RGATE // 1024)),
                   pl.BlockSpec((1, tb, 1024), pcol(C_HGATE // 1024)),
                   pl.BlockSpec((1, tb, 1024), pcol(C_MERGE // 1024)),
                   pl.BlockSpec((1, tb, 1024), pcol(C_MERGE // 1024 + 1)),
                   pl.BlockSpec((1, tb, 1024), pcol(C_MERGE // 1024 + 2)),
                   pl.BlockSpec((1, tb, d), rows),
                   pl.BlockSpec((8, 3 * d), const2),
                   pl.BlockSpec((1, DN_DV), const2),
                   pl.BlockSpec((1, RET_DV), const2),
                   pl.BlockSpec((1, HG_DV), const2),
                   pl.BlockSpec((N_BRANCH, 1024, d), lambda b, i: (0, 0, 0)),
                   pl.BlockSpec((d, d), const2),
                   pl.BlockSpec((1, d), const2)])
    out_rows = ta - blk_off * tb
    return pl.pallas_call(
        functools.partial(_merge_kernel, tc=tc, blk_off=blk_off, n_b=n_b, final=final),
        grid=(n_b, n_blk),
        in_specs=in_specs,
        out_specs=pl.BlockSpec((1, tb, d), lambda b, i: (b, i, 0)),
        out_shape=jax.ShapeDtypeStruct((n_b, out_rows, d), F32),
        compiler_params=_cparams(("arbitrary", "arbitrary")),
        name="merge_final" if final else "merge",
    )(*outs, p, p, p, p, p, p, xa, mod_l, *nws, wb, wo, fnw)


def _permute_w_in(w):
    d = w.shape[0]
    o = 0
    dn_qkv = w[:, o:o + 2048]; o += 2048
    dn_gate = w[:, o:o + 1024]; o += 1024
    dn_a = w[:, o:o + 8]; o += 8
    dn_b = w[:, o:o + 8]; o += 8
    r_q = w[:, o:o + 512]; o += 512
    r_k = w[:, o:o + 512]; o += 512
    rest = w[:, o:]
    deint = lambda t: t.reshape(d, RET_HEADS, RET_DK // 2, 2).transpose(0, 1, 3, 2).reshape(d, 512)
    wp = jnp.concatenate([dn_qkv, dn_gate, deint(r_q), deint(r_k), rest], axis=1).astype(BF16)
    wab = jnp.concatenate([dn_a, dn_b, jnp.zeros((d, LANES - 16), w.dtype)], axis=1).astype(BF16)
    return wp, wab


def _rope_tables(n_lat, tc):
    half = RET_DK // 2
    inv = ROPE_BASE ** (-np.arange(0, half, 2, dtype=np.float64) / half)
    t = np.arange(n_lat)
    row = (t // GRID_W).astype(np.float64)
    col = (t % GRID_W).astype(np.float64)
    ang = np.concatenate([row[:, None] * inv, col[:, None] * inv], axis=-1)
    cos, sin = np.cos(ang), np.sin(ang)
    cos_l = np.concatenate([cos, cos], axis=-1)
    sin_l = np.concatenate([-sin, sin], axis=-1)
    cos_all = np.concatenate([np.ones((tc, LANES)), cos_l], axis=0).astype(np.float32)
    sin_all = np.concatenate([np.zeros((tc, LANES)), sin_l], axis=0).astype(np.float32)
    return jnp.asarray(cos_all), jnp.asarray(sin_all)


def _pad_row(v, width=LANES):
    v = v.reshape(1, -1).astype(F32)
    return jnp.pad(v, ((0, 0), (0, width - v.shape[1])))


def kernel(x, c, ctx, c_ctx, norm_w, ada_w, ada_b, w_in, dn_conv, dn_a_log, dn_dt_bias, dn_norm_w,
           ret_decay, ret_norm_w, hg_lb, hg_norm_w, w_branch, w_out, final_norm_w):
    n_b, n_lat, d = x.shape
    tc = ctx.shape[1]
    depth = norm_w.shape[0]
    assert d == D_MODEL and tc % TOK_BLK == 0 and n_lat % TOK_BLK == 0 and n_b + 1 <= 8
    assert n_lat % GRID_W == 0

    xa = jnp.concatenate([ctx, x], axis=1).astype(F32)
    cin = jnp.concatenate([c, c_ctx[None, :], jnp.zeros((8 - n_b - 1, d), c.dtype)], axis=0).astype(F32)
    mod = _modulation(cin, ada_w.astype(F32), ada_b.astype(F32))
    lower = _hg_lower(hg_lb)
    cos_t, sin_t = _rope_tables(n_lat, tc)
    fnw = final_norm_w.reshape(1, d).astype(F32)

    for l in range(depth):
        final = l == depth - 1
        wp, wab = _permute_w_in(w_in[l])
        p, pab = _project(xa, mod[l], norm_w[l].reshape(1, d).astype(F32), wp, wab, tc)
        dq, dk, dkt, dv, rq, rkt, gcol, grow = _prep(
            p, pab, dn_conv[l].astype(F32), cos_t, sin_t, _pad_row(dn_a_log[l]), _pad_row(dn_dt_bias[l]), tc)
        lam_row = _pad_row(ret_decay[l])
        outs = []
        for reverse in (False, True):
            outs.append(_dn_scan(dq, dk, dkt, dv, gcol, grow, tc, reverse))
        for reverse in (False, True):
            outs.append(_ret_scan(rq, rkt, p, lam_row, tc, reverse))
        for reverse in (False, True):
            outs.append(_hg_scan(p, lower[l, 1 if reverse else 0].reshape(1, -1), tc, reverse))
        nws = (dn_norm_w[l].reshape(1, -1).astype(F32), ret_norm_w[l].reshape(1, -1).astype(F32),
               hg_norm_w[l].reshape(1, -1).astype(F32))
        xa = _merge(outs, p, xa, mod[l], nws, w_branch[l].astype(BF16), w_out[l].astype(BF16), fnw, tc, final)
    return xa
```

```python
import functools
import math

import jax
import jax.numpy as jnp
import numpy as np
from jax import lax
from jax.experimental import pallas as pl
from jax.experimental.pallas import tpu as pltpu

F32 = jnp.float32
BF16 = jnp.bfloat16

D_MODEL = 1024
GRID_W = 64
DN_HEADS, DN_DK, DN_DV = 4, 128, 256
RET_HEADS, RET_DK, RET_DV = 4, 128, 256
HG_HEADS, HG_DK, HG_DV = 8, 128, 128
CONV_W = 5
ROPE_BASE = 10000.0
NORM_EPS = 1e-6
N_BRANCH = 3

LANES = 128
TOK_BLK = 256
DN_CHUNK = 64
HG_CHUNK = 64
NEG_BIG = -1e30

C_DNQKV = 0
C_DNGATE = 2048
C_RQ = 3072
C_RK = 3584
C_RV = 4096
C_RGATE = 5120
C_HQ = 6144
C_HF = 7168
C_HI = 9216
C_HGATE = 10240
C_MERGE = 11264
P_WIDTH = 14336
VMEM_LIMIT = 56 * 1024 * 1024


def _dot(a, b):
    return jnp.dot(a.astype(BF16), b.astype(BF16), preferred_element_type=F32)


def _dot_nt(a, b):
    return lax.dot_general(a.astype(BF16), b.astype(BF16), (((1,), (1,)), ((), ())),
                           preferred_element_type=F32)


def _dot_tn(a, b):
    return lax.dot_general(a.astype(BF16), b.astype(BF16), (((0,), (0,)), ((), ())),
                           preferred_element_type=F32)


def _sigmoid(x):
    return 1.0 / (1.0 + jnp.exp(-x))


def _silu(x):
    return x * _sigmoid(x)


def _softplus(x):
    return jnp.maximum(x, 0.0) + jnp.log(1.0 + jnp.exp(-jnp.abs(x)))


def _iota(shape, dim):
    return lax.broadcasted_iota(jnp.int32, shape, dim)


def _shr(x, s):
    return lax.shift_right_logical(x, int(s).bit_length() - 1)


def _cparams(sem):
    return pltpu.CompilerParams(dimension_semantics=sem, vmem_limit_bytes=VMEM_LIMIT)


def _mod_kernel(cin_ref, w_ref, b_ref, o_ref):
    s = _silu(cin_ref[...])
    o_ref[0] = _dot(s, w_ref[0]) + b_ref[0]


def _modulation(cin, ada_w, ada_b):
    depth, d, d3 = ada_w.shape
    tn = 1024
    return pl.pallas_call(
        _mod_kernel,
        grid=(depth, d3 // tn),
        in_specs=[pl.BlockSpec((8, d), lambda l, j: (0, 0)),
                  pl.BlockSpec((1, d, tn), lambda l, j: (l, 0, j)),
                  pl.BlockSpec((1, 1, tn), lambda l, j: (l, 0, j))],
        out_specs=pl.BlockSpec((1, 8, tn), lambda l, j: (l, 0, j)),
        out_shape=jax.ShapeDtypeStruct((depth, 8, d3), F32),
        compiler_params=_cparams(("arbitrary", "arbitrary")),
        name="ada_mod",
    )(cin, ada_w, ada_b.reshape(depth, 1, d3))


def _lower_kernel(lb_ref, o_ref, *, depth):
    xs = [lb_ref[l] for l in range(depth)]
    m = xs[0]
    for l in range(1, depth):
        m = jnp.maximum(m, xs[l])
    es = [jnp.exp(x - m) for x in xs]
    tot = es[0]
    for l in range(1, depth):
        tot = tot + es[l]
    sm = [e / tot for e in es]
    acc = sm[0]
    o_ref[0] = acc - sm[0]
    for l in range(1, depth):
        acc = acc + sm[l]
        o_ref[l] = acc - sm[0]


def _hg_lower(hg_lb):
    depth = hg_lb.shape[0]
    lb = hg_lb.astype(F32).reshape(depth, 2, HG_HEADS * HG_DK)
    return pl.pallas_call(
        functools.partial(_lower_kernel, depth=depth),
        out_shape=jax.ShapeDtypeStruct(lb.shape, F32),
        name="hg_lower",
    )(lb)


def _proj_kernel(x_ref, mod_ref, nw_ref, w_ref, wab_ref, p_ref, pab_ref, hb_ref, *, tc, tm, tn, n_b):
    b = pl.program_id(0)
    i = pl.program_id(1)
    j = pl.program_id(2)
    d = D_MODEL

    @pl.when(j == 0)
    def _():
        x = x_ref[0]
        ms = jnp.mean(x * x, axis=-1, keepdims=True)
        y = x * lax.rsqrt(ms + NORM_EPS) * nw_ref[...]
        row = i * tm + _iota((tm, 1), 0)
        is_ctx = row < tc
        sh = jnp.where(is_ctx, mod_ref[n_b:n_b + 1, 0:d], mod_ref[pl.ds(b, 1), 0:d])
        sc = jnp.where(is_ctx, mod_ref[n_b:n_b + 1, d:2 * d], mod_ref[pl.ds(b, 1), d:2 * d])
        hb = (y * (1.0 + sc) + sh).astype(BF16)
        hb_ref[...] = hb
        pab_ref[0] = jnp.dot(hb, wab_ref[...], preferred_element_type=F32)

    sub = 512
    for s in range(tn // sub):
        p_ref[0, :, s * sub:(s + 1) * sub] = jnp.dot(
            hb_ref[...], w_ref[:, s * sub:(s + 1) * sub], preferred_element_type=F32).astype(BF16)


def _row_tile(ta):
    best = 16
    for t in range(16, 1101, 16):
        if ta % t == 0:
            best = t
    return best


def _project(xa, mod_l, nw, wp, wab, tc):
    n_b, ta, d = xa.shape
    tm = _row_tile(ta)
    tn = 2048
    return pl.pallas_call(
        functools.partial(_proj_kernel, tc=tc, tm=tm, tn=tn, n_b=n_b),
        grid=(n_b, ta // tm, P_WIDTH // tn),
        in_specs=[pl.BlockSpec((1, tm, d), lambda b, i, j: (b, i, 0)),
                  pl.BlockSpec((8, 3 * d), lambda b, i, j: (0, 0)),
                  pl.BlockSpec((1, d), lambda b, i, j: (0, 0)),
                  pl.BlockSpec((d, tn), lambda b, i, j: (0, j)),
                  pl.BlockSpec((d, LANES), lambda b, i, j: (0, 0))],
        out_specs=[pl.BlockSpec((1, tm, tn), lambda b, i, j: (b, i, j)),
                   pl.BlockSpec((1, tm, LANES), lambda b, i, j: (b, i, 0))],
        out_shape=[jax.ShapeDtypeStruct((n_b, ta, P_WIDTH), BF16),
                   jax.ShapeDtypeStruct((n_b, ta, LANES), F32)],
        scratch_shapes=[pltpu.VMEM((tm, d), BF16)],
        compiler_params=_cparams(("arbitrary", "arbitrary", "arbitrary")),
        name="in_proj",
    )(xa, mod_l, nw, wp, wab)


def _prep_kernel(main_ref, prev_ref, next_ref, rqk_ref, pab_ref, cw_ref, cos_ref, sin_ref, alog_ref, dtb_ref,
                 dq_ref, dk_ref, dkt_ref, dv_ref, rq_ref, rkt_ref, gcol_ref, grow_ref, dktf_ref, growf_ref,
                 xs_ref, *, tc, n_blk):
    i = pl.program_id(1)
    tb = TOK_BLK
    nctx = tc // tb
    has_prev = jnp.logical_and(i != 0, i != nctx)
    has_next = jnp.logical_and(i != nctx - 1, i != n_blk - 1)
    pm = jnp.where(has_prev, 1.0, 0.0)
    nm = jnp.where(has_next, 1.0, 0.0)
    n_ch = tb // DN_CHUNK

    for g in range(16):
        ls = slice(g * LANES, (g + 1) * LANES)
        xs_ref[8:8 + tb, :] = main_ref[0, :, ls].astype(F32)
        xs_ref[0:8, :] = prev_ref[0, 8:16, ls].astype(F32) * pm
        xs_ref[8 + tb:16 + tb, :] = next_ref[0, 0:8, ls].astype(F32) * nm
        acc = cw_ref[0:1, ls] * xs_ref[6:6 + tb, :]
        for t in range(1, CONV_W):
            acc = acc + cw_ref[t:t + 1, ls] * xs_ref[6 + t:6 + t + tb, :]
        y = _silu(acc)
        if g < 8:
            y = y * lax.rsqrt(jnp.sum(y * y, axis=-1, keepdims=True) + NORM_EPS)
        if g < 4:
            dq_ref[0, :, ls] = (y * (DN_DK ** -0.5)).astype(BF16)
        elif g < 8:
            h = g - 4
            hs = slice(h * LANES, (h + 1) * LANES)
            dk_ref[0, :, hs] = y.astype(BF16)
            dktf_ref[0, h] = y.T.astype(BF16)
            for c in range(n_ch):
                dkt_ref[0, h, c] = y[c * DN_CHUNK:(c + 1) * DN_CHUNK, :].T.astype(BF16)
        else:
            vs = slice((g - 8) * LANES, (g - 7) * LANES)
            dv_ref[0, :, vs] = y.astype(BF16)

    pab = pab_ref[0]
    g_log = -jnp.exp(alog_ref[...]) * _softplus(pab + dtb_ref[...])
    beta = _sigmoid(pab)
    r = _iota((tb, tb), 0)
    c_ = _iota((tb, tb), 1)
    same = _shr(r, DN_CHUNK) == _shr(c_, DN_CHUNK)
    m_f = jnp.where(jnp.logical_and(same, r >= c_), 1.0, 0.0)
    m_b = jnp.where(jnp.logical_and(same, r <= c_), 1.0, 0.0)
    cs_f = jnp.dot(m_f, g_log, precision=lax.Precision.HIGHEST, preferred_element_type=F32)
    cs_b = jnp.dot(m_b, g_log, precision=lax.Precision.HIGHEST, preferred_element_type=F32)
    lane = _iota((tb, LANES), 1)
    gcol = jnp.where(lane < 4, cs_f, jnp.where(lane < 8, cs_b, beta))
    gcol_ref[0] = gcol
    growf_ref[0] = gcol.T[0:8, :]
    for c in range(n_ch):
        grow_ref[0, c] = gcol[c * DN_CHUNK:(c + 1) * DN_CHUNK, :].T[0:8, :]

    cosv = cos_ref[...]
    sinv = sin_ref[...]
    for g in range(8):
        ls = slice(g * LANES, (g + 1) * LANES)
        x = rqk_ref[0, :, ls].astype(F32)
        y = x * cosv + pltpu.roll(x, LANES // 2, 1) * sinv
        if g < 4:
            rq_ref[0, :, ls] = y.astype(BF16)
        else:
            rkt_ref[0, g - 4] = (y * (RET_DK ** -0.5)).T.astype(BF16)


def _prep(p, pab, conv_w, cos_t, sin_t, alog_row, dtb_row, tc):
    n_b, ta, _ = p.shape
    tb = TOK_BLK
    n_blk = ta // tb
    n_ch = tb // DN_CHUNK
    hpb = tb // 16
    n16 = ta // 16
    out_shape = [
        jax.ShapeDtypeStruct((n_b, ta, 512), BF16),
        jax.ShapeDtypeStruct((n_b, ta, 512), BF16),
        jax.ShapeDtypeStruct((n_b, DN_HEADS, ta // DN_CHUNK, DN_DK, DN_CHUNK), BF16),
        jax.ShapeDtypeStruct((n_b, ta, 1024), BF16),
        jax.ShapeDtypeStruct((n_b, ta, 512), BF16),
        jax.ShapeDtypeStruct((n_b, RET_HEADS, RET_DK, ta), BF16),
        jax.ShapeDtypeStruct((n_b, ta, LANES), F32),
        jax.ShapeDtypeStruct((n_b, ta // DN_CHUNK, 8, DN_CHUNK), F32),
        jax.ShapeDtypeStruct((n_b, DN_HEADS, DN_DK, ta), BF16),
        jax.ShapeDtypeStruct((n_b, 8, ta), F32),
    ]
    out_specs = [
        pl.BlockSpec((1, tb, 512), lambda b, i: (b, i, 0)),
        pl.BlockSpec((1, tb, 512), lambda b, i: (b, i, 0)),
        pl.BlockSpec((1, DN_HEADS, n_ch, DN_DK, DN_CHUNK), lambda b, i: (b, 0, i, 0, 0)),
        pl.BlockSpec((1, tb, 1024), lambda b, i: (b, i, 0)),
        pl.BlockSpec((1, tb, 512), lambda b, i: (b, i, 0)),
        pl.BlockSpec((1, RET_HEADS, RET_DK, tb), lambda b, i: (b, 0, 0, i)),
        pl.BlockSpec((1, tb, LANES), lambda b, i: (b, i, 0)),
        pl.BlockSpec((1, n_ch, 8, DN_CHUNK), lambda b, i: (b, i, 0, 0)),
        pl.BlockSpec((1, DN_HEADS, DN_DK, tb), lambda b, i: (b, 0, 0, i)),
        pl.BlockSpec((1, 8, tb), lambda b, i: (b, 0, i)),
    ]
    in_specs = [
        pl.BlockSpec((1, tb, 2048), lambda b, i: (b, i, 0)),
        pl.BlockSpec((1, 16, 2048), lambda b, i: (b, jnp.maximum(i * hpb - 1, 0), 0)),
        pl.BlockSpec((1, 16, 2048), lambda b, i: (b, jnp.minimum((i + 1) * hpb, n16 - 1), 0)),
        pl.BlockSpec((1, tb, 1024), lambda b, i: (b, i, C_RQ // 1024)),
        pl.BlockSpec((1, tb, LANES), lambda b, i: (b, i, 0)),
        pl.BlockSpec((CONV_W, 2048), lambda b, i: (0, 0)),
        pl.BlockSpec((tb, LANES), lambda b, i: (i, 0)),
        pl.BlockSpec((tb, LANES), lambda b, i: (i, 0)),
        pl.BlockSpec((1, LANES), lambda b, i: (0, 0)),
        pl.BlockSpec((1, LANES), lambda b, i: (0, 0)),
    ]
    return pl.pallas_call(
        functools.partial(_prep_kernel, tc=tc, n_blk=n_blk),
        grid=(n_b, n_blk),
        in_specs=in_specs,
        out_specs=out_specs,
        out_shape=out_shape,
        scratch_shapes=[pltpu.VMEM((tb + 16, LANES), F32)],
        compiler_params=_cparams(("arbitrary", "arbitrary")),
        name="prep",
    )(p, p, p, p, pab, conv_w, cos_t, sin_t, alog_row, dtb_row)


def _blk_index(n, n_blk, nctx, reverse):
    if not reverse:
        return n
    return jnp.where(n < nctx, nctx - 1 - n, n_blk - 1 - (n - nctx))


DN_MASK_NEG, DN_MASK_STRICT, DN_MASK_B8, DN_MASK_EYE, DN_MASK_OFF0 = 0, 1, 2, 3, 4
DN_MERGE_SIZES = (8, 16, 32)


def _dn_masks(reverse):
    r = np.arange(TOK_BLK)[:, None]
    c = np.arange(TOK_BLK)[None, :]
    same = (r // DN_CHUNK) == (c // DN_CHUNK)
    incl = same & ((r <= c) if reverse else (r >= c))
    strict = incl & (r != c)
    ms = [np.where(incl, 0.0, NEG_BIG), strict, (r // 8) == (c // 8), r == c]
    for s in DN_MERGE_SIZES:
        ms.append(((r // (2 * s)) == (c // (2 * s))) & ((r // s) != (c // s)))
    return jnp.asarray(np.stack([np.asarray(m, np.float32) for m in ms]))


def _dn_kernel(q_ref, k_ref, ktf_ref, ktc_ref, v_ref, gcol_ref, growf_ref, growc_ref, m_ref, o_ref, s_ref,
               *, reverse):
    n = pl.program_id(1)
    c_len = DN_CHUNK
    n_ch = TOK_BLK // c_len
    d_off = 4 if reverse else 0
    heads = range(DN_HEADS)
    last = 0 if reverse else c_len - 1

    @pl.when(n == 0)
    def _():
        s_ref[...] = jnp.zeros_like(s_ref)

    gcol = gcol_ref[0]
    growf = growf_ref[0]
    gc = [gcol[:, d_off + h:d_off + h + 1] for h in heads]
    bc = [gcol[:, 8 + d_off + h:9 + d_off + h] for h in heads]
    egc = [jnp.exp(x) for x in gc]
    q = [q_ref[0, :, h * DN_DK:(h + 1) * DN_DK] for h in heads]
    kb = [k_ref[0, :, h * DN_DK:(h + 1) * DN_DK].astype(F32) * bc[h] for h in heads]

    dec = [jnp.exp((gc[h] - growf[d_off + h:d_off + h + 1, :]) + m_ref[DN_MASK_NEG]) for h in heads]
    a = [_dot(kb[h], ktf_ref[0, h]) * dec[h] * m_ref[DN_MASK_STRICT] for h in heads]
    d = [(a[h] * m_ref[DN_MASK_B8]).astype(BF16) for h in heads]
    d2 = [jnp.dot(d[h], d[h], preferred_element_type=F32) for h in heads]
    d2b = [x.astype(BF16) for x in d2]
    d4 = [jnp.dot(d2b[h], d2b[h], preferred_element_type=F32) for h in heads]
    d3 = [jnp.dot(d[h], d2b[h], preferred_element_type=F32) for h in heads]
    p1 = [m_ref[DN_MASK_EYE] - d[h].astype(F32) + d2[h] - d3[h] for h in heads]
    t = [p1[h] + _dot(p1[h], d4[h]) for h in heads]
    for lvl in range(len(DN_MERGE_SIZES)):
        tb16 = [x.astype(BF16) for x in t]
        x = [_dot(a[h] * m_ref[DN_MASK_OFF0 + lvl], tb16[h]) for h in heads]
        t = [t[h] - _dot(tb16[h], x[h]) for h in heads]
    rhs = [jnp.concatenate([kb[h] * egc[h], v_ref[0, :, h * DN_DV:(h + 1) * DN_DV].astype(F32) * bc[h]], axis=1)
           for h in heads]
    wu = [_dot(t[h], rhs[h]) for h in heads]
    qd = [q[h].astype(F32) * egc[h] for h in heads]

    neg64 = m_ref[DN_MASK_NEG, 0:c_len, 0:c_len]
    attn, kdt, egl = {}, {}, {}
    for c in range(n_ch):
        rs = slice(c * c_len, (c + 1) * c_len)
        growc = growc_ref[0, c]
        for h in heads:
            gr = growc[d_off + h:d_off + h + 1, :]
            gl = gr[:, last:last + 1]
            kt = ktc_ref[0, h, c]
            attn[c, h] = _dot(q[h][rs], kt) * jnp.exp((gc[h][rs] - gr) + neg64)
            kdt[c, h] = kt.astype(F32) * jnp.exp(gl - gr)
            egl[c, h] = jnp.exp(gl)

    state = [s_ref[h] for h in heads]
    for cc in range(n_ch):
        c = (n_ch - 1 - cc) if reverse else cc
        rs = slice(c * c_len, (c + 1) * c_len)
        ws = [_dot(jnp.concatenate([wu[h][rs, 0:DN_DK], qd[h][rs]], axis=0), state[h]) for h in heads]
        vn = [wu[h][rs, DN_DK:DN_DK + DN_DV] - ws[h][0:c_len] for h in heads]
        o = [ws[h][c_len:2 * c_len] + _dot(attn[c, h], vn[h]) for h in heads]
        state = [state[h] * egl[c, h] + _dot(kdt[c, h], vn[h]) for h in heads]
        for h in heads:
            o_ref[0, rs, h * DN_DV:(h + 1) * DN_DV] = o[h].astype(BF16)
    for h in heads:
        s_ref[h] = state[h]


def _dn_scan(dq, dk, dktf, dktc, dv, gcol, growf, growc, tc, reverse):
    n_b, ta, _ = dq.shape
    tb = TOK_BLK
    n_blk = ta // tb
    nctx = tc // tb
    n_ch = tb // DN_CHUNK
    masks = _dn_masks(reverse)
    bi = functools.partial(_blk_index, n_blk=n_blk, nctx=nctx, reverse=reverse)
    return pl.pallas_call(
        functools.partial(_dn_kernel, reverse=reverse),
        grid=(n_b, n_blk),
        in_specs=[pl.BlockSpec((1, tb, 512), lambda b, n: (b, bi(n), 0)),
                  pl.BlockSpec((1, tb, 512), lambda b, n: (b, bi(n), 0)),
                  pl.BlockSpec((1, DN_HEADS, DN_DK, tb), lambda b, n: (b, 0, 0, bi(n))),
                  pl.BlockSpec((1, DN_HEADS, n_ch, DN_DK, DN_CHUNK), lambda b, n: (b, 0, bi(n), 0, 0)),
                  pl.BlockSpec((1, tb, 1024), lambda b, n: (b, bi(n), 0)),
                  pl.BlockSpec((1, tb, LANES), lambda b, n: (b, bi(n), 0)),
                  pl.BlockSpec((1, 8, tb), lambda b, n: (b, 0, bi(n))),
                  pl.BlockSpec((1, n_ch, 8, DN_CHUNK), lambda b, n: (b, bi(n), 0, 0)),
                  pl.BlockSpec(masks.shape, lambda b, n: (0, 0, 0))],
        out_specs=pl.BlockSpec((1, tb, 1024), lambda b, n: (b, bi(n), 0)),
        out_shape=jax.ShapeDtypeStruct((n_b, ta, 1024), BF16),
        scratch_shapes=[pltpu.VMEM((DN_HEADS, DN_DK, DN_DV), F32)],
        compiler_params=_cparams(("arbitrary", "arbitrary")),
        name="dn_bwd" if reverse else "dn_fwd",
    )(dq, dk, dktf, dktc, dv, gcol, growf, growc, masks)


def _ret_kernel(q_ref, kt_ref, v_ref, lam_ref, o_ref, s_ref, dec_ref, eq_ref, ek_ref, *, reverse):
    n = pl.program_id(1)
    c_len = TOK_BLK
    d_off = 4 if reverse else 0

    @pl.when(n == 0)
    def _():
        s_ref[...] = jnp.zeros_like(s_ref)
        ri = _iota((c_len, c_len), 0)
        ci = _iota((c_len, c_len), 1)
        dist = (ci - ri) if reverse else (ri - ci)
        rowpos = _iota((c_len, LANES), 0)
        colpos = _iota((RET_DK, c_len), 1)
        if reverse:
            qexp = (c_len - rowpos).astype(F32)
            kexp = colpos.astype(F32)
        else:
            qexp = (rowpos + 1).astype(F32)
            kexp = (c_len - 1 - colpos).astype(F32)
        for h in range(RET_HEADS):
            x = lam_ref[0:1, d_off + h:d_off + h + 1]
            lam = jnp.minimum(x, 0.0) - jnp.log(1.0 + jnp.exp(-jnp.abs(x)))
            dec_ref[h] = jnp.exp(jnp.where(dist >= 0, dist.astype(F32) * lam, NEG_BIG))
            eq_ref[h] = jnp.exp(qexp * lam)
            ek_ref[h] = jnp.exp(kexp * lam)

    for h in range(RET_HEADS):
        hs = slice(h * RET_DK, (h + 1) * RET_DK)
        vs = slice(h * RET_DV, (h + 1) * RET_DV)
        q = q_ref[0, :, hs]
        kt = kt_ref[0, h]
        v = v_ref[0, :, vs]
        x = lam_ref[0:1, d_off + h:d_off + h + 1]
        lam = jnp.minimum(x, 0.0) - jnp.log(1.0 + jnp.exp(-jnp.abs(x)))
        attn = _dot(q, kt) * dec_ref[h]
        s_old = s_ref[h]
        o = _dot(attn, v) + _dot(q.astype(F32) * eq_ref[h], s_old)
        s_ref[h] = s_old * jnp.exp(lam * float(c_len)) + _dot(kt.astype(F32) * ek_ref[h], v)
        o_ref[0, :, vs] = o.astype(BF16)


def _ret_scan(rq, rkt, p, lam_row, tc, reverse):
    n_b, ta, _ = rq.shape
    tb = TOK_BLK
    n_blk = ta // tb
    nctx = tc // tb
    bi = functools.partial(_blk_index, n_blk=n_blk, nctx=nctx, reverse=reverse)
    return pl.pallas_call(
        functools.partial(_ret_kernel, reverse=reverse),
        grid=(n_b, n_blk),
        in_specs=[pl.BlockSpec((1, tb, 512), lambda b, n: (b, bi(n), 0)),
                  pl.BlockSpec((1, RET_HEADS, RET_DK, tb), lambda b, n: (b, 0, 0, bi(n))),
                  pl.BlockSpec((1, tb, 1024), lambda b, n: (b, bi(n), C_RV // 1024)),
                  pl.BlockSpec((1, LANES), lambda b, n: (0, 0))],
        out_specs=pl.BlockSpec((1, tb, 1024), lambda b, n: (b, bi(n), 0)),
        out_shape=jax.ShapeDtypeStruct((n_b, ta, 1024), BF16),
        scratch_shapes=[pltpu.VMEM((RET_HEADS, RET_DK, RET_DV), F32),
                        pltpu.VMEM((RET_HEADS, tb, tb), F32),
                        pltpu.VMEM((RET_HEADS, tb, LANES), F32),
                        pltpu.VMEM((RET_HEADS, RET_DK, tb), F32)],
        compiler_params=_cparams(("arbitrary", "arbitrary")),
        name="ret_bwd" if reverse else "ret_fwd",
    )(rq, rkt, p, lam_row)


HG_LEVELS = (1, 2, 4, 8, 16, 32)


def _neg_abs(x):
    bits = lax.bitcast_convert_type(x, jnp.uint32) | jnp.uint32(0x80000000)
    return lax.bitcast_convert_type(bits, F32)


def _hg_pair_masks(reverse):
    i = np.arange(HG_CHUNK)[:, None]
    j = np.arange(HG_CHUNK)[None, :]
    ms = []
    for s in HG_LEVELS:
        q_half = 0 if reverse else 1
        ms.append(((i // (2 * s)) == (j // (2 * s))) & (((i // s) & 1) == q_half) & (((j // s) & 1) == 1 - q_half))
    ms.append(i == j)
    return jnp.asarray(np.stack(ms).astype(np.float32))


def _hg_boundary(gc_s, gc, base, hs, s, row8, reverse):
    c_len = HG_CHUNK
    off = s - 1 if reverse else s
    if s == 1:
        rowi = _iota((c_len, HG_DK), 0)
        if reverse:
            return jnp.where((rowi & 1) == 1, pltpu.roll(gc, 1, 0), gc)
        return jnp.where((rowi & 1) == 0, pltpu.roll(gc, c_len - 1, 0), gc)
    parts = []
    for vi in range(c_len // 8):
        r0 = base + 8 * vi
        if s == 2:
            lo = jnp.broadcast_to(gc_s[r0 + off:r0 + off + 1, hs], (8, HG_DK))
            hi = jnp.broadcast_to(gc_s[r0 + 4 + off:r0 + 5 + off, hs], (8, HG_DK))
            parts.append(jnp.where(row8 < 4, lo, hi))
        else:
            m = base + ((8 * vi) // (2 * s)) * (2 * s) + off
            parts.append(jnp.broadcast_to(gc_s[m:m + 1, hs], (8, HG_DK)))
    return jnp.concatenate(parts, axis=0)


def _hg_kernel(xq_ref, xf_ref, xi_ref, low_ref, pm_ref, o_ref, st_ref, q_s, k_s, gc_s, *, reverse):
    n = pl.program_id(1)
    tb = TOK_BLK
    c_len = HG_CHUNK
    n_ch = tb // c_len

    @pl.when(n == 0)
    def _():
        st_ref[...] = jnp.zeros_like(st_ref)

    low = low_ref[...]
    f = low + (1.0 - low) * _sigmoid(xf_ref[0].astype(F32))
    g = jnp.log2(f)
    r = _iota((tb, tb), 0)
    c_ = _iota((tb, tb), 1)
    same = _shr(r, c_len) == _shr(c_, c_len)
    tri = (r <= c_) if reverse else (r >= c_)
    m = jnp.where(jnp.logical_and(same, tri), 1.0, 0.0).astype(BF16)
    g0 = g.astype(BF16)
    r1 = g - g0.astype(F32)
    g1 = r1.astype(BF16)
    g2 = (r1 - g1.astype(F32)).astype(BF16)
    gc_s[...] = (jnp.dot(m, g0, preferred_element_type=F32) + jnp.dot(m, g1, preferred_element_type=F32)
                 + jnp.dot(m, g2, preferred_element_type=F32))
    k_s[...] = (1.0 - f).astype(BF16)
    q_s[...] = _silu(xq_ref[0].astype(F32)).astype(BF16)

    heads = range(HG_HEADS)
    row8 = _iota((8, HG_DK), 0)
    last = 0 if reverse else c_len - 1
    for cc in range(n_ch):
        c = (n_ch - 1 - cc) if reverse else cc
        base = c * c_len
        rs = slice(base, base + c_len)
        hsl = [slice(h * HG_DK, (h + 1) * HG_DK) for h in heads]
        q = [q_s[rs, hsl[h]] for h in heads]
        k = [k_s[rs, hsl[h]] for h in heads]
        gc = [gc_s[rs, hsl[h]] for h in heads]
        a = [_dot_nt(q[h], k[h]) * pm_ref[len(HG_LEVELS)] for h in heads]
        for lvl, s in enumerate(HG_LEVELS):
            e = [jnp.exp2(_neg_abs(gc[h] - _hg_boundary(gc_s, gc[h], base, hsl[h], s, row8, reverse))).astype(BF16)
                 for h in heads]
            pr = [_dot_nt(q[h] * e[h], k[h] * e[h]) for h in heads]
            a = [a[h] + pr[h] * pm_ref[lvl] for h in heads]
        gl = [gc[h][last:last + 1, :] for h in heads]
        st = [st_ref[h] for h in heads]
        o = [_dot(a[h], xi_ref[0, rs, hsl[h]]) + _dot_nt(q[h] * jnp.exp2(gc[h]).astype(BF16), st[h]) for h in heads]
        for h in heads:
            kd = k[h] * jnp.exp2(gl[h] - gc[h]).astype(BF16)
            st_ref[h] = st[h] * jnp.exp2(gl[h]) + _dot_tn(xi_ref[0, rs, hsl[h]], kd)
            o_ref[0, rs, hsl[h]] = o[h].astype(BF16)


def _hg_scan(p, low_row, tc, reverse):
    n_b, ta, _ = p.shape
    tb = TOK_BLK
    n_blk = ta // tb
    nctx = tc // tb
    d_i = 1 if reverse else 0
    masks = _hg_pair_masks(reverse)
    bi = functools.partial(_blk_index, n_blk=n_blk, nctx=nctx, reverse=reverse)
    return pl.pallas_call(
        functools.partial(_hg_kernel, reverse=reverse),
        grid=(n_b, n_blk),
        in_specs=[pl.BlockSpec((1, tb, 1024), lambda b, n: (b, bi(n), C_HQ // 1024)),
                  pl.BlockSpec((1, tb, 1024), lambda b, n: (b, bi(n), C_HF // 1024 + d_i)),
                  pl.BlockSpec((1, tb, 1024), lambda b, n: (b, bi(n), C_HI // 1024)),
                  pl.BlockSpec((1, 1024), lambda b, n: (0, 0)),
                  pl.BlockSpec(masks.shape, lambda b, n: (0, 0, 0))],
        out_specs=pl.BlockSpec((1, tb, 1024), lambda b, n: (b, bi(n), 0)),
        out_shape=jax.ShapeDtypeStruct((n_b, ta, 1024), BF16),
        scratch_shapes=[pltpu.VMEM((HG_HEADS, HG_DV, HG_DK), F32),
                        pltpu.VMEM((tb, 1024), BF16),
                        pltpu.VMEM((tb, 1024), BF16),
                        pltpu.VMEM((tb, 1024), F32)],
        compiler_params=_cparams(("arbitrary", "arbitrary")),
        name="hg_bwd" if reverse else "hg_fwd",
    )(p, p, p, low_row, masks)


def _branch(of_ref, ob_ref, gate_ref, nw_ref, dv):
    o = of_ref[0].astype(F32) + ob_ref[0].astype(F32)
    gate = gate_ref[0].astype(F32)
    nw = nw_ref[...]
    parts = []
    for h in range(1024 // dv):
        x = o[:, h * dv:(h + 1) * dv]
        y = x * lax.rsqrt(jnp.mean(x * x, axis=-1, keepdims=True) + NORM_EPS) * nw
        parts.append(y)
    return (jnp.concatenate(parts, axis=-1) * _silu(gate)).astype(BF16)


def _merge_kernel(dnf, dnb, rtf, rtb, hgf, hgb, g_dn, g_rt, g_hg, m0, m1, m2, x_ref, mod_ref,
                  nw_dn, nw_rt, nw_hg, wb_ref, wo_ref, fnw_ref, o_ref, *, tc, blk_off, n_b, final):
    b = pl.program_id(0)
    i = pl.program_id(1) + blk_off
    d = D_MODEL
    brs = (_branch(dnf, dnb, g_dn, nw_dn, DN_DV),
           _branch(rtf, rtb, g_rt, nw_rt, RET_DV),
           _branch(hgf, hgb, g_hg, nw_hg, HG_DV))
    y = None
    for idx, (br, mg) in enumerate(zip(brs, (m0, m1, m2))):
        t = jnp.dot(br, wb_ref[idx], preferred_element_type=F32) * _sigmoid(mg[0].astype(F32))
        y = t if y is None else y + t
    out = jnp.dot(y.astype(BF16), wo_ref[...], preferred_element_type=F32)
    is_ctx = (i * TOK_BLK) < tc
    gt = jnp.where(is_ctx, mod_ref[n_b:n_b + 1, 2 * d:3 * d], mod_ref[pl.ds(b, 1), 2 * d:3 * d])
    xn = x_ref[0] + gt * out
    if final:
        xn = xn * lax.rsqrt(jnp.mean(xn * xn, axis=-1, keepdims=True) + NORM_EPS) * fnw_ref[...]
    o_ref[0] = xn


def _merge(outs, p, xa, mod_l, nws, wb, wo, fnw, tc, final):
    n_b, ta, d = xa.shape
    tb = TOK_BLK
    blk_off = tc // tb if final else 0
    n_blk = ta // tb - blk_off
    rows = lambda b, i: (b, i + blk_off, 0)
    pcol = lambda cb: (lambda b, i: (b, i + blk_off, cb))
    const2 = lambda b, i: (0, 0)
    in_specs = ([pl.BlockSpec((1, tb, 1024), rows)] * 6
                + [pl.BlockSpec((1, tb, 1024), pcol(C_DNGATE // 1024)),
                   pl.BlockSpec((1, tb, 1024), pcol(C_RGATE // 1024)),
                   pl.BlockSpec((1, tb, 1024), pcol(C_HGATE // 1024)),
                   pl.BlockSpec((1, tb, 1024), pcol(C_MERGE // 1024)),
                   pl.BlockSpec((1, tb, 1024), pcol(C_MERGE // 1024 + 1)),
                   pl.BlockSpec((1, tb, 1024), pcol(C_MERGE // 1024 + 2)),
                   pl.BlockSpec((1, tb, d), rows),
                   pl.BlockSpec((8, 3 * d), const2),
                   pl.BlockSpec((1, DN_DV), const2),
                   pl.BlockSpec((1, RET_DV), const2),
                   pl.BlockSpec((1, HG_DV), const2),
                   pl.BlockSpec((N_BRANCH, 1024, d), lambda b, i: (0, 0, 0)),
                   pl.BlockSpec((d, d), const2),
                   pl.BlockSpec((1, d), const2)])
    out_rows = ta - blk_off * tb
    return pl.pallas_call(
        functools.partial(_merge_kernel, tc=tc, blk_off=blk_off, n_b=n_b, final=final),
        grid=(n_b, n_blk),
        in_specs=in_specs,
        out_specs=pl.BlockSpec((1, tb, d), lambda b, i: (b, i, 0)),
        out_shape=jax.ShapeDtypeStruct((n_b, out_rows, d), F32),
        compiler_params=_cparams(("arbitrary", "arbitrary")),
        name="merge_final" if final else "merge",
    )(*outs, p, p, p, p, p, p, xa, mod_l, *nws, wb, wo, fnw)


def _permute_w_in(w):
    d = w.shape[0]
    o = 0
    dn_qkv = w[:, o:o + 2048]; o += 2048
    dn_gate = w[:, o:o + 1024]; o += 1024
    dn_a = w[:, o:o + 8]; o += 8
    dn_b = w[:, o:o + 8]; o += 8
    r_q = w[:, o:o + 512]; o += 512
    r_k = w[:, o:o + 512]; o += 512
    rest = w[:, o:]
    deint = lambda t: t.reshape(d, RET_HEADS, RET_DK // 2, 2).transpose(0, 1, 3, 2).reshape(d, 512)
    wp = jnp.concatenate([dn_qkv, dn_gate, deint(r_q), deint(r_k), rest], axis=1).astype(BF16)
    wab = jnp.concatenate([dn_a, dn_b, jnp.zeros((d, LANES - 16), w.dtype)], axis=1).astype(BF16)
    return wp, wab


def _rope_tables(n_lat, tc):
    half = RET_DK // 2
    inv = ROPE_BASE ** (-np.arange(0, half, 2, dtype=np.float64) / half)
    t = np.arange(n_lat)
    row = (t // GRID_W).astype(np.float64)
    col = (t % GRID_W).astype(np.float64)
    ang = np.concatenate([row[:, None] * inv, col[:, None] * inv], axis=-1)
    cos, sin = np.cos(ang), np.sin(ang)
    cos_l = np.concatenate([cos, cos], axis=-1)
    sin_l = np.concatenate([-sin, sin], axis=-1)
    cos_all = np.concatenate([np.ones((tc, LANES)), cos_l], axis=0).astype(np.float32)
    sin_all = np.concatenate([np.zeros((tc, LANES)), sin_l], axis=0).astype(np.float32)
    return jnp.asarray(cos_all), jnp.asarray(sin_all)


def _pad_row(v, width=LANES):
    v = v.reshape(1, -1).astype(F32)
    return jnp.pad(v, ((0, 0), (0, width - v.shape[1])))


def kernel(x, c, ctx, c_ctx, norm_w, ada_w, ada_b, w_in, dn_conv, dn_a_log, dn_dt_bias, dn_norm_w,
           ret_decay, ret_norm_w, hg_lb, hg_norm_w, w_branch, w_out, final_norm_w):
    n_b, n_lat, d = x.shape
    tc = ctx.shape[1]
    depth = norm_w.shape[0]
    assert d == D_MODEL and tc % TOK_BLK == 0 and n_lat % TOK_BLK == 0 and n_b + 1 <= 8
    assert n_lat % GRID_W == 0

    xa = jnp.concatenate([ctx, x], axis=1).astype(F32)
    cin = jnp.concatenate([c, c_ctx[None, :], jnp.zeros((8 - n_b - 1, d), c.dtype)], axis=0).astype(F32)
    mod = _modulation(cin, ada_w.astype(F32), ada_b.astype(F32))
    lower = _hg_lower(hg_lb)
    cos_t, sin_t = _rope_tables(n_lat, tc)
    fnw = final_norm_w.reshape(1, d).astype(F32)

    for l in range(depth):
        final = l == depth - 1
        wp, wab = _permute_w_in(w_in[l])
        p, pab = _project(xa, mod[l], norm_w[l].reshape(1, d).astype(F32), wp, wab, tc)
        dq, dk, dkt, dv, rq, rkt, gcol, grow, dktf, growf = _prep(
            p, pab, dn_conv[l].astype(F32), cos_t, sin_t, _pad_row(dn_a_log[l]), _pad_row(dn_dt_bias[l]), tc)
        lam_row = _pad_row(ret_decay[l])
        outs = []
        for reverse in (False, True):
            outs.append(_dn_scan(dq, dk, dktf, dkt, dv, gcol, growf, grow, tc, reverse))
        for reverse in (False, True):
            outs.append(_ret_scan(rq, rkt, p, lam_row, tc, reverse))
        for reverse in (False, True):
            outs.append(_hg_scan(p, lower[l, 1 if reverse else 0].reshape(1, -1), tc, reverse))
        nws = (dn_norm_w[l].reshape(1, -1).astype(F32), ret_norm_w[l].reshape(1, -1).astype(F32),
               hg_norm_w[l].reshape(1, -1).astype(F32))
        xa = _merge(outs, p, xa, mod[l], nws, w_branch[l].astype(BF16), w_out[l].astype(BF16), fnw, tc, final)
    return xa
```

```python
import functools
import math

import jax
import jax.numpy as jnp
import numpy as np
from jax import lax
from jax.experimental import pallas as pl
from jax.experimental.pallas import tpu as pltpu

F32 = jnp.float32
BF16 = jnp.bfloat16

D_MODEL = 1024
GRID_W = 64
DN_HEADS, DN_DK, DN_DV = 4, 128, 256
RET_HEADS, RET_DK, RET_DV = 4, 128, 256
HG_HEADS, HG_DK, HG_DV = 8, 128, 128
CONV_W = 5
ROPE_BASE = 10000.0
NORM_EPS = 1e-6
N_BRANCH = 3

LANES = 128
TOK_BLK = 256
DN_CHUNK = 64
HG_CHUNK = 64
NEG_BIG = -1e30

C_DNQKV = 0
C_RQ = 2048
C_RK = 2560
C_RV = 3072
C_HI = 4096
C_MERGE2 = 5120
C_DNGATE = 6144
C_RGATE = 7168
C_HGATE = 8192
C_HQ = 9216
C_HF = 10240
C_MERGE0 = 12288
C_MERGE1 = 13312
P_WIDTH = 14336
PROJ_TILE = 2048
PROJ_SUB = 512
PROJ_ACT = ("NNNN", "NNNN", "NNGG", "SSSS", "SSSS", "GGGG", "GGGG")
NEG_LOG2E = -1.0 / math.log(2.0)
VMEM_LIMIT = 56 * 1024 * 1024


def _dot(a, b):
    return jnp.dot(a.astype(BF16), b.astype(BF16), preferred_element_type=F32)


def _dot_nt(a, b):
    return lax.dot_general(a.astype(BF16), b.astype(BF16), (((1,), (1,)), ((), ())),
                           preferred_element_type=F32)


def _dot_tn(a, b):
    return lax.dot_general(a.astype(BF16), b.astype(BF16), (((0,), (0,)), ((), ())),
                           preferred_element_type=F32)


def _sigmoid(x):
    return 1.0 / (1.0 + jnp.exp2(x * NEG_LOG2E))


def _silu(x):
    return x * _sigmoid(x)


def _softplus(x):
    return jnp.maximum(x, 0.0) + jnp.log(1.0 + jnp.exp(-jnp.abs(x)))


def _iota(shape, dim):
    return lax.broadcasted_iota(jnp.int32, shape, dim)


def _shr(x, s):
    return lax.shift_right_logical(x, int(s).bit_length() - 1)


def _cparams(sem):
    return pltpu.CompilerParams(dimension_semantics=sem, vmem_limit_bytes=VMEM_LIMIT)


def _mod_kernel(cin_ref, w_ref, b_ref, o_ref):
    s = _silu(cin_ref[...])
    o_ref[0] = _dot(s, w_ref[0]) + b_ref[0]


def _modulation(cin, ada_w, ada_b):
    depth, d, d3 = ada_w.shape
    tn = 1024
    return pl.pallas_call(
        _mod_kernel,
        grid=(depth, d3 // tn),
        in_specs=[pl.BlockSpec((8, d), lambda l, j: (0, 0)),
                  pl.BlockSpec((1, d, tn), lambda l, j: (l, 0, j)),
                  pl.BlockSpec((1, 1, tn), lambda l, j: (l, 0, j))],
        out_specs=pl.BlockSpec((1, 8, tn), lambda l, j: (l, 0, j)),
        out_shape=jax.ShapeDtypeStruct((depth, 8, d3), F32),
        compiler_params=_cparams(("arbitrary", "arbitrary")),
        name="ada_mod",
    )(cin, ada_w, ada_b.reshape(depth, 1, d3))


def _lower_kernel(lb_ref, o_ref, *, depth):
    xs = [lb_ref[l] for l in range(depth)]
    m = xs[0]
    for l in range(1, depth):
        m = jnp.maximum(m, xs[l])
    es = [jnp.exp(x - m) for x in xs]
    tot = es[0]
    for l in range(1, depth):
        tot = tot + es[l]
    sm = [e / tot for e in es]
    acc = sm[0]
    o_ref[0] = acc - sm[0]
    for l in range(1, depth):
        acc = acc + sm[l]
        o_ref[l] = acc - sm[0]


def _hg_lower(hg_lb):
    depth = hg_lb.shape[0]
    lb = hg_lb.astype(F32).reshape(depth, 2, HG_HEADS * HG_DK)
    return pl.pallas_call(
        functools.partial(_lower_kernel, depth=depth),
        out_shape=jax.ShapeDtypeStruct(lb.shape, F32),
        name="hg_lower",
    )(lb)


def _proj_kernel(x_ref, mod_ref, nw_ref, w_ref, wab_ref, p_ref, pab_ref, hb_ref, *, tc, tm, tn, n_b):
    b = pl.program_id(0)
    i = pl.program_id(1)
    j = pl.program_id(2)
    d = D_MODEL

    @pl.when(j == 0)
    def _():
        x = x_ref[0]
        ms = jnp.mean(x * x, axis=-1, keepdims=True)
        y = x * lax.rsqrt(ms + NORM_EPS) * nw_ref[...]
        row = i * tm + _iota((tm, 1), 0)
        is_ctx = row < tc
        sh = jnp.where(is_ctx, mod_ref[n_b:n_b + 1, 0:d], mod_ref[pl.ds(b, 1), 0:d])
        sc = jnp.where(is_ctx, mod_ref[n_b:n_b + 1, d:2 * d], mod_ref[pl.ds(b, 1), d:2 * d])
        hb = (y * (1.0 + sc) + sh).astype(BF16)
        hb_ref[...] = hb
        pab_ref[0] = jnp.dot(hb, wab_ref[...], preferred_element_type=F32)

    sub = PROJ_SUB
    for pattern in sorted(set(PROJ_ACT)):
        tiles = [t for t, pat in enumerate(PROJ_ACT) if pat == pattern]
        cond = j == tiles[0]
        for t in tiles[1:]:
            cond = jnp.logical_or(cond, j == t)

        @pl.when(cond)
        def _(pattern=pattern):
            for s in range(tn // sub):
                r = jnp.dot(hb_ref[...], w_ref[:, s * sub:(s + 1) * sub], preferred_element_type=F32)
                if pattern[s] == "S":
                    r = _silu(r)
                elif pattern[s] == "G":
                    r = _sigmoid(r)
                p_ref[0, :, s * sub:(s + 1) * sub] = r.astype(BF16)


def _row_tile(ta):
    best = 16
    for t in range(16, 1101, 16):
        if ta % t == 0:
            best = t
    return best


def _project(xa, mod_l, nw, wp, wab, tc):
    n_b, ta, d = xa.shape
    tm = _row_tile(ta)
    tn = PROJ_TILE
    assert len(PROJ_ACT) * tn == P_WIDTH and all(len(pat) * PROJ_SUB == tn for pat in PROJ_ACT)
    return pl.pallas_call(
        functools.partial(_proj_kernel, tc=tc, tm=tm, tn=tn, n_b=n_b),
        grid=(n_b, ta // tm, P_WIDTH // tn),
        in_specs=[pl.BlockSpec((1, tm, d), lambda b, i, j: (b, i, 0)),
                  pl.BlockSpec((8, 3 * d), lambda b, i, j: (0, 0)),
                  pl.BlockSpec((1, d), lambda b, i, j: (0, 0)),
                  pl.BlockSpec((d, tn), lambda b, i, j: (0, j)),
                  pl.BlockSpec((d, LANES), lambda b, i, j: (0, 0))],
        out_specs=[pl.BlockSpec((1, tm, tn), lambda b, i, j: (b, i, j)),
                   pl.BlockSpec((1, tm, LANES), lambda b, i, j: (b, i, 0))],
        out_shape=[jax.ShapeDtypeStruct((n_b, ta, P_WIDTH), BF16),
                   jax.ShapeDtypeStruct((n_b, ta, LANES), F32)],
        scratch_shapes=[pltpu.VMEM((tm, d), BF16)],
        compiler_params=_cparams(("arbitrary", "arbitrary", "arbitrary")),
        name="in_proj",
    )(xa, mod_l, nw, wp, wab)


def _prep_kernel(main_ref, prev_ref, next_ref, rqk_ref, pab_ref, cw_ref, cos_ref, sin_ref, alog_ref, dtb_ref,
                 dq_ref, dk_ref, dkt_ref, dv_ref, rq_ref, rkt_ref, gcol_ref, grow_ref, dktf_ref, growf_ref,
                 xs_ref, *, tc, n_blk):
    i = pl.program_id(1)
    tb = TOK_BLK
    nctx = tc // tb
    has_prev = jnp.logical_and(i != 0, i != nctx)
    has_next = jnp.logical_and(i != nctx - 1, i != n_blk - 1)
    pm = jnp.where(has_prev, 1.0, 0.0)
    nm = jnp.where(has_next, 1.0, 0.0)
    n_ch = tb // DN_CHUNK

    for g in range(16):
        ls = slice(g * LANES, (g + 1) * LANES)
        xs_ref[8:8 + tb, :] = main_ref[0, :, ls].astype(F32)
        xs_ref[0:8, :] = prev_ref[0, 8:16, ls].astype(F32) * pm
        xs_ref[8 + tb:16 + tb, :] = next_ref[0, 0:8, ls].astype(F32) * nm
        acc = cw_ref[0:1, ls] * xs_ref[6:6 + tb, :]
        for t in range(1, CONV_W):
            acc = acc + cw_ref[t:t + 1, ls] * xs_ref[6 + t:6 + t + tb, :]
        y = _silu(acc)
        if g < 8:
            y = y * lax.rsqrt(jnp.sum(y * y, axis=-1, keepdims=True) + NORM_EPS)
        if g < 4:
            dq_ref[0, :, ls] = (y * (DN_DK ** -0.5)).astype(BF16)
        elif g < 8:
            h = g - 4
            hs = slice(h * LANES, (h + 1) * LANES)
            dk_ref[0, :, hs] = y.astype(BF16)
            dktf_ref[0, h] = y.T.astype(BF16)
            for c in range(n_ch):
                dkt_ref[0, h, c] = y[c * DN_CHUNK:(c + 1) * DN_CHUNK, :].T.astype(BF16)
        else:
            vs = slice((g - 8) * LANES, (g - 7) * LANES)
            dv_ref[0, :, vs] = y.astype(BF16)

    pab = pab_ref[0]
    g_log = -jnp.exp(alog_ref[...]) * _softplus(pab + dtb_ref[...])
    beta = _sigmoid(pab)
    r = _iota((tb, tb), 0)
    c_ = _iota((tb, tb), 1)
    same = _shr(r, DN_CHUNK) == _shr(c_, DN_CHUNK)
    m_f = jnp.where(jnp.logical_and(same, r >= c_), 1.0, 0.0)
    m_b = jnp.where(jnp.logical_and(same, r <= c_), 1.0, 0.0)
    cs_f = jnp.dot(m_f, g_log, precision=lax.Precision.HIGHEST, preferred_element_type=F32)
    cs_b = jnp.dot(m_b, g_log, precision=lax.Precision.HIGHEST, preferred_element_type=F32)
    lane = _iota((tb, LANES), 1)
    gcol = jnp.where(lane < 4, cs_f, jnp.where(lane < 8, cs_b, beta))
    gcol_ref[0] = gcol
    growf_ref[0] = gcol.T[0:8, :]
    for c in range(n_ch):
        grow_ref[0, c] = gcol[c * DN_CHUNK:(c + 1) * DN_CHUNK, :].T[0:8, :]

    cosv = cos_ref[...]
    sinv = sin_ref[...]
    for g in range(8):
        ls = slice(g * LANES, (g + 1) * LANES)
        x = rqk_ref[0, :, ls].astype(F32)
        y = x * cosv + pltpu.roll(x, LANES // 2, 1) * sinv
        if g < 4:
            rq_ref[0, :, ls] = y.astype(BF16)
        else:
            rkt_ref[0, g - 4] = (y * (RET_DK ** -0.5)).T.astype(BF16)


def _prep(p, pab, conv_w, cos_t, sin_t, alog_row, dtb_row, tc):
    n_b, ta, _ = p.shape
    tb = TOK_BLK
    n_blk = ta // tb
    n_ch = tb // DN_CHUNK
    hpb = tb // 16
    n16 = ta // 16
    out_shape = [
        jax.ShapeDtypeStruct((n_b, ta, 512), BF16),
        jax.ShapeDtypeStruct((n_b, ta, 512), BF16),
        jax.ShapeDtypeStruct((n_b, DN_HEADS, ta // DN_CHUNK, DN_DK, DN_CHUNK), BF16),
        jax.ShapeDtypeStruct((n_b, ta, 1024), BF16),
        jax.ShapeDtypeStruct((n_b, ta, 512), BF16),
        jax.ShapeDtypeStruct((n_b, RET_HEADS, RET_DK, ta), BF16),
        jax.ShapeDtypeStruct((n_b, ta, LANES), F32),
        jax.ShapeDtypeStruct((n_b, ta // DN_CHUNK, 8, DN_CHUNK), F32),
        jax.ShapeDtypeStruct((n_b, DN_HEADS, DN_DK, ta), BF16),
        jax.ShapeDtypeStruct((n_b, 8, ta), F32),
    ]
    out_specs = [
        pl.BlockSpec((1, tb, 512), lambda b, i: (b, i, 0)),
        pl.BlockSpec((1, tb, 512), lambda b, i: (b, i, 0)),
        pl.BlockSpec((1, DN_HEADS, n_ch, DN_DK, DN_CHUNK), lambda b, i: (b, 0, i, 0, 0)),
        pl.BlockSpec((1, tb, 1024), lambda b, i: (b, i, 0)),
        pl.BlockSpec((1, tb, 512), lambda b, i: (b, i, 0)),
        pl.BlockSpec((1, RET_HEADS, RET_DK, tb), lambda b, i: (b, 0, 0, i)),
        pl.BlockSpec((1, tb, LANES), lambda b, i: (b, i, 0)),
        pl.BlockSpec((1, n_ch, 8, DN_CHUNK), lambda b, i: (b, i, 0, 0)),
        pl.BlockSpec((1, DN_HEADS, DN_DK, tb), lambda b, i: (b, 0, 0, i)),
        pl.BlockSpec((1, 8, tb), lambda b, i: (b, 0, i)),
    ]
    in_specs = [
        pl.BlockSpec((1, tb, 2048), lambda b, i: (b, i, 0)),
        pl.BlockSpec((1, 16, 2048), lambda b, i: (b, jnp.maximum(i * hpb - 1, 0), 0)),
        pl.BlockSpec((1, 16, 2048), lambda b, i: (b, jnp.minimum((i + 1) * hpb, n16 - 1), 0)),
        pl.BlockSpec((1, tb, 1024), lambda b, i: (b, i, C_RQ // 1024)),
        pl.BlockSpec((1, tb, LANES), lambda b, i: (b, i, 0)),
        pl.BlockSpec((CONV_W, 2048), lambda b, i: (0, 0)),
        pl.BlockSpec((tb, LANES), lambda b, i: (i, 0)),
        pl.BlockSpec((tb, LANES), lambda b, i: (i, 0)),
        pl.BlockSpec((1, LANES), lambda b, i: (0, 0)),
        pl.BlockSpec((1, LANES), lambda b, i: (0, 0)),
    ]
    return pl.pallas_call(
        functools.partial(_prep_kernel, tc=tc, n_blk=n_blk),
        grid=(n_b, n_blk),
        in_specs=in_specs,
        out_specs=out_specs,
        out_shape=out_shape,
        scratch_shapes=[pltpu.VMEM((tb + 16, LANES), F32)],
        compiler_params=_cparams(("arbitrary", "arbitrary")),
        name="prep",
    )(p, p, p, p, pab, conv_w, cos_t, sin_t, alog_row, dtb_row)


def _blk_index(n, n_blk, nctx, reverse):
    if not reverse:
        return n
    return jnp.where(n < nctx, nctx - 1 - n, n_blk - 1 - (n - nctx))


DN_MASK_NEG, DN_MASK_STRICT, DN_MASK_B8, DN_MASK_EYE, DN_MASK_OFF0 = 0, 1, 2, 3, 4
DN_MERGE_SIZES = (8, 16, 32)


DN_PAIR = 2 * DN_CHUNK


def _dn_masks(reverse):
    r = np.arange(DN_PAIR)[:, None]
    c = np.arange(DN_PAIR)[None, :]
    same = (r // DN_CHUNK) == (c // DN_CHUNK)
    incl = same & ((r <= c) if reverse else (r >= c))
    strict = incl & (r != c)
    ms = [np.where(incl, 0.0, NEG_BIG), strict, (r // 8) == (c // 8), r == c]
    for s in DN_MERGE_SIZES:
        ms.append(((r // (2 * s)) == (c // (2 * s))) & ((r // s) != (c // s)))
    return jnp.asarray(np.stack([np.asarray(m, np.float32) for m in ms]))


def _dn_kernel(q_ref, k_ref, ktf_ref, ktc_ref, v_ref, gcol_ref, growf_ref, growc_ref, m_ref, o_ref, s_ref,
               wq_s, u_s, attn_s, kdt_s, egl_s, *, reverse):
    n = pl.program_id(1)
    c_len = DN_CHUNK
    n_ch = TOK_BLK // c_len
    d_off = 4 if reverse else 0
    heads = range(DN_HEADS)
    last = 0 if reverse else c_len - 1
    pairs = [slice(g * DN_PAIR, (g + 1) * DN_PAIR) for g in range(TOK_BLK // DN_PAIR)]
    units = [(h, g) for h in heads for g in range(len(pairs))]

    wr = lax.rem(n, 2)
    rd = 1 - wr

    @pl.when(n == 0)
    def _():
        s_ref[...] = jnp.zeros_like(s_ref)
        wq_s[...] = jnp.zeros_like(wq_s)
        u_s[...] = jnp.zeros_like(u_s)
        attn_s[...] = jnp.zeros_like(attn_s)
        kdt_s[...] = jnp.zeros_like(kdt_s)
        egl_s[...] = jnp.zeros_like(egl_s)

    us = range(len(units))
    pa = {}

    def a_load():
        gcol = gcol_ref[0]
        growf = growf_ref[0]
        pa["gc"] = [gcol[:, d_off + h:d_off + h + 1] for h in heads]
        pa["bc"] = [gcol[:, 8 + d_off + h:9 + d_off + h] for h in heads]
        pa["egc"] = [jnp.exp(x) for x in pa["gc"]]
        pa["kb"] = [k_ref[0, :, h * DN_DK:(h + 1) * DN_DK].astype(F32) * pa["bc"][h] for h in heads]
        dec = [jnp.exp((pa["gc"][h][pairs[g]] - growf[d_off + h:d_off + h + 1, pairs[g]]) + m_ref[DN_MASK_NEG])
               for h, g in units]
        pa["a"] = [_dot(pa["kb"][h][pairs[g]], ktf_ref[0, h, :, pairs[g]]) * dec[u] * m_ref[DN_MASK_STRICT]
                   for u, (h, g) in enumerate(units)]

    def a_sq():
        pa["d"] = [(pa["a"][u] * m_ref[DN_MASK_B8]).astype(BF16) for u in us]
        pa["d2"] = [jnp.dot(pa["d"][u], pa["d"][u], preferred_element_type=F32) for u in us]

    def a_pow():
        d2b = [x.astype(BF16) for x in pa["d2"]]
        pa["d4"] = [jnp.dot(d2b[u], d2b[u], preferred_element_type=F32) for u in us]
        pa["d3"] = [jnp.dot(pa["d"][u], d2b[u], preferred_element_type=F32) for u in us]

    def a_base():
        p1 = [m_ref[DN_MASK_EYE] - pa["d"][u].astype(F32) + pa["d2"][u] - pa["d3"][u] for u in us]
        pa["t"] = [p1[u] + _dot(p1[u], pa["d4"][u]) for u in us]

    def a_merge_x(lvl):
        def run():
            pa["tb"] = [x.astype(BF16) for x in pa["t"]]
            pa["x"] = [_dot(pa["a"][u] * m_ref[DN_MASK_OFF0 + lvl], pa["tb"][u]) for u in us]
        return run

    def a_merge_t():
        pa["t"] = [pa["t"][u] - _dot(pa["tb"][u], pa["x"][u]) for u in us]

    def a_store():
        neg64 = m_ref[DN_MASK_NEG, 0:c_len, 0:c_len]
        for u, (h, g) in enumerate(units):
            rows = pairs[g]
            rhs = jnp.concatenate([pa["kb"][h][rows] * pa["egc"][h][rows],
                                   v_ref[0, rows, h * DN_DV:(h + 1) * DN_DV].astype(F32) * pa["bc"][h][rows]], axis=1)
            wu = _dot(pa["t"][u], rhs)
            qd = q_ref[0, rows, h * DN_DK:(h + 1) * DN_DK].astype(F32) * pa["egc"][h][rows]
            u_s[wr, h, rows, :] = wu[:, DN_DK:DN_DK + DN_DV]
            for ci in range(DN_PAIR // c_len):
                c = g * (DN_PAIR // c_len) + ci
                cs = slice(ci * c_len, (ci + 1) * c_len)
                wq_s[wr, h, c] = jnp.concatenate([wu[cs, 0:DN_DK], qd[cs]], axis=0).astype(BF16)
        for c in range(n_ch):
            rs = slice(c * c_len, (c + 1) * c_len)
            growc = growc_ref[0, c]
            for h in heads:
                gr = growc[d_off + h:d_off + h + 1, :]
                gl = gr[:, last:last + 1]
                kt = ktc_ref[0, h, c]
                attn = _dot(q_ref[0, rs, h * DN_DK:(h + 1) * DN_DK], kt) * jnp.exp((pa["gc"][h][rs] - gr) + neg64)
                attn_s[wr, h, c] = attn.astype(BF16)
                kdt_s[wr, h, c] = (kt.astype(F32) * jnp.exp(gl - gr)).astype(BF16)
                egl_s[wr, h, c] = jnp.broadcast_to(jnp.exp(gl), (8, LANES))

    pb = {"state": [s_ref[h] for h in heads]}

    def b_first(c):
        def run():
            pb["ws"] = [jnp.dot(wq_s[rd, h, c], pb["state"][h].astype(BF16), preferred_element_type=F32)
                        for h in heads]
        return run

    def b_second(c):
        def run():
            rs = slice(c * c_len, (c + 1) * c_len)
            vn = [(u_s[rd, h, rs, :] - pb["ws"][h][0:c_len]).astype(BF16) for h in heads]
            o = [pb["ws"][h][c_len:2 * c_len] + jnp.dot(attn_s[rd, h, c], vn[h], preferred_element_type=F32)
                 for h in heads]
            pb["state"] = [pb["state"][h] * egl_s[rd, h, c][0:1, 0:1]
                           + jnp.dot(kdt_s[rd, h, c], vn[h], preferred_element_type=F32) for h in heads]
            for h in heads:
                o_ref[0, rs, h * DN_DV:(h + 1) * DN_DV] = o[h].astype(BF16)
        return run

    a_stages = [a_load, a_sq, a_pow, a_base]
    for lvl in range(len(DN_MERGE_SIZES)):
        a_stages += [a_merge_x(lvl), a_merge_t]
    a_stages.append(a_store)
    b_stages = []
    for cc in range(n_ch):
        c = (n_ch - 1 - cc) if reverse else cc
        b_stages += [b_first(c), b_second(c)]
    for i in range(max(len(a_stages), len(b_stages))):
        if i < len(a_stages):
            a_stages[i]()
        if i < len(b_stages):
            b_stages[i]()
    for h in heads:
        s_ref[h] = pb["state"][h]


def _dn_scan(dq, dk, dktf, dktc, dv, gcol, growf, growc, tc, reverse):
    n_b, ta, _ = dq.shape
    tb = TOK_BLK
    n_blk = ta // tb
    nctx = tc // tb
    n_ch = tb // DN_CHUNK
    masks = _dn_masks(reverse)
    blk = functools.partial(_blk_index, n_blk=n_blk, nctx=nctx, reverse=reverse)
    bi = lambda n: blk(jnp.minimum(n, n_blk - 1))
    bo = lambda n: blk(jnp.maximum(n - 1, 0))
    return pl.pallas_call(
        functools.partial(_dn_kernel, reverse=reverse),
        grid=(n_b, n_blk + 1),
        in_specs=[pl.BlockSpec((1, tb, 512), lambda b, n: (b, bi(n), 0)),
                  pl.BlockSpec((1, tb, 512), lambda b, n: (b, bi(n), 0)),
                  pl.BlockSpec((1, DN_HEADS, DN_DK, tb), lambda b, n: (b, 0, 0, bi(n))),
                  pl.BlockSpec((1, DN_HEADS, n_ch, DN_DK, DN_CHUNK), lambda b, n: (b, 0, bi(n), 0, 0)),
                  pl.BlockSpec((1, tb, 1024), lambda b, n: (b, bi(n), 0)),
                  pl.BlockSpec((1, tb, LANES), lambda b, n: (b, bi(n), 0)),
                  pl.BlockSpec((1, 8, tb), lambda b, n: (b, 0, bi(n))),
                  pl.BlockSpec((1, n_ch, 8, DN_CHUNK), lambda b, n: (b, bi(n), 0, 0)),
                  pl.BlockSpec(masks.shape, lambda b, n: (0, 0, 0))],
        out_specs=pl.BlockSpec((1, tb, 1024), lambda b, n: (b, bo(n), 0)),
        out_shape=jax.ShapeDtypeStruct((n_b, ta, 1024), BF16),
        scratch_shapes=[pltpu.VMEM((DN_HEADS, DN_DK, DN_DV), F32),
                        pltpu.VMEM((2, DN_HEADS, n_ch, 2 * DN_CHUNK, DN_DK), BF16),
                        pltpu.VMEM((2, DN_HEADS, tb, DN_DV), F32),
                        pltpu.VMEM((2, DN_HEADS, n_ch, DN_CHUNK, DN_CHUNK), BF16),
                        pltpu.VMEM((2, DN_HEADS, n_ch, DN_DK, DN_CHUNK), BF16),
                        pltpu.VMEM((2, DN_HEADS, n_ch, 8, LANES), F32)],
        compiler_params=_cparams(("arbitrary", "arbitrary")),
        name="dn_bwd" if reverse else "dn_fwd",
    )(dq, dk, dktf, dktc, dv, gcol, growf, growc, masks)


def _ret_kernel(q_ref, kt_ref, v_ref, lam_ref, o_ref, s_ref, dec_ref, eq_ref, ek_ref, *, reverse):
    n = pl.program_id(1)
    c_len = TOK_BLK
    d_off = 4 if reverse else 0

    @pl.when(n == 0)
    def _():
        s_ref[...] = jnp.zeros_like(s_ref)
        ri = _iota((c_len, c_len), 0)
        ci = _iota((c_len, c_len), 1)
        dist = (ci - ri) if reverse else (ri - ci)
        rowpos = _iota((c_len, LANES), 0)
        colpos = _iota((RET_DK, c_len), 1)
        if reverse:
            qexp = (c_len - rowpos).astype(F32)
            kexp = colpos.astype(F32)
        else:
            qexp = (rowpos + 1).astype(F32)
            kexp = (c_len - 1 - colpos).astype(F32)
        for h in range(RET_HEADS):
            x = lam_ref[0:1, d_off + h:d_off + h + 1]
            lam = jnp.minimum(x, 0.0) - jnp.log(1.0 + jnp.exp(-jnp.abs(x)))
            dec_ref[h] = jnp.exp(jnp.where(dist >= 0, dist.astype(F32) * lam, NEG_BIG))
            eq_ref[h] = jnp.exp(qexp * lam)
            ek_ref[h] = jnp.exp(kexp * lam)

    for h in range(RET_HEADS):
        hs = slice(h * RET_DK, (h + 1) * RET_DK)
        vs = slice(h * RET_DV, (h + 1) * RET_DV)
        q = q_ref[0, :, hs]
        kt = kt_ref[0, h]
        v = v_ref[0, :, vs]
        x = lam_ref[0:1, d_off + h:d_off + h + 1]
        lam = jnp.minimum(x, 0.0) - jnp.log(1.0 + jnp.exp(-jnp.abs(x)))
        attn = _dot(q, kt) * dec_ref[h]
        s_old = s_ref[h]
        o = _dot(attn, v) + _dot(q.astype(F32) * eq_ref[h], s_old)
        s_ref[h] = s_old * jnp.exp(lam * float(c_len)) + _dot(kt.astype(F32) * ek_ref[h], v)
        o_ref[0, :, vs] = o.astype(BF16)


def _ret_scan(rq, rkt, p, lam_row, tc, reverse):
    n_b, ta, _ = rq.shape
    tb = TOK_BLK
    n_blk = ta // tb
    nctx = tc // tb
    bi = functools.partial(_blk_index, n_blk=n_blk, nctx=nctx, reverse=reverse)
    return pl.pallas_call(
        functools.partial(_ret_kernel, reverse=reverse),
        grid=(n_b, n_blk),
        in_specs=[pl.BlockSpec((1, tb, 512), lambda b, n: (b, bi(n), 0)),
                  pl.BlockSpec((1, RET_HEADS, RET_DK, tb), lambda b, n: (b, 0, 0, bi(n))),
                  pl.BlockSpec((1, tb, 1024), lambda b, n: (b, bi(n), C_RV // 1024)),
                  pl.BlockSpec((1, LANES), lambda b, n: (0, 0))],
        out_specs=pl.BlockSpec((1, tb, 1024), lambda b, n: (b, bi(n), 0)),
        out_shape=jax.ShapeDtypeStruct((n_b, ta, 1024), BF16),
        scratch_shapes=[pltpu.VMEM((RET_HEADS, RET_DK, RET_DV), F32),
                        pltpu.VMEM((RET_HEADS, tb, tb), F32),
                        pltpu.VMEM((RET_HEADS, tb, LANES), F32),
                        pltpu.VMEM((RET_HEADS, RET_DK, tb), F32)],
        compiler_params=_cparams(("arbitrary", "arbitrary")),
        name="ret_bwd" if reverse else "ret_fwd",
    )(rq, rkt, p, lam_row)


HG_LEVELS = (1, 2, 4, 8, 16, 32)
HG_GROUP = 8


def _neg_abs(x):
    bits = lax.bitcast_convert_type(x, jnp.uint32) | jnp.uint32(0x80000000)
    return lax.bitcast_convert_type(bits, F32)


def _hg_pair_masks(reverse):
    i = np.arange(HG_CHUNK)[:, None]
    j = np.arange(HG_CHUNK)[None, :]
    ms = []
    for s in HG_LEVELS:
        q_half = 0 if reverse else 1
        ms.append(((i // (2 * s)) == (j // (2 * s))) & (((i // s) & 1) == q_half) & (((j // s) & 1) == 1 - q_half))
    ms.append(i == j)
    return jnp.asarray(np.stack(ms).astype(np.float32))


def _hg_boundary(gc_s, gc, base, hs, s, row8, reverse):
    c_len = HG_CHUNK
    off = s - 1 if reverse else s
    if s == 1:
        rowi = _iota((c_len, HG_DK), 0)
        if reverse:
            return jnp.where((rowi & 1) == 1, pltpu.roll(gc, 1, 0), gc)
        return jnp.where((rowi & 1) == 0, pltpu.roll(gc, c_len - 1, 0), gc)
    parts = []
    for vi in range(c_len // 8):
        r0 = base + 8 * vi
        if s == 2:
            lo = jnp.broadcast_to(gc_s[r0 + off:r0 + off + 1, hs], (8, HG_DK))
            hi = jnp.broadcast_to(gc_s[r0 + 4 + off:r0 + 5 + off, hs], (8, HG_DK))
            parts.append(jnp.where(row8 < 4, lo, hi))
        else:
            m = base + ((8 * vi) // (2 * s)) * (2 * s) + off
            parts.append(jnp.broadcast_to(gc_s[m:m + 1, hs], (8, HG_DK)))
    return jnp.concatenate(parts, axis=0)


def _hg_kernel(xq_ref, xf_ref, xi_ref, low_ref, pm_ref, o_ref, st_ref, k_s, gc_s, *, reverse):
    n = pl.program_id(1)
    tb = TOK_BLK
    c_len = HG_CHUNK
    n_ch = tb // c_len

    @pl.when(n == 0)
    def _():
        st_ref[...] = jnp.zeros_like(st_ref)

    low = low_ref[...]
    f = low + (1.0 - low) * xf_ref[0].astype(F32)
    g = jnp.log2(f)
    r = _iota((tb, tb), 0)
    c_ = _iota((tb, tb), 1)
    same = _shr(r, c_len) == _shr(c_, c_len)
    tri = (r <= c_) if reverse else (r >= c_)
    m = jnp.where(jnp.logical_and(same, tri), 1.0, 0.0).astype(BF16)
    g0 = g.astype(BF16)
    r1 = g - g0.astype(F32)
    g1 = r1.astype(BF16)
    g2 = (r1 - g1.astype(F32)).astype(BF16)
    gc_s[...] = (jnp.dot(m, g0, preferred_element_type=F32) + jnp.dot(m, g1, preferred_element_type=F32)
                 + jnp.dot(m, g2, preferred_element_type=F32))
    k_s[...] = (1.0 - f).astype(BF16)

    row8 = _iota((8, HG_DK), 0)
    last = 0 if reverse else c_len - 1
    hsl = [slice(h * HG_DK, (h + 1) * HG_DK) for h in range(HG_HEADS)]
    for cc in range(n_ch):
        c = (n_ch - 1 - cc) if reverse else cc
        base = c * c_len
        rs = slice(base, base + c_len)
        for h0 in range(0, HG_HEADS, HG_GROUP):
            heads = range(h0, h0 + HG_GROUP)
            q = {h: xq_ref[0, rs, hsl[h]] for h in heads}
            k = {h: k_s[rs, hsl[h]] for h in heads}
            gc = {h: gc_s[rs, hsl[h]] for h in heads}
            a = {h: _dot_nt(q[h], k[h]) * pm_ref[len(HG_LEVELS)] for h in heads}
            for lvl, s in enumerate(HG_LEVELS):
                e = {h: jnp.exp2(_neg_abs(gc[h] - _hg_boundary(gc_s, gc[h], base, hsl[h], s, row8, reverse))
                                 ).astype(BF16) for h in heads}
                pr = {h: _dot_nt(q[h] * e[h], k[h] * e[h]) for h in heads}
                a = {h: a[h] + pr[h] * pm_ref[lvl] for h in heads}
            gl = {h: gc[h][last:last + 1, :] for h in heads}
            st = {h: st_ref[h] for h in heads}
            o = {h: _dot(a[h], xi_ref[0, rs, hsl[h]]) + _dot_nt(q[h] * jnp.exp2(gc[h]).astype(BF16), st[h])
                 for h in heads}
            for h in heads:
                kd = k[h] * jnp.exp2(gl[h] - gc[h]).astype(BF16)
                st_ref[h] = st[h] * jnp.exp2(gl[h]) + _dot_tn(xi_ref[0, rs, hsl[h]], kd)
                o_ref[0, rs, hsl[h]] = o[h].astype(BF16)


def _hg_scan(p, low_row, tc, reverse):
    n_b, ta, _ = p.shape
    tb = TOK_BLK
    n_blk = ta // tb
    nctx = tc // tb
    d_i = 1 if reverse else 0
    masks = _hg_pair_masks(reverse)
    bi = functools.partial(_blk_index, n_blk=n_blk, nctx=nctx, reverse=reverse)
    return pl.pallas_call(
        functools.partial(_hg_kernel, reverse=reverse),
        grid=(n_b, n_blk),
        in_specs=[pl.BlockSpec((1, tb, 1024), lambda b, n: (b, bi(n), C_HQ // 1024)),
                  pl.BlockSpec((1, tb, 1024), lambda b, n: (b, bi(n), C_HF // 1024 + d_i)),
                  pl.BlockSpec((1, tb, 1024), lambda b, n: (b, bi(n), C_HI // 1024)),
                  pl.BlockSpec((1, 1024), lambda b, n: (0, 0)),
                  pl.BlockSpec(masks.shape, lambda b, n: (0, 0, 0))],
        out_specs=pl.BlockSpec((1, tb, 1024), lambda b, n: (b, bi(n), 0)),
        out_shape=jax.ShapeDtypeStruct((n_b, ta, 1024), BF16),
        scratch_shapes=[pltpu.VMEM((HG_HEADS, HG_DV, HG_DK), F32),
                        pltpu.VMEM((tb, 1024), BF16),
                        pltpu.VMEM((tb, 1024), F32)],
        compiler_params=_cparams(("arbitrary", "arbitrary")),
        name="hg_bwd" if reverse else "hg_fwd",
    )(p, p, p, low_row, masks)


def _branch(of_ref, ob_ref, gate_ref, nw_ref, dv):
    o = of_ref[0].astype(F32) + ob_ref[0].astype(F32)
    gate = gate_ref[0].astype(F32)
    nw = nw_ref[...]
    parts = []
    for h in range(1024 // dv):
        x = o[:, h * dv:(h + 1) * dv]
        y = x * lax.rsqrt(jnp.mean(x * x, axis=-1, keepdims=True) + NORM_EPS) * nw
        parts.append(y)
    return (jnp.concatenate(parts, axis=-1) * gate).astype(BF16)


def _merge_kernel(dnf, dnb, rtf, rtb, hgf, hgb, g_dn, g_rt, g_hg, m0, m1, m2, x_ref, mod_ref,
                  nw_dn, nw_rt, nw_hg, wb_ref, wo_ref, fnw_ref, o_ref, *, tc, blk_off, n_b, final):
    b = pl.program_id(0)
    i = pl.program_id(1) + blk_off
    d = D_MODEL
    brs = (_branch(dnf, dnb, g_dn, nw_dn, DN_DV),
           _branch(rtf, rtb, g_rt, nw_rt, RET_DV),
           _branch(hgf, hgb, g_hg, nw_hg, HG_DV))
    y = None
    for idx, (br, mg) in enumerate(zip(brs, (m0, m1, m2))):
        t = jnp.dot(br, wb_ref[idx], preferred_element_type=F32) * mg[0].astype(F32)
        y = t if y is None else y + t
    out = jnp.dot(y.astype(BF16), wo_ref[...], preferred_element_type=F32)
    is_ctx = (i * TOK_BLK) < tc
    gt = jnp.where(is_ctx, mod_ref[n_b:n_b + 1, 2 * d:3 * d], mod_ref[pl.ds(b, 1), 2 * d:3 * d])
    xn = x_ref[0] + gt * out
    if final:
        xn = xn * lax.rsqrt(jnp.mean(xn * xn, axis=-1, keepdims=True) + NORM_EPS) * fnw_ref[...]
    o_ref[0] = xn


def _merge(outs, p, xa, mod_l, nws, wb, wo, fnw, tc, final):
    n_b, ta, d = xa.shape
    tb = TOK_BLK
    blk_off = tc // tb if final else 0
    n_blk = ta // tb - blk_off
    rows = lambda b, i: (b, i + blk_off, 0)
    pcol = lambda cb: (lambda b, i: (b, i + blk_off, cb))
    const2 = lambda b, i: (0, 0)
    in_specs = ([pl.BlockSpec((1, tb, 1024), rows)] * 6
                + [pl.BlockSpec((1, tb, 1024), pcol(C_DNGATE // 1024)),
                   pl.BlockSpec((1, tb, 1024), pcol(C_RGATE // 1024)),
                   pl.BlockSpec((1, tb, 1024), pcol(C_HGATE // 1024)),
                   pl.BlockSpec((1, tb, 1024), pcol(C_MERGE0 // 1024)),
                   pl.BlockSpec((1, tb, 1024), pcol(C_MERGE1 // 1024)),
                   pl.BlockSpec((1, tb, 1024), pcol(C_MERGE2 // 1024)),
                   pl.BlockSpec((1, tb, d), rows),
                   pl.BlockSpec((8, 3 * d), const2),
                   pl.BlockSpec((1, DN_DV), const2),
                   pl.BlockSpec((1, RET_DV), const2),
                   pl.BlockSpec((1, HG_DV), const2),
                   pl.BlockSpec((N_BRANCH, 1024, d), lambda b, i: (0, 0, 0)),
                   pl.BlockSpec((d, d), const2),
                   pl.BlockSpec((1, d), const2)])
    out_rows = ta - blk_off * tb
    return pl.pallas_call(
        functools.partial(_merge_kernel, tc=tc, blk_off=blk_off, n_b=n_b, final=final),
        grid=(n_b, n_blk),
        in_specs=in_specs,
        out_specs=pl.BlockSpec((1, tb, d), lambda b, i: (b, i, 0)),
        out_shape=jax.ShapeDtypeStruct((n_b, out_rows, d), F32),
        compiler_params=_cparams(("arbitrary", "arbitrary")),
        name="merge_final" if final else "merge",
    )(*outs, p, p, p, p, p, p, xa, mod_l, *nws, wb, wo, fnw)


def _permute_w_in(w):
    d = w.shape[0]
    o = 0
    dn_qkv = w[:, o:o + 2048]; o += 2048
    dn_gate = w[:, o:o + 1024]; o += 1024
    dn_a = w[:, o:o + 8]; o += 8
    dn_b = w[:, o:o + 8]; o += 8
    r_q = w[:, o:o + 512]; o += 512
    r_k = w[:, o:o + 512]; o += 512
    r_v = w[:, o:o + 1024]; o += 1024
    r_gate = w[:, o:o + 1024]; o += 1024
    hg_q = w[:, o:o + 1024]; o += 1024
    hg_f = w[:, o:o + 2048]; o += 2048
    hg_i = w[:, o:o + 1024]; o += 1024
    hg_gate = w[:, o:o + 1024]; o += 1024
    merge = w[:, o:o + 3072]; o += 3072
    deint = lambda t: t.reshape(d, RET_HEADS, RET_DK // 2, 2).transpose(0, 1, 3, 2).reshape(d, 512)
    wp = jnp.concatenate([dn_qkv, deint(r_q), deint(r_k), r_v, hg_i, merge[:, 2048:], dn_gate, r_gate,
                          hg_gate, hg_q, hg_f, merge[:, :2048]], axis=1).astype(BF16)
    wab = jnp.concatenate([dn_a, dn_b, jnp.zeros((d, LANES - 16), w.dtype)], axis=1).astype(BF16)
    return wp, wab


def _rope_tables(n_lat, tc):
    half = RET_DK // 2
    inv = ROPE_BASE ** (-np.arange(0, half, 2, dtype=np.float64) / half)
    t = np.arange(n_lat)
    row = (t // GRID_W).astype(np.float64)
    col = (t % GRID_W).astype(np.float64)
    ang = np.concatenate([row[:, None] * inv, col[:, None] * inv], axis=-1)
    cos, sin = np.cos(ang), np.sin(ang)
    cos_l = np.concatenate([cos, cos], axis=-1)
    sin_l = np.concatenate([-sin, sin], axis=-1)
    cos_all = np.concatenate([np.ones((tc, LANES)), cos_l], axis=0).astype(np.float32)
    sin_all = np.concatenate([np.zeros((tc, LANES)), sin_l], axis=0).astype(np.float32)
    return jnp.asarray(cos_all), jnp.asarray(sin_all)


def _pad_row(v, width=LANES):
    v = v.reshape(1, -1).astype(F32)
    return jnp.pad(v, ((0, 0), (0, width - v.shape[1])))


def kernel(x, c, ctx, c_ctx, norm_w, ada_w, ada_b, w_in, dn_conv, dn_a_log, dn_dt_bias, dn_norm_w,
           ret_decay, ret_norm_w, hg_lb, hg_norm_w, w_branch, w_out, final_norm_w):
    n_b, n_lat, d = x.shape
    tc = ctx.shape[1]
    depth = norm_w.shape[0]
    assert d == D_MODEL and tc % TOK_BLK == 0 and n_lat % TOK_BLK == 0 and n_b + 1 <= 8
    assert n_lat % GRID_W == 0

    xa = jnp.concatenate([ctx, x], axis=1).astype(F32)
    cin = jnp.concatenate([c, c_ctx[None, :], jnp.zeros((8 - n_b - 1, d), c.dtype)], axis=0).astype(F32)
    mod = _modulation(cin, ada_w.astype(F32), ada_b.astype(F32))
    lower = _hg_lower(hg_lb)
    cos_t, sin_t = _rope_tables(n_lat, tc)
    fnw = final_norm_w.reshape(1, d).astype(F32)

    for l in range(depth):
        final = l == depth - 1
        wp, wab = _permute_w_in(w_in[l])
        p, pab = _project(xa, mod[l], norm_w[l].reshape(1, d).astype(F32), wp, wab, tc)
        dq, dk, dkt, dv, rq, rkt, gcol, grow, dktf, growf = _prep(
            p, pab, dn_conv[l].astype(F32), cos_t, sin_t, _pad_row(dn_a_log[l]), _pad_row(dn_dt_bias[l]), tc)
        lam_row = _pad_row(ret_decay[l])
        outs = []
        for reverse in (False, True):
            outs.append(_dn_scan(dq, dk, dktf, dkt, dv, gcol, growf, grow, tc, reverse))
        for reverse in (False, True):
            outs.append(_ret_scan(rq, rkt, p, lam_row, tc, reverse))
        for reverse in (False, True):
            outs.append(_hg_scan(p, lower[l, 1 if reverse else 0].reshape(1, -1), tc, reverse))
        nws = (dn_norm_w[l].reshape(1, -1).astype(F32), ret_norm_w[l].reshape(1, -1).astype(F32),
               hg_norm_w[l].reshape(1, -1).astype(F32))
        xa = _merge(outs, p, xa, mod[l], nws, w_branch[l].astype(BF16), w_out[l].astype(BF16), fnw, tc, final)
    return xa
```

```python
import functools
import math

import jax
import jax.numpy as jnp
import numpy as np
from jax import lax
from jax.experimental import pallas as pl
from jax.experimental.pallas import tpu as pltpu

F32 = jnp.float32
BF16 = jnp.bfloat16

D_MODEL = 1024
GRID_W = 64
DN_HEADS, DN_DK, DN_DV = 4, 128, 256
RET_HEADS, RET_DK, RET_DV = 4, 128, 256
HG_HEADS, HG_DK, HG_DV = 8, 128, 128
CONV_W = 5
ROPE_BASE = 10000.0
NORM_EPS = 1e-6
N_BRANCH = 3

LANES = 128
TOK_BLK = 256
DN_CHUNK = 64
HG_CHUNK = 64
NEG_BIG = -1e30

C_DNQKV = 0
C_RQ = 2048
C_RK = 2560
C_RV = 3072
C_HI = 4096
C_MERGE2 = 5120
C_DNGATE = 6144
C_RGATE = 7168
C_HGATE = 8192
C_HQ = 9216
C_HF = 10240
C_MERGE0 = 12288
C_MERGE1 = 13312
P_WIDTH = 14336
PROJ_TILE = 2048
PROJ_SUB = 512
PROJ_ACT = ("NNNN", "NNNN", "NNGG", "SSSS", "SSSS", "GGGG", "GGGG")
NEG_LOG2E = -1.0 / math.log(2.0)
VMEM_LIMIT = 56 * 1024 * 1024


def _dot(a, b):
    return jnp.dot(a.astype(BF16), b.astype(BF16), preferred_element_type=F32)


def _dot_nt(a, b):
    return lax.dot_general(a.astype(BF16), b.astype(BF16), (((1,), (1,)), ((), ())),
                           preferred_element_type=F32)


def _dot_tn(a, b):
    return lax.dot_general(a.astype(BF16), b.astype(BF16), (((0,), (0,)), ((), ())),
                           preferred_element_type=F32)


def _sigmoid(x):
    return 0.5 * jnp.tanh(0.5 * x) + 0.5


def _silu(x):
    return x * _sigmoid(x)


def _softplus(x):
    return jnp.maximum(x, 0.0) + jnp.log(1.0 + jnp.exp(-jnp.abs(x)))


def _iota(shape, dim):
    return lax.broadcasted_iota(jnp.int32, shape, dim)


def _shr(x, s):
    return lax.shift_right_logical(x, int(s).bit_length() - 1)


def _cparams(sem):
    return pltpu.CompilerParams(dimension_semantics=sem, vmem_limit_bytes=VMEM_LIMIT)


def _mod_kernel(cin_ref, w_ref, b_ref, o_ref):
    s = _silu(cin_ref[...])
    o_ref[0] = _dot(s, w_ref[0]) + b_ref[0]


def _modulation(cin, ada_w, ada_b):
    depth, d, d3 = ada_w.shape
    tn = 1024
    return pl.pallas_call(
        _mod_kernel,
        grid=(depth, d3 // tn),
        in_specs=[pl.BlockSpec((8, d), lambda l, j: (0, 0)),
                  pl.BlockSpec((1, d, tn), lambda l, j: (l, 0, j)),
                  pl.BlockSpec((1, 1, tn), lambda l, j: (l, 0, j))],
        out_specs=pl.BlockSpec((1, 8, tn), lambda l, j: (l, 0, j)),
        out_shape=jax.ShapeDtypeStruct((depth, 8, d3), F32),
        compiler_params=_cparams(("arbitrary", "arbitrary")),
        name="ada_mod",
    )(cin, ada_w, ada_b.reshape(depth, 1, d3))


def _lower_kernel(lb_ref, o_ref, *, depth):
    xs = [lb_ref[l] for l in range(depth)]
    m = xs[0]
    for l in range(1, depth):
        m = jnp.maximum(m, xs[l])
    es = [jnp.exp(x - m) for x in xs]
    tot = es[0]
    for l in range(1, depth):
        tot = tot + es[l]
    sm = [e / tot for e in es]
    acc = sm[0]
    o_ref[0] = acc - sm[0]
    for l in range(1, depth):
        acc = acc + sm[l]
        o_ref[l] = acc - sm[0]


def _hg_lower(hg_lb):
    depth = hg_lb.shape[0]
    lb = hg_lb.astype(F32).reshape(depth, 2, HG_HEADS * HG_DK)
    return pl.pallas_call(
        functools.partial(_lower_kernel, depth=depth),
        out_shape=jax.ShapeDtypeStruct(lb.shape, F32),
        name="hg_lower",
    )(lb)


def _proj_kernel(x_ref, mod_ref, nw_ref, w_ref, wab_ref, p_ref, pab_ref, hb_ref, *, tc, tm, tn, n_b):
    b = pl.program_id(0)
    i = pl.program_id(1)
    j = pl.program_id(2)
    d = D_MODEL

    @pl.when(j == 0)
    def _():
        x = x_ref[0]
        ms = jnp.mean(x * x, axis=-1, keepdims=True)
        y = x * lax.rsqrt(ms + NORM_EPS) * nw_ref[...]
        row = i * tm + _iota((tm, 1), 0)
        is_ctx = row < tc
        sh = jnp.where(is_ctx, mod_ref[n_b:n_b + 1, 0:d], mod_ref[pl.ds(b, 1), 0:d])
        sc = jnp.where(is_ctx, mod_ref[n_b:n_b + 1, d:2 * d], mod_ref[pl.ds(b, 1), d:2 * d])
        hb = (y * (1.0 + sc) + sh).astype(BF16)
        hb_ref[...] = hb
        pab_ref[0] = jnp.dot(hb, wab_ref[...], preferred_element_type=F32)

    sub = PROJ_SUB
    for pattern in sorted(set(PROJ_ACT)):
        tiles = [t for t, pat in enumerate(PROJ_ACT) if pat == pattern]
        cond = j == tiles[0]
        for t in tiles[1:]:
            cond = jnp.logical_or(cond, j == t)

        @pl.when(cond)
        def _(pattern=pattern):
            for s in range(tn // sub):
                r = jnp.dot(hb_ref[...], w_ref[:, s * sub:(s + 1) * sub], preferred_element_type=F32)
                if pattern[s] == "S":
                    r = _silu(r)
                elif pattern[s] == "G":
                    r = _sigmoid(r)
                p_ref[0, :, s * sub:(s + 1) * sub] = r.astype(BF16)


def _row_tile(ta):
    best = 16
    for t in range(16, 1101, 16):
        if ta % t == 0:
            best = t
    return best


def _project(xa, mod_l, nw, wp, wab, tc):
    n_b, ta, d = xa.shape
    tm = _row_tile(ta)
    tn = PROJ_TILE
    assert len(PROJ_ACT) * tn == P_WIDTH and all(len(pat) * PROJ_SUB == tn for pat in PROJ_ACT)
    return pl.pallas_call(
        functools.partial(_proj_kernel, tc=tc, tm=tm, tn=tn, n_b=n_b),
        grid=(n_b, ta // tm, P_WIDTH // tn),
        in_specs=[pl.BlockSpec((1, tm, d), lambda b, i, j: (b, i, 0)),
                  pl.BlockSpec((8, 3 * d), lambda b, i, j: (0, 0)),
                  pl.BlockSpec((1, d), lambda b, i, j: (0, 0)),
                  pl.BlockSpec((d, tn), lambda b, i, j: (0, j)),
                  pl.BlockSpec((d, LANES), lambda b, i, j: (0, 0))],
        out_specs=[pl.BlockSpec((1, tm, tn), lambda b, i, j: (b, i, j)),
                   pl.BlockSpec((1, tm, LANES), lambda b, i, j: (b, i, 0))],
        out_shape=[jax.ShapeDtypeStruct((n_b, ta, P_WIDTH), BF16),
                   jax.ShapeDtypeStruct((n_b, ta, LANES), F32)],
        scratch_shapes=[pltpu.VMEM((tm, d), BF16)],
        compiler_params=_cparams(("arbitrary", "arbitrary", "arbitrary")),
        name="in_proj",
    )(xa, mod_l, nw, wp, wab)


def _prep_kernel(main_ref, prev_ref, next_ref, rqk_ref, pab_ref, cw_ref, cos_ref, sin_ref, alog_ref, dtb_ref,
                 dq_ref, dk_ref, dkt_ref, dv_ref, rq_ref, rkt_ref, gcol_ref, grow_ref, dktf_ref, growf_ref,
                 xs_ref, *, tc, n_blk):
    i = pl.program_id(1)
    tb = TOK_BLK
    nctx = tc // tb
    has_prev = jnp.logical_and(i != 0, i != nctx)
    has_next = jnp.logical_and(i != nctx - 1, i != n_blk - 1)
    pm = jnp.where(has_prev, 1.0, 0.0)
    nm = jnp.where(has_next, 1.0, 0.0)
    n_ch = tb // DN_CHUNK

    for g in range(16):
        ls = slice(g * LANES, (g + 1) * LANES)
        xs_ref[8:8 + tb, :] = main_ref[0, :, ls].astype(F32)
        xs_ref[0:8, :] = prev_ref[0, 8:16, ls].astype(F32) * pm
        xs_ref[8 + tb:16 + tb, :] = next_ref[0, 0:8, ls].astype(F32) * nm
        acc = cw_ref[0:1, ls] * xs_ref[6:6 + tb, :]
        for t in range(1, CONV_W):
            acc = acc + cw_ref[t:t + 1, ls] * xs_ref[6 + t:6 + t + tb, :]
        y = _silu(acc)
        if g < 8:
            y = y * lax.rsqrt(jnp.sum(y * y, axis=-1, keepdims=True) + NORM_EPS)
        if g < 4:
            dq_ref[0, :, ls] = (y * (DN_DK ** -0.5)).astype(BF16)
        elif g < 8:
            h = g - 4
            hs = slice(h * LANES, (h + 1) * LANES)
            dk_ref[0, :, hs] = y.astype(BF16)
            dktf_ref[0, h] = y.T.astype(BF16)
            for c in range(n_ch):
                dkt_ref[0, h, c] = y[c * DN_CHUNK:(c + 1) * DN_CHUNK, :].T.astype(BF16)
        else:
            vs = slice((g - 8) * LANES, (g - 7) * LANES)
            dv_ref[0, :, vs] = y.astype(BF16)

    pab = pab_ref[0]
    g_log = -jnp.exp(alog_ref[...]) * _softplus(pab + dtb_ref[...])
    beta = _sigmoid(pab)
    r = _iota((tb, tb), 0)
    c_ = _iota((tb, tb), 1)
    same = _shr(r, DN_CHUNK) == _shr(c_, DN_CHUNK)
    m_f = jnp.where(jnp.logical_and(same, r >= c_), 1.0, 0.0)
    m_b = jnp.where(jnp.logical_and(same, r <= c_), 1.0, 0.0)
    cs_f = jnp.dot(m_f, g_log, precision=lax.Precision.HIGHEST, preferred_element_type=F32)
    cs_b = jnp.dot(m_b, g_log, precision=lax.Precision.HIGHEST, preferred_element_type=F32)
    lane = _iota((tb, LANES), 1)
    gcol = jnp.where(lane < 4, cs_f, jnp.where(lane < 8, cs_b, beta))
    gcol_ref[0] = gcol
    growf_ref[0] = gcol.T[0:8, :]
    for c in range(n_ch):
        grow_ref[0, c] = gcol[c * DN_CHUNK:(c + 1) * DN_CHUNK, :].T[0:8, :]

    cosv = cos_ref[...]
    sinv = sin_ref[...]
    for g in range(8):
        ls = slice(g * LANES, (g + 1) * LANES)
        x = rqk_ref[0, :, ls].astype(F32)
        y = x * cosv + pltpu.roll(x, LANES // 2, 1) * sinv
        if g < 4:
            rq_ref[0, :, ls] = y.astype(BF16)
        else:
            rkt_ref[0, g - 4] = (y * (RET_DK ** -0.5)).T.astype(BF16)


def _prep(p, pab, conv_w, cos_t, sin_t, alog_row, dtb_row, tc):
    n_b, ta, _ = p.shape
    tb = TOK_BLK
    n_blk = ta // tb
    n_ch = tb // DN_CHUNK
    hpb = tb // 16
    n16 = ta // 16
    out_shape = [
        jax.ShapeDtypeStruct((n_b, ta, 512), BF16),
        jax.ShapeDtypeStruct((n_b, ta, 512), BF16),
        jax.ShapeDtypeStruct((n_b, DN_HEADS, ta // DN_CHUNK, DN_DK, DN_CHUNK), BF16),
        jax.ShapeDtypeStruct((n_b, ta, 1024), BF16),
        jax.ShapeDtypeStruct((n_b, ta, 512), BF16),
        jax.ShapeDtypeStruct((n_b, RET_HEADS, RET_DK, ta), BF16),
        jax.ShapeDtypeStruct((n_b, ta, LANES), F32),
        jax.ShapeDtypeStruct((n_b, ta // DN_CHUNK, 8, DN_CHUNK), F32),
        jax.ShapeDtypeStruct((n_b, DN_HEADS, DN_DK, ta), BF16),
        jax.ShapeDtypeStruct((n_b, 8, ta), F32),
    ]
    out_specs = [
        pl.BlockSpec((1, tb, 512), lambda b, i: (b, i, 0)),
        pl.BlockSpec((1, tb, 512), lambda b, i: (b, i, 0)),
        pl.BlockSpec((1, DN_HEADS, n_ch, DN_DK, DN_CHUNK), lambda b, i: (b, 0, i, 0, 0)),
        pl.BlockSpec((1, tb, 1024), lambda b, i: (b, i, 0)),
        pl.BlockSpec((1, tb, 512), lambda b, i: (b, i, 0)),
        pl.BlockSpec((1, RET_HEADS, RET_DK, tb), lambda b, i: (b, 0, 0, i)),
        pl.BlockSpec((1, tb, LANES), lambda b, i: (b, i, 0)),
        pl.BlockSpec((1, n_ch, 8, DN_CHUNK), lambda b, i: (b, i, 0, 0)),
        pl.BlockSpec((1, DN_HEADS, DN_DK, tb), lambda b, i: (b, 0, 0, i)),
        pl.BlockSpec((1, 8, tb), lambda b, i: (b, 0, i)),
    ]
    in_specs = [
        pl.BlockSpec((1, tb, 2048), lambda b, i: (b, i, 0)),
        pl.BlockSpec((1, 16, 2048), lambda b, i: (b, jnp.maximum(i * hpb - 1, 0), 0)),
        pl.BlockSpec((1, 16, 2048), lambda b, i: (b, jnp.minimum((i + 1) * hpb, n16 - 1), 0)),
        pl.BlockSpec((1, tb, 1024), lambda b, i: (b, i, C_RQ // 1024)),
        pl.BlockSpec((1, tb, LANES), lambda b, i: (b, i, 0)),
        pl.BlockSpec((CONV_W, 2048), lambda b, i: (0, 0)),
        pl.BlockSpec((tb, LANES), lambda b, i: (i, 0)),
        pl.BlockSpec((tb, LANES), lambda b, i: (i, 0)),
        pl.BlockSpec((1, LANES), lambda b, i: (0, 0)),
        pl.BlockSpec((1, LANES), lambda b, i: (0, 0)),
    ]
    return pl.pallas_call(
        functools.partial(_prep_kernel, tc=tc, n_blk=n_blk),
        grid=(n_b, n_blk),
        in_specs=in_specs,
        out_specs=out_specs,
        out_shape=out_shape,
        scratch_shapes=[pltpu.VMEM((tb + 16, LANES), F32)],
        compiler_params=_cparams(("arbitrary", "arbitrary")),
        name="prep",
    )(p, p, p, p, pab, conv_w, cos_t, sin_t, alog_row, dtb_row)


def _blk_index(n, n_blk, nctx, reverse):
    if not reverse:
        return n
    return jnp.where(n < nctx, nctx - 1 - n, n_blk - 1 - (n - nctx))


DN_MASK_NEG, DN_MASK_STRICT, DN_MASK_B8, DN_MASK_EYE, DN_MASK_OFF0 = 0, 1, 2, 3, 4
DN_MERGE_SIZES = (8, 16, 32)


DN_PAIR = 2 * DN_CHUNK


def _dn_masks(reverse):
    r = np.arange(DN_PAIR)[:, None]
    c = np.arange(DN_PAIR)[None, :]
    same = (r // DN_CHUNK) == (c // DN_CHUNK)
    incl = same & ((r <= c) if reverse else (r >= c))
    strict = incl & (r != c)
    ms = [np.where(incl, 0.0, NEG_BIG), strict, (r // 8) == (c // 8), r == c]
    for s in DN_MERGE_SIZES:
        ms.append(((r // (2 * s)) == (c // (2 * s))) & ((r // s) != (c // s)))
    return jnp.asarray(np.stack([np.asarray(m, np.float32) for m in ms]))


def _dn_streams(n, q_ref, k_ref, ktf_ref, ktc_ref, v_ref, gcol_ref, growf_ref, growc_ref, m_ref, o_ref, s_ref,
                wq_s, u_s, attn_s, kdt_s, egl_s, reverse):
    c_len = DN_CHUNK
    n_ch = TOK_BLK // c_len
    d_off = 4 if reverse else 0
    heads = range(DN_HEADS)
    last = 0 if reverse else c_len - 1
    pairs = [slice(g * DN_PAIR, (g + 1) * DN_PAIR) for g in range(TOK_BLK // DN_PAIR)]
    units = [(h, g) for h in heads for g in range(len(pairs))]

    wr = lax.rem(n, 2)
    rd = 1 - wr

    @pl.when(n == 0)
    def _():
        s_ref[...] = jnp.zeros_like(s_ref)
        wq_s[...] = jnp.zeros_like(wq_s)
        u_s[...] = jnp.zeros_like(u_s)
        attn_s[...] = jnp.zeros_like(attn_s)
        kdt_s[...] = jnp.zeros_like(kdt_s)
        egl_s[...] = jnp.zeros_like(egl_s)

    us = range(len(units))
    pa = {}

    def a_load():
        gcol = gcol_ref[0]
        growf = growf_ref[0]
        pa["gc"] = [gcol[:, d_off + h:d_off + h + 1] for h in heads]
        pa["bc"] = [gcol[:, 8 + d_off + h:9 + d_off + h] for h in heads]
        pa["egc"] = [jnp.exp(x) for x in pa["gc"]]
        pa["kb"] = [k_ref[0, :, h * DN_DK:(h + 1) * DN_DK].astype(F32) * pa["bc"][h] for h in heads]
        dec = [jnp.exp((pa["gc"][h][pairs[g]] - growf[d_off + h:d_off + h + 1, pairs[g]]) + m_ref[DN_MASK_NEG])
               for h, g in units]
        pa["a"] = [_dot(pa["kb"][h][pairs[g]], ktf_ref[0, h, :, pairs[g]]) * dec[u] * m_ref[DN_MASK_STRICT]
                   for u, (h, g) in enumerate(units)]

    def a_sq():
        pa["d"] = [(pa["a"][u] * m_ref[DN_MASK_B8]).astype(BF16) for u in us]
        pa["d2"] = [jnp.dot(pa["d"][u], pa["d"][u], preferred_element_type=F32) for u in us]

    def a_pow():
        d2b = [x.astype(BF16) for x in pa["d2"]]
        pa["d4"] = [jnp.dot(d2b[u], d2b[u], preferred_element_type=F32) for u in us]
        pa["d3"] = [jnp.dot(pa["d"][u], d2b[u], preferred_element_type=F32) for u in us]

    def a_base():
        p1 = [m_ref[DN_MASK_EYE] - pa["d"][u].astype(F32) + pa["d2"][u] - pa["d3"][u] for u in us]
        pa["t"] = [p1[u] + _dot(p1[u], pa["d4"][u]) for u in us]

    def a_merge_x(lvl):
        def run():
            pa["tb"] = [x.astype(BF16) for x in pa["t"]]
            pa["x"] = [_dot(pa["a"][u] * m_ref[DN_MASK_OFF0 + lvl], pa["tb"][u]) for u in us]
        return run

    def a_merge_t():
        pa["t"] = [pa["t"][u] - _dot(pa["tb"][u], pa["x"][u]) for u in us]

    def a_store():
        neg64 = m_ref[DN_MASK_NEG, 0:c_len, 0:c_len]
        for u, (h, g) in enumerate(units):
            rows = pairs[g]
            rhs = jnp.concatenate([pa["kb"][h][rows] * pa["egc"][h][rows],
                                   v_ref[0, rows, h * DN_DV:(h + 1) * DN_DV].astype(F32) * pa["bc"][h][rows]], axis=1)
            wu = _dot(pa["t"][u], rhs)
            qd = q_ref[0, rows, h * DN_DK:(h + 1) * DN_DK].astype(F32) * pa["egc"][h][rows]
            u_s[wr, h, rows, :] = wu[:, DN_DK:DN_DK + DN_DV]
            for ci in range(DN_PAIR // c_len):
                c = g * (DN_PAIR // c_len) + ci
                cs = slice(ci * c_len, (ci + 1) * c_len)
                wq_s[wr, h, c] = jnp.concatenate([wu[cs, 0:DN_DK], qd[cs]], axis=0).astype(BF16)
        for c in range(n_ch):
            rs = slice(c * c_len, (c + 1) * c_len)
            growc = growc_ref[0, c]
            for h in heads:
                gr = growc[d_off + h:d_off + h + 1, :]
                gl = gr[:, last:last + 1]
                kt = ktc_ref[0, h, c]
                attn = _dot(q_ref[0, rs, h * DN_DK:(h + 1) * DN_DK], kt) * jnp.exp((pa["gc"][h][rs] - gr) + neg64)
                attn_s[wr, h, c] = attn.astype(BF16)
                kdt_s[wr, h, c] = (kt.astype(F32) * jnp.exp(gl - gr)).astype(BF16)
                egl_s[wr, h, c] = jnp.broadcast_to(jnp.exp(gl), (8, LANES))

    pb = {"state": [s_ref[h] for h in heads]}

    def b_first(c):
        def run():
            pb["ws"] = [jnp.dot(wq_s[rd, h, c], pb["state"][h].astype(BF16), preferred_element_type=F32)
                        for h in heads]
        return run

    def b_second(c):
        def run():
            rs = slice(c * c_len, (c + 1) * c_len)
            vn = [(u_s[rd, h, rs, :] - pb["ws"][h][0:c_len]).astype(BF16) for h in heads]
            o = [pb["ws"][h][c_len:2 * c_len] + jnp.dot(attn_s[rd, h, c], vn[h], preferred_element_type=F32)
                 for h in heads]
            pb["state"] = [pb["state"][h] * egl_s[rd, h, c][0:1, 0:1]
                           + jnp.dot(kdt_s[rd, h, c], vn[h], preferred_element_type=F32) for h in heads]
            for h in heads:
                o_ref[0, rs, h * DN_DV:(h + 1) * DN_DV] = o[h].astype(BF16)
        return run

    a_stages = [a_load, a_sq, a_pow, a_base]
    for lvl in range(len(DN_MERGE_SIZES)):
        a_stages += [a_merge_x(lvl), a_merge_t]
    a_stages.append(a_store)
    b_stages = []
    for cc in range(n_ch):
        c = (n_ch - 1 - cc) if reverse else cc
        b_stages += [b_first(c), b_second(c)]

    def finish():
        for h in heads:
            s_ref[h] = pb["state"][h]

    return a_stages, b_stages, finish


DN_N_IN = 9
DN_N_SCRATCH = 6


def _dn_kernel(*refs):
    n = pl.program_id(1)
    ins_f, ins_b = refs[0:DN_N_IN], refs[DN_N_IN:2 * DN_N_IN]
    o_f, o_b = refs[2 * DN_N_IN], refs[2 * DN_N_IN + 1]
    scr = refs[2 * DN_N_IN + 2:]
    streams = [_dn_streams(n, *ins_f, o_f, *scr[0:DN_N_SCRATCH], reverse=False),
               _dn_streams(n, *ins_b, o_b, *scr[DN_N_SCRATCH:], reverse=True)]
    n_stage = max(max(len(a), len(b)) for a, b, _ in streams)
    for i in range(n_stage):
        for a_stages, _, _ in streams:
            if i < len(a_stages):
                a_stages[i]()
        for _, b_stages, _ in streams:
            if i < len(b_stages):
                b_stages[i]()
    for _, _, finish in streams:
        finish()


def _dn_scan(dq, dk, dktf, dktc, dv, gcol, growf, growc, tc):
    n_b, ta, _ = dq.shape
    tb = TOK_BLK
    n_blk = ta // tb
    nctx = tc // tb
    n_ch = tb // DN_CHUNK
    in_specs, args, out_specs, scratch = [], [], [], []
    for reverse in (False, True):
        masks = _dn_masks(reverse)
        blk = functools.partial(_blk_index, n_blk=n_blk, nctx=nctx, reverse=reverse)
        bi = lambda n, blk=blk: blk(jnp.minimum(n, n_blk - 1))
        bo = lambda n, blk=blk: blk(jnp.maximum(n - 1, 0))
        in_specs += [pl.BlockSpec((1, tb, 512), lambda b, n, bi=bi: (b, bi(n), 0)),
                     pl.BlockSpec((1, tb, 512), lambda b, n, bi=bi: (b, bi(n), 0)),
                     pl.BlockSpec((1, DN_HEADS, DN_DK, tb), lambda b, n, bi=bi: (b, 0, 0, bi(n))),
                     pl.BlockSpec((1, DN_HEADS, n_ch, DN_DK, DN_CHUNK), lambda b, n, bi=bi: (b, 0, bi(n), 0, 0)),
                     pl.BlockSpec((1, tb, 1024), lambda b, n, bi=bi: (b, bi(n), 0)),
                     pl.BlockSpec((1, tb, LANES), lambda b, n, bi=bi: (b, bi(n), 0)),
                     pl.BlockSpec((1, 8, tb), lambda b, n, bi=bi: (b, 0, bi(n))),
                     pl.BlockSpec((1, n_ch, 8, DN_CHUNK), lambda b, n, bi=bi: (b, bi(n), 0, 0)),
                     pl.BlockSpec(masks.shape, lambda b, n: (0, 0, 0))]
        args += [dq, dk, dktf, dktc, dv, gcol, growf, growc, masks]
        out_specs.append(pl.BlockSpec((1, tb, 1024), lambda b, n, bo=bo: (b, bo(n), 0)))
        scratch += [pltpu.VMEM((DN_HEADS, DN_DK, DN_DV), F32),
                    pltpu.VMEM((2, DN_HEADS, n_ch, 2 * DN_CHUNK, DN_DK), BF16),
                    pltpu.VMEM((2, DN_HEADS, tb, DN_DV), F32),
                    pltpu.VMEM((2, DN_HEADS, n_ch, DN_CHUNK, DN_CHUNK), BF16),
                    pltpu.VMEM((2, DN_HEADS, n_ch, DN_DK, DN_CHUNK), BF16),
                    pltpu.VMEM((2, DN_HEADS, n_ch, 8, LANES), F32)]
    assert len(args) == 2 * DN_N_IN and len(scratch) == 2 * DN_N_SCRATCH
    return pl.pallas_call(
        _dn_kernel,
        grid=(n_b, n_blk + 1),
        in_specs=in_specs,
        out_specs=out_specs,
        out_shape=[jax.ShapeDtypeStruct((n_b, ta, 1024), BF16)] * 2,
        scratch_shapes=scratch,
        compiler_params=_cparams(("arbitrary", "arbitrary")),
        name="dn_scan",
    )(*args)


def _ret_kernel(qf_ref, ktf_ref, vf_ref, qb_ref, ktb_ref, vb_ref, lam_ref, of_ref, ob_ref,
                s_ref, dec_ref, eq_ref, ek_ref, egl_ref):
    n = pl.program_id(1)
    c_len = TOK_BLK
    dirs = ((0, qf_ref, ktf_ref, vf_ref, of_ref), (1, qb_ref, ktb_ref, vb_ref, ob_ref))
    streams = [(d, h) for d in range(2) for h in range(RET_HEADS)]

    @pl.when(n == 0)
    def _():
        s_ref[...] = jnp.zeros_like(s_ref)
        ri = _iota((c_len, c_len), 0)
        ci = _iota((c_len, c_len), 1)
        rowpos = _iota((c_len, LANES), 0)
        colpos = _iota((RET_DK, c_len), 1)
        for d, h in streams:
            reverse = d == 1
            dist = (ci - ri) if reverse else (ri - ci)
            if reverse:
                qexp = (c_len - rowpos).astype(F32)
                kexp = colpos.astype(F32)
            else:
                qexp = (rowpos + 1).astype(F32)
                kexp = (c_len - 1 - colpos).astype(F32)
            x = lam_ref[0:1, 4 * d + h:4 * d + h + 1]
            lam = jnp.minimum(x, 0.0) - jnp.log(1.0 + jnp.exp(-jnp.abs(x)))
            dec_ref[d, h] = jnp.exp(jnp.where(dist >= 0, dist.astype(F32) * lam, NEG_BIG))
            eq_ref[d, h] = jnp.exp(qexp * lam)
            ek_ref[d, h] = jnp.exp(kexp * lam)
            egl_ref[d, h] = jnp.broadcast_to(jnp.exp(lam * float(c_len)), (8, LANES))

    def q_of(d, h):
        return dirs[d][1][0, :, h * RET_DK:(h + 1) * RET_DK]

    def v_of(d, h):
        return dirs[d][3][0, :, h * RET_DV:(h + 1) * RET_DV]

    s_old = {u: s_ref[u[0], u[1]] for u in streams}
    qk = {u: _dot(q_of(*u), dirs[u[0]][2][0, u[1]]) for u in streams}
    kv = {u: _dot(dirs[u[0]][2][0, u[1]].astype(F32) * ek_ref[u[0], u[1]], v_of(*u)) for u in streams}
    qs = {u: _dot(q_of(*u).astype(F32) * eq_ref[u[0], u[1]], s_old[u]) for u in streams}
    av = {u: _dot(qk[u] * dec_ref[u[0], u[1]], v_of(*u)) for u in streams}
    for u in streams:
        d, h = u
        s_ref[d, h] = s_old[u] * egl_ref[d, h][0:1, 0:1] + kv[u]
        dirs[d][4][0, :, h * RET_DV:(h + 1) * RET_DV] = (av[u] + qs[u]).astype(BF16)


def _ret_scan(rq, rkt, p, lam_row, tc):
    n_b, ta, _ = rq.shape
    tb = TOK_BLK
    n_blk = ta // tb
    nctx = tc // tb
    in_specs, out_specs = [], []
    for reverse in (False, True):
        bi = functools.partial(_blk_index, n_blk=n_blk, nctx=nctx, reverse=reverse)
        in_specs += [pl.BlockSpec((1, tb, 512), lambda b, n, bi=bi: (b, bi(n), 0)),
                     pl.BlockSpec((1, RET_HEADS, RET_DK, tb), lambda b, n, bi=bi: (b, 0, 0, bi(n))),
                     pl.BlockSpec((1, tb, 1024), lambda b, n, bi=bi: (b, bi(n), C_RV // 1024))]
        out_specs.append(pl.BlockSpec((1, tb, 1024), lambda b, n, bi=bi: (b, bi(n), 0)))
    in_specs.append(pl.BlockSpec((1, LANES), lambda b, n: (0, 0)))
    return pl.pallas_call(
        _ret_kernel,
        grid=(n_b, n_blk),
        in_specs=in_specs,
        out_specs=out_specs,
        out_shape=[jax.ShapeDtypeStruct((n_b, ta, 1024), BF16)] * 2,
        scratch_shapes=[pltpu.VMEM((2, RET_HEADS, RET_DK, RET_DV), F32),
                        pltpu.VMEM((2, RET_HEADS, tb, tb), F32),
                        pltpu.VMEM((2, RET_HEADS, tb, LANES), F32),
                        pltpu.VMEM((2, RET_HEADS, RET_DK, tb), F32),
                        pltpu.VMEM((2, RET_HEADS, 8, LANES), F32)],
        compiler_params=_cparams(("arbitrary", "arbitrary")),
        name="ret_scan",
    )(rq, rkt, p, rq, rkt, p, lam_row)


HG_LEVELS = (1, 2, 4, 8, 16, 32)


def _neg_abs(x):
    bits = lax.bitcast_convert_type(x, jnp.uint32) | jnp.uint32(0x80000000)
    return lax.bitcast_convert_type(bits, F32)


def _hg_pair_masks(reverse):
    i = np.arange(HG_CHUNK)[:, None]
    j = np.arange(HG_CHUNK)[None, :]
    ms = []
    for s in HG_LEVELS:
        q_half = 0 if reverse else 1
        ms.append(((i // (2 * s)) == (j // (2 * s))) & (((i // s) & 1) == q_half) & (((j // s) & 1) == 1 - q_half))
    ms.append(i == j)
    return jnp.asarray(np.stack(ms).astype(np.float32))


def _hg_boundary(gc_s, gc, base, hs, s, row8, reverse):
    c_len = HG_CHUNK
    off = s - 1 if reverse else s
    if s == 1:
        rowi = _iota((c_len, HG_DK), 0)
        if reverse:
            return jnp.where((rowi & 1) == 1, pltpu.roll(gc, 1, 0), gc)
        return jnp.where((rowi & 1) == 0, pltpu.roll(gc, c_len - 1, 0), gc)
    parts = []
    for vi in range(c_len // 8):
        r0 = base + 8 * vi
        if s == 2:
            lo = jnp.broadcast_to(gc_s[r0 + off:r0 + off + 1, hs], (8, HG_DK))
            hi = jnp.broadcast_to(gc_s[r0 + 4 + off:r0 + 5 + off, hs], (8, HG_DK))
            parts.append(jnp.where(row8 < 4, lo, hi))
        else:
            m = base + ((8 * vi) // (2 * s)) * (2 * s) + off
            parts.append(jnp.broadcast_to(gc_s[m:m + 1, hs], (8, HG_DK)))
    return jnp.concatenate(parts, axis=0)


def _hg_kernel(xq_ref, xf_ref, xi_ref, low_ref, pm_ref, o_ref, st_ref, k_s, gc_s, *, reverse):
    n = pl.program_id(1)
    tb = TOK_BLK
    c_len = HG_CHUNK
    n_ch = tb // c_len

    @pl.when(n == 0)
    def _():
        st_ref[...] = jnp.zeros_like(st_ref)

    low = low_ref[...]
    f = low + (1.0 - low) * xf_ref[0].astype(F32)
    g = jnp.log2(f)
    r = _iota((tb, tb), 0)
    c_ = _iota((tb, tb), 1)
    same = _shr(r, c_len) == _shr(c_, c_len)
    tri = (r <= c_) if reverse else (r >= c_)
    m = jnp.where(jnp.logical_and(same, tri), 1.0, 0.0).astype(BF16)
    g0 = g.astype(BF16)
    r1 = g - g0.astype(F32)
    g1 = r1.astype(BF16)
    g2 = (r1 - g1.astype(F32)).astype(BF16)
    gc_s[...] = (jnp.dot(m, g0, preferred_element_type=F32) + jnp.dot(m, g1, preferred_element_type=F32)
                 + jnp.dot(m, g2, preferred_element_type=F32))
    k_s[...] = (1.0 - f).astype(BF16)

    row8 = _iota((8, HG_DK), 0)
    last = 0 if reverse else c_len - 1
    hsl = [slice(h * HG_DK, (h + 1) * HG_DK) for h in range(HG_HEADS)]
    heads = range(HG_HEADS)

    def scores(c):
        base = c * c_len
        rs = slice(base, base + c_len)
        q = {h: xq_ref[0, rs, hsl[h]].astype(F32) for h in heads}
        k = {h: k_s[rs, hsl[h]].astype(F32) for h in heads}
        gc = {h: gc_s[rs, hsl[h]] for h in heads}
        a = {h: _dot_nt(q[h], k[h]) * pm_ref[len(HG_LEVELS)] for h in heads}
        for lvl, s in enumerate(HG_LEVELS):
            e = {h: jnp.exp2(_neg_abs(gc[h] - _hg_boundary(gc_s, gc[h], base, hsl[h], s, row8, reverse)))
                 for h in heads}
            pr = {h: _dot_nt(q[h] * e[h], k[h] * e[h]) for h in heads}
            a = {h: a[h] + pr[h] * pm_ref[lvl] for h in heads}
        return a

    def outputs(c, a):
        base = c * c_len
        rs = slice(base, base + c_len)
        q = {h: xq_ref[0, rs, hsl[h]].astype(F32) for h in heads}
        k = {h: k_s[rs, hsl[h]].astype(F32) for h in heads}
        gc = {h: gc_s[rs, hsl[h]] for h in heads}
        gl = {h: gc[h][last:last + 1, :] for h in heads}
        st = {h: st_ref[h] for h in heads}
        o = {h: _dot(a[h], xi_ref[0, rs, hsl[h]]) + _dot_nt(q[h] * jnp.exp2(gc[h]), st[h]) for h in heads}
        for h in heads:
            kd = k[h] * jnp.exp2(gl[h] - gc[h])
            st_ref[h] = st[h] * jnp.exp2(gl[h]) + _dot_tn(xi_ref[0, rs, hsl[h]], kd)
            o_ref[0, rs, hsl[h]] = o[h].astype(BF16)

    order = [(n_ch - 1 - cc) if reverse else cc for cc in range(n_ch)]
    pending = None
    for c in order:
        a = scores(c)
        if pending is not None:
            outputs(*pending)
        pending = (c, a)
    outputs(*pending)


def _hg_scan(p, low_row, tc, reverse):
    n_b, ta, _ = p.shape
    tb = TOK_BLK
    n_blk = ta // tb
    nctx = tc // tb
    d_i = 1 if reverse else 0
    masks = _hg_pair_masks(reverse)
    bi = functools.partial(_blk_index, n_blk=n_blk, nctx=nctx, reverse=reverse)
    return pl.pallas_call(
        functools.partial(_hg_kernel, reverse=reverse),
        grid=(n_b, n_blk),
        in_specs=[pl.BlockSpec((1, tb, 1024), lambda b, n: (b, bi(n), C_HQ // 1024)),
                  pl.BlockSpec((1, tb, 1024), lambda b, n: (b, bi(n), C_HF // 1024 + d_i)),
                  pl.BlockSpec((1, tb, 1024), lambda b, n: (b, bi(n), C_HI // 1024)),
                  pl.BlockSpec((1, 1024), lambda b, n: (0, 0)),
                  pl.BlockSpec(masks.shape, lambda b, n: (0, 0, 0))],
        out_specs=pl.BlockSpec((1, tb, 1024), lambda b, n: (b, bi(n), 0)),
        out_shape=jax.ShapeDtypeStruct((n_b, ta, 1024), BF16),
        scratch_shapes=[pltpu.VMEM((HG_HEADS, HG_DV, HG_DK), F32),
                        pltpu.VMEM((tb, 1024), BF16),
                        pltpu.VMEM((tb, 1024), F32)],
        compiler_params=_cparams(("arbitrary", "arbitrary")),
        name="hg_bwd" if reverse else "hg_fwd",
    )(p, p, p, low_row, masks)


def _branch(of_ref, ob_ref, gate_ref, nw_ref, dv):
    o = of_ref[0].astype(F32) + ob_ref[0].astype(F32)
    gate = gate_ref[0].astype(F32)
    nw = nw_ref[...]
    parts = []
    for h in range(1024 // dv):
        x = o[:, h * dv:(h + 1) * dv]
        y = x * lax.rsqrt(jnp.mean(x * x, axis=-1, keepdims=True) + NORM_EPS) * nw
        parts.append(y)
    return (jnp.concatenate(parts, axis=-1) * gate).astype(BF16)


def _merge_kernel(dnf, dnb, rtf, rtb, hgf, hgb, g_dn, g_rt, g_hg, m0, m1, m2, x_ref, mod_ref,
                  nw_dn, nw_rt, nw_hg, wb_ref, wo_ref, fnw_ref, o_ref, *, tc, blk_off, n_b, final):
    b = pl.program_id(0)
    i = pl.program_id(1) + blk_off
    d = D_MODEL
    brs = (_branch(dnf, dnb, g_dn, nw_dn, DN_DV),
           _branch(rtf, rtb, g_rt, nw_rt, RET_DV),
           _branch(hgf, hgb, g_hg, nw_hg, HG_DV))
    y = None
    for idx, (br, mg) in enumerate(zip(brs, (m0, m1, m2))):
        t = jnp.dot(br, wb_ref[idx], preferred_element_type=F32) * mg[0].astype(F32)
        y = t if y is None else y + t
    out = jnp.dot(y.astype(BF16), wo_ref[...], preferred_element_type=F32)
    is_ctx = (i * TOK_BLK) < tc
    gt = jnp.where(is_ctx, mod_ref[n_b:n_b + 1, 2 * d:3 * d], mod_ref[pl.ds(b, 1), 2 * d:3 * d])
    xn = x_ref[0] + gt * out
    if final:
        xn = xn * lax.rsqrt(jnp.mean(xn * xn, axis=-1, keepdims=True) + NORM_EPS) * fnw_ref[...]
    o_ref[0] = xn


def _merge(outs, p, xa, mod_l, nws, wb, wo, fnw, tc, final):
    n_b, ta, d = xa.shape
    tb = TOK_BLK
    blk_off = tc // tb if final else 0
    n_blk = ta // tb - blk_off
    rows = lambda b, i: (b, i + blk_off, 0)
    pcol = lambda cb: (lambda b, i: (b, i + blk_off, cb))
    const2 = lambda b, i: (0, 0)
    in_specs = ([pl.BlockSpec((1, tb, 1024), rows)] * 6
                + [pl.BlockSpec((1, tb, 1024), pcol(C_DNGATE // 1024)),
                   pl.BlockSpec((1, tb, 1024), pcol(C_RGATE // 1024)),
                   pl.BlockSpec((1, tb, 1024), pcol(C_HGATE // 1024)),
                   pl.BlockSpec((1, tb, 1024), pcol(C_MERGE0 // 1024)),
                   pl.BlockSpec((1, tb, 1024), pcol(C_MERGE1 // 1024)),
                   pl.BlockSpec((1, tb, 1024), pcol(C_MERGE2 // 1024)),
                   pl.BlockSpec((1, tb, d), rows),
                   pl.BlockSpec((8, 3 * d), const2),
                   pl.BlockSpec((1, DN_DV), const2),
                   pl.BlockSpec((1, RET_DV), const2),
                   pl.BlockSpec((1, HG_DV), const2),
                   pl.BlockSpec((N_BRANCH, 1024, d), lambda b, i: (0, 0, 0)),
                   pl.BlockSpec((d, d), const2),
                   pl.BlockSpec((1, d), const2)])
    out_rows = ta - blk_off * tb
    return pl.pallas_call(
        functools.partial(_merge_kernel, tc=tc, blk_off=blk_off, n_b=n_b, final=final),
        grid=(n_b, n_blk),
        in_specs=in_specs,
        out_specs=pl.BlockSpec((1, tb, d), lambda b, i: (b, i, 0)),
        out_shape=jax.ShapeDtypeStruct((n_b, out_rows, d), F32),
        compiler_params=_cparams(("arbitrary", "arbitrary")),
        name="merge_final" if final else "merge",
    )(*outs, p, p, p, p, p, p, xa, mod_l, *nws, wb, wo, fnw)


def _permute_w_in(w):
    d = w.shape[0]
    o = 0
    dn_qkv = w[:, o:o + 2048]; o += 2048
    dn_gate = w[:, o:o + 1024]; o += 1024
    dn_a = w[:, o:o + 8]; o += 8
    dn_b = w[:, o:o + 8]; o += 8
    r_q = w[:, o:o + 512]; o += 512
    r_k = w[:, o:o + 512]; o += 512
    r_v = w[:, o:o + 1024]; o += 1024
    r_gate = w[:, o:o + 1024]; o += 1024
    hg_q = w[:, o:o + 1024]; o += 1024
    hg_f = w[:, o:o + 2048]; o += 2048
    hg_i = w[:, o:o + 1024]; o += 1024
    hg_gate = w[:, o:o + 1024]; o += 1024
    merge = w[:, o:o + 3072]; o += 3072
    deint = lambda t: t.reshape(d, RET_HEADS, RET_DK // 2, 2).transpose(0, 1, 3, 2).reshape(d, 512)
    wp = jnp.concatenate([dn_qkv, deint(r_q), deint(r_k), r_v, hg_i, merge[:, 2048:], dn_gate, r_gate,
                          hg_gate, hg_q, hg_f, merge[:, :2048]], axis=1).astype(BF16)
    wab = jnp.concatenate([dn_a, dn_b, jnp.zeros((d, LANES - 16), w.dtype)], axis=1).astype(BF16)
    return wp, wab


def _rope_tables(n_lat, tc):
    half = RET_DK // 2
    inv = ROPE_BASE ** (-np.arange(0, half, 2, dtype=np.float64) / half)
    t = np.arange(n_lat)
    row = (t // GRID_W).astype(np.float64)
    col = (t % GRID_W).astype(np.float64)
    ang = np.concatenate([row[:, None] * inv, col[:, None] * inv], axis=-1)
    cos, sin = np.cos(ang), np.sin(ang)
    cos_l = np.concatenate([cos, cos], axis=-1)
    sin_l = np.concatenate([-sin, sin], axis=-1)
    cos_all = np.concatenate([np.ones((tc, LANES)), cos_l], axis=0).astype(np.float32)
    sin_all = np.concatenate([np.zeros((tc, LANES)), sin_l], axis=0).astype(np.float32)
    return jnp.asarray(cos_all), jnp.asarray(sin_all)


def _pad_row(v, width=LANES):
    v = v.reshape(1, -1).astype(F32)
    return jnp.pad(v, ((0, 0), (0, width - v.shape[1])))


def kernel(x, c, ctx, c_ctx, norm_w, ada_w, ada_b, w_in, dn_conv, dn_a_log, dn_dt_bias, dn_norm_w,
           ret_decay, ret_norm_w, hg_lb, hg_norm_w, w_branch, w_out, final_norm_w):
    n_b, n_lat, d = x.shape
    tc = ctx.shape[1]
    depth = norm_w.shape[0]
    assert d == D_MODEL and tc % TOK_BLK == 0 and n_lat % TOK_BLK == 0 and n_b + 1 <= 8
    assert n_lat % GRID_W == 0

    xa = jnp.concatenate([ctx, x], axis=1).astype(F32)
    cin = jnp.concatenate([c, c_ctx[None, :], jnp.zeros((8 - n_b - 1, d), c.dtype)], axis=0).astype(F32)
    mod = _modulation(cin, ada_w.astype(F32), ada_b.astype(F32))
    lower = _hg_lower(hg_lb)
    cos_t, sin_t = _rope_tables(n_lat, tc)
    fnw = final_norm_w.reshape(1, d).astype(F32)

    for l in range(depth):
        final = l == depth - 1
        wp, wab = _permute_w_in(w_in[l])
        p, pab = _project(xa, mod[l], norm_w[l].reshape(1, d).astype(F32), wp, wab, tc)
        dq, dk, dkt, dv, rq, rkt, gcol, grow, dktf, growf = _prep(
            p, pab, dn_conv[l].astype(F32), cos_t, sin_t, _pad_row(dn_a_log[l]), _pad_row(dn_dt_bias[l]), tc)
        lam_row = _pad_row(ret_decay[l])
        outs = []
        outs += _dn_scan(dq, dk, dktf, dkt, dv, gcol, growf, grow, tc)
        outs += _ret_scan(rq, rkt, p, lam_row, tc)
        for reverse in (False, True):
            outs.append(_hg_scan(p, lower[l, 1 if reverse else 0].reshape(1, -1), tc, reverse))
        nws = (dn_norm_w[l].reshape(1, -1).astype(F32), ret_norm_w[l].reshape(1, -1).astype(F32),
               hg_norm_w[l].reshape(1, -1).astype(F32))
        xa = _merge(outs, p, xa, mod[l], nws, w_branch[l].astype(BF16), w_out[l].astype(BF16), fnw, tc, final)
    return xa
```

```python
import functools
import math

import jax
import jax.numpy as jnp
import numpy as np
from jax import lax
from jax.experimental import pallas as pl
from jax.experimental.pallas import tpu as pltpu

F32 = jnp.float32
BF16 = jnp.bfloat16

D_MODEL = 1024
GRID_W = 64
DN_HEADS, DN_DK, DN_DV = 4, 128, 256
RET_HEADS, RET_DK, RET_DV = 4, 128, 256
HG_HEADS, HG_DK, HG_DV = 8, 128, 128
CONV_W = 5
ROPE_BASE = 10000.0
NORM_EPS = 1e-6
N_BRANCH = 3

LANES = 128
TOK_BLK = 256
DN_CHUNK = 64
HG_CHUNK = 64
NEG_BIG = -1e30

C_DNQKV = 0
C_DNGATE = 2048
C_RQ = 3072
C_RK = 3584
C_RV = 4096
C_RGATE = 5120
C_HQ = 6144
C_HF = 7168
C_HI = 9216
C_HGATE = 10240
C_MERGE0 = 11264
C_MERGE1 = 12288
C_MERGE2 = 13312
P_WIDTH = 14336
PROJ_TILE = 2048
PROJ_SUB = 512
PROJ_ACT = ("NNNN", "SSNN", "NNSS", "SSGG", "GGNN", "SSGG", "GGGG")
NEG_LOG2E = -1.0 / math.log(2.0)
VMEM_LIMIT = 56 * 1024 * 1024


def _dot(a, b):
    return jnp.dot(a.astype(BF16), b.astype(BF16), preferred_element_type=F32)


def _dot_nt(a, b):
    return lax.dot_general(a.astype(BF16), b.astype(BF16), (((1,), (1,)), ((), ())),
                           preferred_element_type=F32)


def _dot_tn(a, b):
    return lax.dot_general(a.astype(BF16), b.astype(BF16), (((0,), (0,)), ((), ())),
                           preferred_element_type=F32)


def _sigmoid(x):
    return 0.5 * jnp.tanh(0.5 * x) + 0.5


def _silu(x):
    h = 0.5 * x
    return h + h * jnp.tanh(h)


def _softplus(x):
    return jnp.maximum(x, 0.0) + jnp.log(1.0 + jnp.exp(-jnp.abs(x)))


def _iota(shape, dim):
    return lax.broadcasted_iota(jnp.int32, shape, dim)


def _shr(x, s):
    return lax.shift_right_logical(x, int(s).bit_length() - 1)


def _cparams(sem):
    return pltpu.CompilerParams(dimension_semantics=sem, vmem_limit_bytes=VMEM_LIMIT)


def _mod_kernel(cin_ref, w_ref, b_ref, o_ref):
    s = _silu(cin_ref[...])
    o_ref[0] = _dot(s, w_ref[0]) + b_ref[0]


def _modulation(cin, ada_w, ada_b):
    depth, d, d3 = ada_w.shape
    tn = 1024
    return pl.pallas_call(
        _mod_kernel,
        grid=(depth, d3 // tn),
        in_specs=[pl.BlockSpec((8, d), lambda l, j: (0, 0)),
                  pl.BlockSpec((1, d, tn), lambda l, j: (l, 0, j)),
                  pl.BlockSpec((1, 1, tn), lambda l, j: (l, 0, j))],
        out_specs=pl.BlockSpec((1, 8, tn), lambda l, j: (l, 0, j)),
        out_shape=jax.ShapeDtypeStruct((depth, 8, d3), F32),
        compiler_params=_cparams(("arbitrary", "arbitrary")),
        name="ada_mod",
    )(cin, ada_w, ada_b.reshape(depth, 1, d3))


def _lower_kernel(lb_ref, o_ref, *, depth):
    xs = [lb_ref[l] for l in range(depth)]
    m = xs[0]
    for l in range(1, depth):
        m = jnp.maximum(m, xs[l])
    es = [jnp.exp(x - m) for x in xs]
    tot = es[0]
    for l in range(1, depth):
        tot = tot + es[l]
    sm = [e / tot for e in es]
    acc = sm[0]
    o_ref[0] = acc - sm[0]
    for l in range(1, depth):
        acc = acc + sm[l]
        o_ref[l] = acc - sm[0]


def _hg_lower(hg_lb):
    depth = hg_lb.shape[0]
    lb = hg_lb.astype(F32).reshape(depth, 2, HG_HEADS * HG_DK)
    return pl.pallas_call(
        functools.partial(_lower_kernel, depth=depth),
        out_shape=jax.ShapeDtypeStruct(lb.shape, F32),
        name="hg_lower",
    )(lb)


def _proj_kernel(x_ref, mod_ref, nw_ref, w_ref, wab_ref, p_ref, pab_ref, hb_ref, *, tc, tm, tn, n_b):
    b = pl.program_id(0)
    i = pl.program_id(1)
    j = pl.program_id(2)
    d = D_MODEL

    @pl.when(j == 0)
    def _():
        x = x_ref[0]
        ms = jnp.mean(x * x, axis=-1, keepdims=True)
        y = x * lax.rsqrt(ms + NORM_EPS) * nw_ref[...]
        row = i * tm + _iota((tm, 1), 0)
        is_ctx = row < tc
        sh = jnp.where(is_ctx, mod_ref[n_b:n_b + 1, 0:d], mod_ref[pl.ds(b, 1), 0:d])
        sc = jnp.where(is_ctx, mod_ref[n_b:n_b + 1, d:2 * d], mod_ref[pl.ds(b, 1), d:2 * d])
        hb = (y * (1.0 + sc) + sh).astype(BF16)
        hb_ref[...] = hb
        pab_ref[0] = jnp.dot(hb, wab_ref[...], preferred_element_type=F32)

    sub = PROJ_SUB
    for pattern in sorted(set(PROJ_ACT)):
        tiles = [t for t, pat in enumerate(PROJ_ACT) if pat == pattern]
        cond = j == tiles[0]
        for t in tiles[1:]:
            cond = jnp.logical_or(cond, j == t)

        @pl.when(cond)
        def _(pattern=pattern):
            for s in range(tn // sub):
                r = jnp.dot(hb_ref[...], w_ref[:, s * sub:(s + 1) * sub], preferred_element_type=F32)
                if pattern[s] == "S":
                    r = _silu(r)
                elif pattern[s] == "G":
                    r = _sigmoid(r)
                p_ref[0, :, s * sub:(s + 1) * sub] = r.astype(BF16)


def _row_tile(ta):
    best = 16
    for t in range(16, 1101, 16):
        if ta % t == 0:
            best = t
    return best


def _project(xa, mod_l, nw, wp, wab, tc):
    n_b, ta, d = xa.shape
    tm = _row_tile(ta)
    tn = PROJ_TILE
    assert len(PROJ_ACT) * tn == P_WIDTH and all(len(pat) * PROJ_SUB == tn for pat in PROJ_ACT)
    return pl.pallas_call(
        functools.partial(_proj_kernel, tc=tc, tm=tm, tn=tn, n_b=n_b),
        grid=(n_b, ta // tm, P_WIDTH // tn),
        in_specs=[pl.BlockSpec((1, tm, d), lambda b, i, j: (b, i, 0)),
                  pl.BlockSpec((8, 3 * d), lambda b, i, j: (0, 0)),
                  pl.BlockSpec((1, d), lambda b, i, j: (0, 0)),
                  pl.BlockSpec((d, tn), lambda b, i, j: (0, j)),
                  pl.BlockSpec((d, LANES), lambda b, i, j: (0, 0))],
        out_specs=[pl.BlockSpec((1, tm, tn), lambda b, i, j: (b, i, j)),
                   pl.BlockSpec((1, tm, LANES), lambda b, i, j: (b, i, 0))],
        out_shape=[jax.ShapeDtypeStruct((n_b, ta, P_WIDTH), BF16),
                   jax.ShapeDtypeStruct((n_b, ta, LANES), F32)],
        scratch_shapes=[pltpu.VMEM((tm, d), BF16)],
        compiler_params=_cparams(("arbitrary", "arbitrary", "arbitrary")),
        name="in_proj",
    )(xa, mod_l, nw, wp, wab)


def _prep_kernel(main_ref, prev_ref, next_ref, rqk_ref, pab_ref, cw_ref, cos_ref, sin_ref, alog_ref, dtb_ref,
                 dq_ref, dk_ref, dkt_ref, dv_ref, rq_ref, rkt_ref, gcol_ref, grow_ref, dktf_ref, growf_ref,
                 xs_ref, *, tc, n_blk):
    i = pl.program_id(1)
    tb = TOK_BLK
    nctx = tc // tb
    has_prev = jnp.logical_and(i != 0, i != nctx)
    has_next = jnp.logical_and(i != nctx - 1, i != n_blk - 1)
    pm = jnp.where(has_prev, 1.0, 0.0)
    nm = jnp.where(has_next, 1.0, 0.0)
    n_ch = tb // DN_CHUNK

    for g in range(16):
        ls = slice(g * LANES, (g + 1) * LANES)
        xs_ref[8:8 + tb, :] = main_ref[0, :, ls].astype(F32)
        xs_ref[0:8, :] = prev_ref[0, 8:16, ls].astype(F32) * pm
        xs_ref[8 + tb:16 + tb, :] = next_ref[0, 0:8, ls].astype(F32) * nm
        acc = cw_ref[0:1, ls] * xs_ref[6:6 + tb, :]
        for t in range(1, CONV_W):
            acc = acc + cw_ref[t:t + 1, ls] * xs_ref[6 + t:6 + t + tb, :]
        y = _silu(acc)
        if g < 8:
            y = y * lax.rsqrt(jnp.sum(y * y, axis=-1, keepdims=True) + NORM_EPS)
        if g < 4:
            dq_ref[0, :, ls] = (y * (DN_DK ** -0.5)).astype(BF16)
        elif g < 8:
            h = g - 4
            hs = slice(h * LANES, (h + 1) * LANES)
            dk_ref[0, :, hs] = y.astype(BF16)
            dktf_ref[0, h] = y.T.astype(BF16)
            for c in range(n_ch):
                dkt_ref[0, h, c] = y[c * DN_CHUNK:(c + 1) * DN_CHUNK, :].T.astype(BF16)
        else:
            vs = slice((g - 8) * LANES, (g - 7) * LANES)
            dv_ref[0, :, vs] = y.astype(BF16)

    pab = pab_ref[0]
    g_log = -jnp.exp(alog_ref[...]) * _softplus(pab + dtb_ref[...])
    beta = _sigmoid(pab)
    r = _iota((tb, tb), 0)
    c_ = _iota((tb, tb), 1)
    same = _shr(r, DN_CHUNK) == _shr(c_, DN_CHUNK)
    m_f = jnp.where(jnp.logical_and(same, r >= c_), 1.0, 0.0)
    m_b = jnp.where(jnp.logical_and(same, r <= c_), 1.0, 0.0)
    cs_f = jnp.dot(m_f, g_log, precision=lax.Precision.HIGHEST, preferred_element_type=F32)
    cs_b = jnp.dot(m_b, g_log, precision=lax.Precision.HIGHEST, preferred_element_type=F32)
    lane = _iota((tb, LANES), 1)
    gcol = jnp.where(lane < 4, cs_f, jnp.where(lane < 8, cs_b, beta))
    gcol_ref[0] = gcol
    growf_ref[0] = gcol.T[0:8, :]
    for c in range(n_ch):
        grow_ref[0, c] = gcol[c * DN_CHUNK:(c + 1) * DN_CHUNK, :].T[0:8, :]

    cosv = cos_ref[...]
    sinv = sin_ref[...]
    even_lane = (_iota((tb, LANES), 1) & 1) == 0
    for g in range(8):
        ls = slice(g * LANES, (g + 1) * LANES)
        x = rqk_ref[0, :, ls].astype(F32)
        partner = jnp.where(even_lane, pltpu.roll(x, LANES - 1, 1), pltpu.roll(x, 1, 1))
        y = x * cosv + partner * sinv
        if g < 4:
            rq_ref[0, :, ls] = y.astype(BF16)
        else:
            rkt_ref[0, g - 4] = (y * (RET_DK ** -0.5)).T.astype(BF16)


def _prep(p, pab, conv_w, cos_t, sin_t, alog_row, dtb_row, tc):
    n_b, ta, _ = p.shape
    tb = TOK_BLK
    n_blk = ta // tb
    n_ch = tb // DN_CHUNK
    hpb = tb // 16
    n16 = ta // 16
    out_shape = [
        jax.ShapeDtypeStruct((n_b, ta, 512), BF16),
        jax.ShapeDtypeStruct((n_b, ta, 512), BF16),
        jax.ShapeDtypeStruct((n_b, DN_HEADS, ta // DN_CHUNK, DN_DK, DN_CHUNK), BF16),
        jax.ShapeDtypeStruct((n_b, ta, 1024), BF16),
        jax.ShapeDtypeStruct((n_b, ta, 512), BF16),
        jax.ShapeDtypeStruct((n_b, RET_HEADS, RET_DK, ta), BF16),
        jax.ShapeDtypeStruct((n_b, ta, LANES), F32),
        jax.ShapeDtypeStruct((n_b, ta // DN_CHUNK, 8, DN_CHUNK), F32),
        jax.ShapeDtypeStruct((n_b, DN_HEADS, DN_DK, ta), BF16),
        jax.ShapeDtypeStruct((n_b, 8, ta), F32),
    ]
    out_specs = [
        pl.BlockSpec((1, tb, 512), lambda b, i: (b, i, 0)),
        pl.BlockSpec((1, tb, 512), lambda b, i: (b, i, 0)),
        pl.BlockSpec((1, DN_HEADS, n_ch, DN_DK, DN_CHUNK), lambda b, i: (b, 0, i, 0, 0)),
        pl.BlockSpec((1, tb, 1024), lambda b, i: (b, i, 0)),
        pl.BlockSpec((1, tb, 512), lambda b, i: (b, i, 0)),
        pl.BlockSpec((1, RET_HEADS, RET_DK, tb), lambda b, i: (b, 0, 0, i)),
        pl.BlockSpec((1, tb, LANES), lambda b, i: (b, i, 0)),
        pl.BlockSpec((1, n_ch, 8, DN_CHUNK), lambda b, i: (b, i, 0, 0)),
        pl.BlockSpec((1, DN_HEADS, DN_DK, tb), lambda b, i: (b, 0, 0, i)),
        pl.BlockSpec((1, 8, tb), lambda b, i: (b, 0, i)),
    ]
    in_specs = [
        pl.BlockSpec((1, tb, 2048), lambda b, i: (b, i, 0)),
        pl.BlockSpec((1, 16, 2048), lambda b, i: (b, jnp.maximum(i * hpb - 1, 0), 0)),
        pl.BlockSpec((1, 16, 2048), lambda b, i: (b, jnp.minimum((i + 1) * hpb, n16 - 1), 0)),
        pl.BlockSpec((1, tb, 1024), lambda b, i: (b, i, C_RQ // 1024)),
        pl.BlockSpec((1, tb, LANES), lambda b, i: (b, i, 0)),
        pl.BlockSpec((CONV_W, 2048), lambda b, i: (0, 0)),
        pl.BlockSpec((tb, LANES), lambda b, i: (i, 0)),
        pl.BlockSpec((tb, LANES), lambda b, i: (i, 0)),
        pl.BlockSpec((1, LANES), lambda b, i: (0, 0)),
        pl.BlockSpec((1, LANES), lambda b, i: (0, 0)),
    ]
    return pl.pallas_call(
        functools.partial(_prep_kernel, tc=tc, n_blk=n_blk),
        grid=(n_b, n_blk),
        in_specs=in_specs,
        out_specs=out_specs,
        out_shape=out_shape,
        scratch_shapes=[pltpu.VMEM((tb + 16, LANES), F32)],
        compiler_params=_cparams(("arbitrary", "arbitrary")),
        name="prep",
    )(p, p, p, p, pab, conv_w, cos_t, sin_t, alog_row, dtb_row)


def _blk_index(n, n_blk, nctx, reverse):
    if not reverse:
        return n
    return jnp.where(n < nctx, nctx - 1 - n, n_blk - 1 - (n - nctx))


DN_MASK_NEG, DN_MASK_STRICT, DN_MASK_B8, DN_MASK_EYE, DN_MASK_OFF0 = 0, 1, 2, 3, 4
DN_MERGE_SIZES = (8, 16, 32)


DN_PAIR = 2 * DN_CHUNK


def _dn_masks(reverse):
    r = np.arange(DN_PAIR)[:, None]
    c = np.arange(DN_PAIR)[None, :]
    same = (r // DN_CHUNK) == (c // DN_CHUNK)
    incl = same & ((r <= c) if reverse else (r >= c))
    strict = incl & (r != c)
    ms = [np.where(incl, 0.0, NEG_BIG), strict, (r // 8) == (c // 8), r == c]
    for s in DN_MERGE_SIZES:
        ms.append(((r // (2 * s)) == (c // (2 * s))) & ((r // s) != (c // s)))
    return jnp.asarray(np.stack([np.asarray(m, np.float32) for m in ms]))


def _dn_streams(n, q_ref, k_ref, ktf_ref, ktc_ref, v_ref, gcol_ref, growf_ref, growc_ref, m_ref, o_ref, s_ref,
                wq_s, u_s, attn_s, kdt_s, egl_s, reverse):
    c_len = DN_CHUNK
    n_ch = TOK_BLK // c_len
    d_off = 4 if reverse else 0
    heads = range(DN_HEADS)
    last = 0 if reverse else c_len - 1
    pairs = [slice(g * DN_PAIR, (g + 1) * DN_PAIR) for g in range(TOK_BLK // DN_PAIR)]
    units = [(h, g) for h in heads for g in range(len(pairs))]

    wr = lax.rem(n, 2)
    rd = 1 - wr

    @pl.when(n == 0)
    def _():
        s_ref[...] = jnp.zeros_like(s_ref)
        wq_s[...] = jnp.zeros_like(wq_s)
        u_s[...] = jnp.zeros_like(u_s)
        attn_s[...] = jnp.zeros_like(attn_s)
        kdt_s[...] = jnp.zeros_like(kdt_s)
        egl_s[...] = jnp.zeros_like(egl_s)

    us = range(len(units))
    pa = {}

    def a_load():
        gcol = gcol_ref[0]
        growf = growf_ref[0]
        pa["gc"] = [gcol[:, d_off + h:d_off + h + 1] for h in heads]
        pa["bc"] = [gcol[:, 8 + d_off + h:9 + d_off + h] for h in heads]
        pa["egc"] = [jnp.exp(x) for x in pa["gc"]]
        pa["kb"] = [k_ref[0, :, h * DN_DK:(h + 1) * DN_DK].astype(F32) * pa["bc"][h] for h in heads]
        dec = [jnp.exp((pa["gc"][h][pairs[g]] - growf[d_off + h:d_off + h + 1, pairs[g]]) + m_ref[DN_MASK_NEG])
               for h, g in units]
        pa["a"] = [_dot(pa["kb"][h][pairs[g]], ktf_ref[0, h, :, pairs[g]]) * dec[u] * m_ref[DN_MASK_STRICT]
                   for u, (h, g) in enumerate(units)]

    def a_sq():
        pa["d"] = [(pa["a"][u] * m_ref[DN_MASK_B8]).astype(BF16) for u in us]
        pa["d2"] = [jnp.dot(pa["d"][u], pa["d"][u], preferred_element_type=F32) for u in us]

    def a_pow():
        d2b = [x.astype(BF16) for x in pa["d2"]]
        pa["d4"] = [jnp.dot(d2b[u], d2b[u], preferred_element_type=F32) for u in us]
        pa["d3"] = [jnp.dot(pa["d"][u], d2b[u], preferred_element_type=F32) for u in us]

    def a_base():
        p1 = [m_ref[DN_MASK_EYE] - pa["d"][u].astype(F32) + pa["d2"][u] - pa["d3"][u] for u in us]
        pa["t"] = [p1[u] + _dot(p1[u], pa["d4"][u]) for u in us]

    def a_merge_x(lvl):
        def run():
            pa["tb"] = [x.astype(BF16) for x in pa["t"]]
            pa["x"] = [_dot(pa["a"][u] * m_ref[DN_MASK_OFF0 + lvl], pa["tb"][u]) for u in us]
        return run

    def a_merge_t():
        pa["t"] = [pa["t"][u] - _dot(pa["tb"][u], pa["x"][u]) for u in us]

    def a_store():
        neg64 = m_ref[DN_MASK_NEG, 0:c_len, 0:c_len]
        for u, (h, g) in enumerate(units):
            rows = pairs[g]
            rhs = jnp.concatenate([pa["kb"][h][rows] * pa["egc"][h][rows],
                                   v_ref[0, rows, h * DN_DV:(h + 1) * DN_DV].astype(F32) * pa["bc"][h][rows]], axis=1)
            wu = _dot(pa["t"][u], rhs)
            qd = q_ref[0, rows, h * DN_DK:(h + 1) * DN_DK].astype(F32) * pa["egc"][h][rows]
            u_s[wr, h, rows, :] = wu[:, DN_DK:DN_DK + DN_DV]
            for ci in range(DN_PAIR // c_len):
                c = g * (DN_PAIR // c_len) + ci
                cs = slice(ci * c_len, (ci + 1) * c_len)
                wq_s[wr, h, c] = jnp.concatenate([wu[cs, 0:DN_DK], qd[cs]], axis=0).astype(BF16)
        for c in range(n_ch):
            rs = slice(c * c_len, (c + 1) * c_len)
            growc = growc_ref[0, c]
            for h in heads:
                gr = growc[d_off + h:d_off + h + 1, :]
                gl = gr[:, last:last + 1]
                kt = ktc_ref[0, h, c]
                attn = _dot(q_ref[0, rs, h * DN_DK:(h + 1) * DN_DK], kt) * jnp.exp((pa["gc"][h][rs] - gr) + neg64)
                attn_s[wr, h, c] = attn.astype(BF16)
                kdt_s[wr, h, c] = (kt.astype(F32) * jnp.exp(gl - gr)).astype(BF16)
                egl_s[wr, h, c] = jnp.broadcast_to(jnp.exp(gl), (8, LANES))

    pb = {"state": [s_ref[h] for h in heads]}

    def b_first(c):
        def run():
            pb["ws"] = [jnp.dot(wq_s[rd, h, c], pb["state"][h].astype(BF16), preferred_element_type=F32)
                        for h in heads]
        return run

    def b_second(c):
        def run():
            rs = slice(c * c_len, (c + 1) * c_len)
            vn = [(u_s[rd, h, rs, :] - pb["ws"][h][0:c_len]).astype(BF16) for h in heads]
            o = [pb["ws"][h][c_len:2 * c_len] + jnp.dot(attn_s[rd, h, c], vn[h], preferred_element_type=F32)
                 for h in heads]
            pb["state"] = [pb["state"][h] * egl_s[rd, h, c][0:1, 0:1]
                           + jnp.dot(kdt_s[rd, h, c], vn[h], preferred_element_type=F32) for h in heads]
            for h in heads:
                o_ref[0, rs, h * DN_DV:(h + 1) * DN_DV] = o[h].astype(BF16)
        return run

    a_stages = [a_load, a_sq, a_pow, a_base]
    for lvl in range(len(DN_MERGE_SIZES)):
        a_stages += [a_merge_x(lvl), a_merge_t]
    a_stages.append(a_store)
    b_stages = []
    for cc in range(n_ch):
        c = (n_ch - 1 - cc) if reverse else cc
        b_stages += [b_first(c), b_second(c)]

    def finish():
        for h in heads:
            s_ref[h] = pb["state"][h]

    return a_stages, b_stages, finish


DN_N_IN = 9
DN_N_SCRATCH = 6


def _dn_kernel(*refs):
    n = pl.program_id(1)
    ins_f, ins_b = refs[0:DN_N_IN], refs[DN_N_IN:2 * DN_N_IN]
    o_f, o_b = refs[2 * DN_N_IN], refs[2 * DN_N_IN + 1]
    scr = refs[2 * DN_N_IN + 2:]
    streams = [_dn_streams(n, *ins_f, o_f, *scr[0:DN_N_SCRATCH], reverse=False),
               _dn_streams(n, *ins_b, o_b, *scr[DN_N_SCRATCH:], reverse=True)]
    n_stage = max(max(len(a), len(b)) for a, b, _ in streams)
    for i in range(n_stage):
        for a_stages, _, _ in streams:
            if i < len(a_stages):
                a_stages[i]()
        for _, b_stages, _ in streams:
            if i < len(b_stages):
                b_stages[i]()
    for _, _, finish in streams:
        finish()


def _dn_scan(dq, dk, dktf, dktc, dv, gcol, growf, growc, tc):
    n_b, ta, _ = dq.shape
    tb = TOK_BLK
    n_blk = ta // tb
    nctx = tc // tb
    n_ch = tb // DN_CHUNK
    in_specs, args, out_specs, scratch = [], [], [], []
    for reverse in (False, True):
        masks = _dn_masks(reverse)
        blk = functools.partial(_blk_index, n_blk=n_blk, nctx=nctx, reverse=reverse)
        bi = lambda n, blk=blk: blk(jnp.minimum(n, n_blk - 1))
        bo = lambda n, blk=blk: blk(jnp.maximum(n - 1, 0))
        in_specs += [pl.BlockSpec((1, tb, 512), lambda b, n, bi=bi: (b, bi(n), 0)),
                     pl.BlockSpec((1, tb, 512), lambda b, n, bi=bi: (b, bi(n), 0)),
                     pl.BlockSpec((1, DN_HEADS, DN_DK, tb), lambda b, n, bi=bi: (b, 0, 0, bi(n))),
                     pl.BlockSpec((1, DN_HEADS, n_ch, DN_DK, DN_CHUNK), lambda b, n, bi=bi: (b, 0, bi(n), 0, 0)),
                     pl.BlockSpec((1, tb, 1024), lambda b, n, bi=bi: (b, bi(n), 0)),
                     pl.BlockSpec((1, tb, LANES), lambda b, n, bi=bi: (b, bi(n), 0)),
                     pl.BlockSpec((1, 8, tb), lambda b, n, bi=bi: (b, 0, bi(n))),
                     pl.BlockSpec((1, n_ch, 8, DN_CHUNK), lambda b, n, bi=bi: (b, bi(n), 0, 0)),
                     pl.BlockSpec(masks.shape, lambda b, n: (0, 0, 0))]
        args += [dq, dk, dktf, dktc, dv, gcol, growf, growc, masks]
        out_specs.append(pl.BlockSpec((1, tb, 1024), lambda b, n, bo=bo: (b, bo(n), 0)))
        scratch += [pltpu.VMEM((DN_HEADS, DN_DK, DN_DV), F32),
                    pltpu.VMEM((2, DN_HEADS, n_ch, 2 * DN_CHUNK, DN_DK), BF16),
                    pltpu.VMEM((2, DN_HEADS, tb, DN_DV), F32),
                    pltpu.VMEM((2, DN_HEADS, n_ch, DN_CHUNK, DN_CHUNK), BF16),
                    pltpu.VMEM((2, DN_HEADS, n_ch, DN_DK, DN_CHUNK), BF16),
                    pltpu.VMEM((2, DN_HEADS, n_ch, 8, LANES), F32)]
    assert len(args) == 2 * DN_N_IN and len(scratch) == 2 * DN_N_SCRATCH
    return pl.pallas_call(
        _dn_kernel,
        grid=(n_b, n_blk + 1),
        in_specs=in_specs,
        out_specs=out_specs,
        out_shape=[jax.ShapeDtypeStruct((n_b, ta, 1024), BF16)] * 2,
        scratch_shapes=scratch,
        compiler_params=_cparams(("arbitrary", "arbitrary")),
        name="dn_scan",
    )(*args)


def _ret_kernel(qf_ref, ktf_ref, vf_ref, qb_ref, ktb_ref, vb_ref, lam_ref, of_ref, ob_ref,
                s_ref, dec_ref, eq_ref, ek_ref, egl_ref):
    n = pl.program_id(1)
    c_len = TOK_BLK
    dirs = ((0, qf_ref, ktf_ref, vf_ref, of_ref), (1, qb_ref, ktb_ref, vb_ref, ob_ref))
    streams = [(d, h) for d in range(2) for h in range(RET_HEADS)]

    @pl.when(n == 0)
    def _():
        s_ref[...] = jnp.zeros_like(s_ref)
        ri = _iota((c_len, c_len), 0)
        ci = _iota((c_len, c_len), 1)
        rowpos = _iota((c_len, LANES), 0)
        colpos = _iota((RET_DK, c_len), 1)
        for d, h in streams:
            reverse = d == 1
            dist = (ci - ri) if reverse else (ri - ci)
            if reverse:
                qexp = (c_len - rowpos).astype(F32)
                kexp = colpos.astype(F32)
            else:
                qexp = (rowpos + 1).astype(F32)
                kexp = (c_len - 1 - colpos).astype(F32)
            x = lam_ref[0:1, 4 * d + h:4 * d + h + 1]
            lam = jnp.minimum(x, 0.0) - jnp.log(1.0 + jnp.exp(-jnp.abs(x)))
            dec_ref[d, h] = jnp.exp(jnp.where(dist >= 0, dist.astype(F32) * lam, NEG_BIG))
            eq_ref[d, h] = jnp.exp(qexp * lam)
            ek_ref[d, h] = jnp.exp(kexp * lam)
            egl_ref[d, h] = jnp.broadcast_to(jnp.exp(lam * float(c_len)), (8, LANES))

    def q_of(d, h):
        return dirs[d][1][0, :, h * RET_DK:(h + 1) * RET_DK]

    def v_of(d, h):
        return dirs[d][3][0, :, h * RET_DV:(h + 1) * RET_DV]

    s_old = {u: s_ref[u[0], u[1]] for u in streams}
    qk = {u: _dot(q_of(*u), dirs[u[0]][2][0, u[1]]) for u in streams}
    kv = {u: _dot(dirs[u[0]][2][0, u[1]].astype(F32) * ek_ref[u[0], u[1]], v_of(*u)) for u in streams}
    qs = {u: _dot(q_of(*u).astype(F32) * eq_ref[u[0], u[1]], s_old[u]) for u in streams}
    av = {u: _dot(qk[u] * dec_ref[u[0], u[1]], v_of(*u)) for u in streams}
    for u in streams:
        d, h = u
        s_ref[d, h] = s_old[u] * egl_ref[d, h][0:1, 0:1] + kv[u]
        dirs[d][4][0, :, h * RET_DV:(h + 1) * RET_DV] = (av[u] + qs[u]).astype(BF16)


def _ret_scan(rq, rkt, p, lam_row, tc):
    n_b, ta, _ = rq.shape
    tb = TOK_BLK
    n_blk = ta // tb
    nctx = tc // tb
    in_specs, out_specs = [], []
    for reverse in (False, True):
        bi = functools.partial(_blk_index, n_blk=n_blk, nctx=nctx, reverse=reverse)
        in_specs += [pl.BlockSpec((1, tb, 512), lambda b, n, bi=bi: (b, bi(n), 0)),
                     pl.BlockSpec((1, RET_HEADS, RET_DK, tb), lambda b, n, bi=bi: (b, 0, 0, bi(n))),
                     pl.BlockSpec((1, tb, 1024), lambda b, n, bi=bi: (b, bi(n), C_RV // 1024))]
        out_specs.append(pl.BlockSpec((1, tb, 1024), lambda b, n, bi=bi: (b, bi(n), 0)))
    in_specs.append(pl.BlockSpec((1, LANES), lambda b, n: (0, 0)))
    return pl.pallas_call(
        _ret_kernel,
        grid=(n_b, n_blk),
        in_specs=in_specs,
        out_specs=out_specs,
        out_shape=[jax.ShapeDtypeStruct((n_b, ta, 1024), BF16)] * 2,
        scratch_shapes=[pltpu.VMEM((2, RET_HEADS, RET_DK, RET_DV), F32),
                        pltpu.VMEM((2, RET_HEADS, tb, tb), F32),
                        pltpu.VMEM((2, RET_HEADS, tb, LANES), F32),
                        pltpu.VMEM((2, RET_HEADS, RET_DK, tb), F32),
                        pltpu.VMEM((2, RET_HEADS, 8, LANES), F32)],
        compiler_params=_cparams(("arbitrary", "arbitrary")),
        name="ret_scan",
    )(rq, rkt, p, rq, rkt, p, lam_row)


HG_LEVELS = (1, 2, 4, 8, 16, 32)


def _neg_abs(x):
    bits = lax.bitcast_convert_type(x, jnp.uint32) | jnp.uint32(0x80000000)
    return lax.bitcast_convert_type(bits, F32)


def _hg_pair_masks(reverse):
    i = np.arange(HG_CHUNK)[:, None]
    j = np.arange(HG_CHUNK)[None, :]
    ms = []
    for s in HG_LEVELS:
        q_half = 0 if reverse else 1
        ms.append(((i // (2 * s)) == (j // (2 * s))) & (((i // s) & 1) == q_half) & (((j // s) & 1) == 1 - q_half))
    ms.append(i == j)
    return jnp.asarray(np.stack(ms).astype(np.float32))


def _hg_boundary(gc_s, gc, base, hs, s, row8, reverse):
    c_len = HG_CHUNK
    off = s - 1 if reverse else s
    if s == 1:
        rowi = _iota((c_len, HG_DK), 0)
        if reverse:
            return jnp.where((rowi & 1) == 1, pltpu.roll(gc, 1, 0), gc)
        return jnp.where((rowi & 1) == 0, pltpu.roll(gc, c_len - 1, 0), gc)
    parts = []
    for vi in range(c_len // 8):
        r0 = base + 8 * vi
        if s == 2:
            lo = jnp.broadcast_to(gc_s[r0 + off:r0 + off + 1, hs], (8, HG_DK))
            hi = jnp.broadcast_to(gc_s[r0 + 4 + off:r0 + 5 + off, hs], (8, HG_DK))
            parts.append(jnp.where(row8 < 4, lo, hi))
        else:
            m = base + ((8 * vi) // (2 * s)) * (2 * s) + off
            parts.append(jnp.broadcast_to(gc_s[m:m + 1, hs], (8, HG_DK)))
    return jnp.concatenate(parts, axis=0)


def _hg_kernel(xq_ref, xf_ref, xi_ref, low_ref, pm_ref, o_ref, st_ref, k_s, gc_s, *, reverse):
    n = pl.program_id(1)
    tb = TOK_BLK
    c_len = HG_CHUNK
    n_ch = tb // c_len

    @pl.when(n == 0)
    def _():
        st_ref[...] = jnp.zeros_like(st_ref)

    low = low_ref[...]
    f = low + (1.0 - low) * xf_ref[0].astype(F32)
    g = jnp.log2(f)
    r = _iota((tb, tb), 0)
    c_ = _iota((tb, tb), 1)
    same = _shr(r, c_len) == _shr(c_, c_len)
    tri = (r <= c_) if reverse else (r >= c_)
    m = jnp.where(jnp.logical_and(same, tri), 1.0, 0.0).astype(BF16)
    g0 = g.astype(BF16)
    r1 = g - g0.astype(F32)
    g1 = r1.astype(BF16)
    g2 = (r1 - g1.astype(F32)).astype(BF16)
    gc_s[...] = (jnp.dot(m, g0, preferred_element_type=F32) + jnp.dot(m, g1, preferred_element_type=F32)
                 + jnp.dot(m, g2, preferred_element_type=F32))
    k_s[...] = (1.0 - f).astype(BF16)

    row8 = _iota((8, HG_DK), 0)
    last = 0 if reverse else c_len - 1
    hsl = [slice(h * HG_DK, (h + 1) * HG_DK) for h in range(HG_HEADS)]
    heads = range(HG_HEADS)

    def scores(c):
        base = c * c_len
        rs = slice(base, base + c_len)
        q = {h: xq_ref[0, rs, hsl[h]].astype(F32) for h in heads}
        k = {h: k_s[rs, hsl[h]].astype(F32) for h in heads}
        gc = {h: gc_s[rs, hsl[h]] for h in heads}
        tiles = range(c_len // 8)
        t8 = lambda x, v: x[8 * v:8 * v + 8]
        diag = pm_ref[len(HG_LEVELS)]
        a = {h: [t8(x, v) for v in tiles] for h, x in ((h, _dot_nt(q[h], k[h]) * diag) for h in heads)}
        q_half = 0 if reverse else 1
        for lvl, s in enumerate(HG_LEVELS):
            pm = pm_ref[lvl]
            if s < 8:
                is_q = (_shr(row8, s) & 1) == q_half
                pick = lambda h, v: jnp.where(is_q, t8(q[h], v), t8(k[h], v))
                q_tiles = list(tiles)
            else:
                pick = lambda h, v: t8(q[h], v) if ((8 * v) // s) & 1 == q_half else t8(k[h], v)
                q_tiles = [v for v in tiles if ((8 * v) // s) & 1 == q_half]
            z = {}
            for h in heads:
                e = jnp.exp2(_neg_abs(gc[h] - _hg_boundary(gc_s, gc[h], base, hsl[h], s, row8, reverse)))
                z[h] = jnp.concatenate([pick(h, v) * t8(e, v) for v in tiles], axis=0)
            pr = {h: _dot_nt(z[h], z[h]) for h in heads}
            for h in heads:
                for v in q_tiles:
                    a[h][v] = a[h][v] + t8(pr[h], v) * t8(pm, v)
        return {h: jnp.concatenate(a[h], axis=0) for h in heads}

    def outputs(c, a):
        base = c * c_len
        rs = slice(base, base + c_len)
        q = {h: xq_ref[0, rs, hsl[h]].astype(F32) for h in heads}
        k = {h: k_s[rs, hsl[h]].astype(F32) for h in heads}
        gc = {h: gc_s[rs, hsl[h]] for h in heads}
        gl = {h: gc[h][last:last + 1, :] for h in heads}
        st = {h: st_ref[h] for h in heads}
        o = {h: _dot(a[h], xi_ref[0, rs, hsl[h]]) + _dot_nt(q[h] * jnp.exp2(gc[h]), st[h]) for h in heads}
        for h in heads:
            kd = k[h] * jnp.exp2(gl[h] - gc[h])
            st_ref[h] = st[h] * jnp.exp2(gl[h]) + _dot_tn(xi_ref[0, rs, hsl[h]], kd)
            o_ref[0, rs, hsl[h]] = o[h].astype(BF16)

    order = [(n_ch - 1 - cc) if reverse else cc for cc in range(n_ch)]
    pending = None
    for c in order:
        a = scores(c)
        if pending is not None:
            outputs(*pending)
        pending = (c, a)
    outputs(*pending)


def _hg_scan(p, low_row, tc, reverse):
    n_b, ta, _ = p.shape
    tb = TOK_BLK
    n_blk = ta // tb
    nctx = tc // tb
    d_i = 1 if reverse else 0
    masks = _hg_pair_masks(reverse)
    bi = functools.partial(_blk_index, n_blk=n_blk, nctx=nctx, reverse=reverse)
    return pl.pallas_call(
        functools.partial(_hg_kernel, reverse=reverse),
        grid=(n_b, n_blk),
        in_specs=[pl.BlockSpec((1, tb, 1024), lambda b, n: (b, bi(n), C_HQ // 1024)),
                  pl.BlockSpec((1, tb, 1024), lambda b, n: (b, bi(n), C_HF // 1024 + d_i)),
                  pl.BlockSpec((1, tb, 1024), lambda b, n: (b, bi(n), C_HI // 1024)),
                  pl.BlockSpec((1, 1024), lambda b, n: (0, 0)),
                  pl.BlockSpec(masks.shape, lambda b, n: (0, 0, 0))],
        out_specs=pl.BlockSpec((1, tb, 1024), lambda b, n: (b, bi(n), 0)),
        out_shape=jax.ShapeDtypeStruct((n_b, ta, 1024), BF16),
        scratch_shapes=[pltpu.VMEM((HG_HEADS, HG_DV, HG_DK), F32),
                        pltpu.VMEM((tb, 1024), BF16),
                        pltpu.VMEM((tb, 1024), F32)],
        compiler_params=_cparams(("arbitrary", "arbitrary")),
        name="hg_bwd" if reverse else "hg_fwd",
    )(p, p, p, low_row, masks)


def _branch(of_ref, ob_ref, gate_ref, nw_ref, dv):
    o = of_ref[0].astype(F32) + ob_ref[0].astype(F32)
    gate = gate_ref[0].astype(F32)
    nw = nw_ref[...]
    parts = []
    for h in range(1024 // dv):
        x = o[:, h * dv:(h + 1) * dv]
        y = x * lax.rsqrt(jnp.mean(x * x, axis=-1, keepdims=True) + NORM_EPS) * nw
        parts.append(y)
    return (jnp.concatenate(parts, axis=-1) * gate).astype(BF16)


def _merge_kernel(dnf, dnb, rtf, rtb, hgf, hgb, g_dn, g_rt, g_hg, m0, m1, m2, x_ref, mod_ref,
                  nw_dn, nw_rt, nw_hg, wb_ref, wo_ref, fnw_ref, o_ref, *, tc, blk_off, n_b, final):
    b = pl.program_id(0)
    i = pl.program_id(1) + blk_off
    d = D_MODEL
    brs = (_branch(dnf, dnb, g_dn, nw_dn, DN_DV),
           _branch(rtf, rtb, g_rt, nw_rt, RET_DV),
           _branch(hgf, hgb, g_hg, nw_hg, HG_DV))
    y = None
    for idx, (br, mg) in enumerate(zip(brs, (m0, m1, m2))):
        t = jnp.dot(br, wb_ref[idx], preferred_element_type=F32) * mg[0].astype(F32)
        y = t if y is None else y + t
    out = jnp.dot(y.astype(BF16), wo_ref[...], preferred_element_type=F32)
    is_ctx = (i * TOK_BLK) < tc
    gt = jnp.where(is_ctx, mod_ref[n_b:n_b + 1, 2 * d:3 * d], mod_ref[pl.ds(b, 1), 2 * d:3 * d])
    xn = x_ref[0] + gt * out
    if final:
        xn = xn * lax.rsqrt(jnp.mean(xn * xn, axis=-1, keepdims=True) + NORM_EPS) * fnw_ref[...]
    o_ref[0] = xn


def _merge(outs, p, xa, mod_l, nws, wb, wo, fnw, tc, final):
    n_b, ta, d = xa.shape
    tb = TOK_BLK
    blk_off = tc // tb if final else 0
    n_blk = ta // tb - blk_off
    rows = lambda b, i: (b, i + blk_off, 0)
    pcol = lambda cb: (lambda b, i: (b, i + blk_off, cb))
    const2 = lambda b, i: (0, 0)
    in_specs = ([pl.BlockSpec((1, tb, 1024), rows)] * 6
                + [pl.BlockSpec((1, tb, 1024), pcol(C_DNGATE // 1024)),
                   pl.BlockSpec((1, tb, 1024), pcol(C_RGATE // 1024)),
                   pl.BlockSpec((1, tb, 1024), pcol(C_HGATE // 1024)),
                   pl.BlockSpec((1, tb, 1024), pcol(C_MERGE0 // 1024)),
                   pl.BlockSpec((1, tb, 1024), pcol(C_MERGE1 // 1024)),
                   pl.BlockSpec((1, tb, 1024), pcol(C_MERGE2 // 1024)),
                   pl.BlockSpec((1, tb, d), rows),
                   pl.BlockSpec((8, 3 * d), const2),
                   pl.BlockSpec((1, DN_DV), const2),
                   pl.BlockSpec((1, RET_DV), const2),
                   pl.BlockSpec((1, HG_DV), const2),
                   pl.BlockSpec((N_BRANCH, 1024, d), lambda b, i: (0, 0, 0)),
                   pl.BlockSpec((d, d), const2),
                   pl.BlockSpec((1, d), const2)])
    out_rows = ta - blk_off * tb
    return pl.pallas_call(
        functools.partial(_merge_kernel, tc=tc, blk_off=blk_off, n_b=n_b, final=final),
        grid=(n_b, n_blk),
        in_specs=in_specs,
        out_specs=pl.BlockSpec((1, tb, d), lambda b, i: (b, i, 0)),
        out_shape=jax.ShapeDtypeStruct((n_b, out_rows, d), F32),
        compiler_params=_cparams(("arbitrary", "arbitrary")),
        name="merge_final" if final else "merge",
    )(*outs, p, p, p, p, p, p, xa, mod_l, *nws, wb, wo, fnw)


def _permute_w_in(w):
    d = w.shape[0]
    ab0 = 2048 + 1024
    wp = jnp.concatenate([w[:, :ab0], w[:, ab0 + 16:]], axis=1).astype(BF16)
    wab = jnp.pad(w[:, ab0:ab0 + 16], ((0, 0), (0, LANES - 16))).astype(BF16)
    return wp, wab


def _rope_tables(n_lat, tc):
    half = RET_DK // 2
    inv = ROPE_BASE ** (-np.arange(0, half, 2, dtype=np.float64) / half)
    t = np.arange(n_lat)
    row = (t // GRID_W).astype(np.float64)
    col = (t % GRID_W).astype(np.float64)
    ang = np.concatenate([row[:, None] * inv, col[:, None] * inv], axis=-1)
    cos, sin = np.cos(ang), np.sin(ang)
    cos_l = np.repeat(cos, 2, axis=-1)
    sin_l = np.stack([-sin, sin], axis=-1).reshape(n_lat, LANES)
    cos_all = np.concatenate([np.ones((tc, LANES)), cos_l], axis=0).astype(np.float32)
    sin_all = np.concatenate([np.zeros((tc, LANES)), sin_l], axis=0).astype(np.float32)
    return jnp.asarray(cos_all), jnp.asarray(sin_all)


def _pad_row(v, width=LANES):
    v = v.reshape(1, -1).astype(F32)
    return jnp.pad(v, ((0, 0), (0, width - v.shape[1])))


def kernel(x, c, ctx, c_ctx, norm_w, ada_w, ada_b, w_in, dn_conv, dn_a_log, dn_dt_bias, dn_norm_w,
           ret_decay, ret_norm_w, hg_lb, hg_norm_w, w_branch, w_out, final_norm_w):
    n_b, n_lat, d = x.shape
    tc = ctx.shape[1]
    depth = norm_w.shape[0]
    assert d == D_MODEL and tc % TOK_BLK == 0 and n_lat % TOK_BLK == 0 and n_b + 1 <= 8
    assert n_lat % GRID_W == 0

    xa = jnp.concatenate([ctx, x], axis=1).astype(F32)
    cin = jnp.concatenate([c, c_ctx[None, :], jnp.zeros((8 - n_b - 1, d), c.dtype)], axis=0).astype(F32)
    mod = _modulation(cin, ada_w.astype(F32), ada_b.astype(F32))
    lower = _hg_lower(hg_lb)
    cos_t, sin_t = _rope_tables(n_lat, tc)
    fnw = final_norm_w.reshape(1, d).astype(F32)

    for l in range(depth):
        final = l == depth - 1
        wp, wab = _permute_w_in(w_in[l])
        p, pab = _project(xa, mod[l], norm_w[l].reshape(1, d).astype(F32), wp, wab, tc)
        dq, dk, dkt, dv, rq, rkt, gcol, grow, dktf, growf = _prep(
            p, pab, dn_conv[l].astype(F32), cos_t, sin_t, _pad_row(dn_a_log[l]), _pad_row(dn_dt_bias[l]), tc)
        lam_row = _pad_row(ret_decay[l])
        outs = []
        outs += _dn_scan(dq, dk, dktf, dkt, dv, gcol, growf, grow, tc)
        outs += _ret_scan(rq, rkt, p, lam_row, tc)
        for reverse in (False, True):
            outs.append(_hg_scan(p, lower[l, 1 if reverse else 0].reshape(1, -1), tc, reverse))
        nws = (dn_norm_w[l].reshape(1, -1).astype(F32), ret_norm_w[l].reshape(1, -1).astype(F32),
               hg_norm_w[l].reshape(1, -1).astype(F32))
        xa = _merge(outs, p, xa, mod[l], nws, w_branch[l].astype(BF16), w_out[l].astype(BF16), fnw, tc, final)
    return xa
```

```python
import functools
import math

import jax
import jax.numpy as jnp
import numpy as np
from jax import lax
from jax.experimental import pallas as pl
from jax.experimental.pallas import tpu as pltpu

F32 = jnp.float32
BF16 = jnp.bfloat16

D_MODEL = 1024
GRID_W = 64
DN_HEADS, DN_DK, DN_DV = 4, 128, 256
RET_HEADS, RET_DK, RET_DV = 4, 128, 256
HG_HEADS, HG_DK, HG_DV = 8, 128, 128
CONV_W = 5
ROPE_BASE = 10000.0
NORM_EPS = 1e-6
N_BRANCH = 3

LANES = 128
TOK_BLK = 256
DN_CHUNK = 64
HG_CHUNK = 64
NEG_BIG = -1e30

C_DNQKV = 0
C_DNGATE = 2048
C_RQ = 3072
C_RK = 3584
C_RV = 4096
C_RGATE = 5120
C_HQ = 6144
C_HF = 7168
C_HI = 9216
C_HGATE = 10240
C_MERGE0 = 11264
C_MERGE1 = 12288
C_MERGE2 = 13312
P_WIDTH = 14336
PROJ_TILE = 3584
PROJ_SUB = 512
_PROJ_ACT_SLICES = "NNNN" "SSNN" "NNSS" "SSGG" "GGNN" "SSGG" "GGGG"
PROJ_ACT = tuple(_PROJ_ACT_SLICES[i:i + PROJ_TILE // PROJ_SUB]
                 for i in range(0, len(_PROJ_ACT_SLICES), PROJ_TILE // PROJ_SUB))
NEG_LOG2E = -1.0 / math.log(2.0)
VMEM_LIMIT = 56 * 1024 * 1024


def _dot(a, b):
    return jnp.dot(a.astype(BF16), b.astype(BF16), preferred_element_type=F32)


def _dot_nt(a, b):
    return lax.dot_general(a.astype(BF16), b.astype(BF16), (((1,), (1,)), ((), ())),
                           preferred_element_type=F32)


def _dot_tn(a, b):
    return lax.dot_general(a.astype(BF16), b.astype(BF16), (((0,), (0,)), ((), ())),
                           preferred_element_type=F32)


def _sigmoid(x):
    return 0.5 * jnp.tanh(0.5 * x) + 0.5


def _silu(x):
    h = 0.5 * x
    return h + h * jnp.tanh(h)


def _softplus(x):
    return jnp.maximum(x, 0.0) + jnp.log(1.0 + jnp.exp(-jnp.abs(x)))


def _iota(shape, dim):
    return lax.broadcasted_iota(jnp.int32, shape, dim)


def _shr(x, s):
    return lax.shift_right_logical(x, int(s).bit_length() - 1)


def _cparams(sem):
    return pltpu.CompilerParams(dimension_semantics=sem, vmem_limit_bytes=VMEM_LIMIT)


def _mod_kernel(cin_ref, w_ref, b_ref, o_ref):
    s = _silu(cin_ref[...])
    o_ref[0] = _dot(s, w_ref[0]) + b_ref[0]


def _modulation(cin, ada_w, ada_b):
    depth, d, d3 = ada_w.shape
    tn = 1024
    return pl.pallas_call(
        _mod_kernel,
        grid=(depth, d3 // tn),
        in_specs=[pl.BlockSpec((8, d), lambda l, j: (0, 0)),
                  pl.BlockSpec((1, d, tn), lambda l, j: (l, 0, j)),
                  pl.BlockSpec((1, 1, tn), lambda l, j: (l, 0, j))],
        out_specs=pl.BlockSpec((1, 8, tn), lambda l, j: (l, 0, j)),
        out_shape=jax.ShapeDtypeStruct((depth, 8, d3), F32),
        compiler_params=_cparams(("arbitrary", "arbitrary")),
        name="ada_mod",
    )(cin, ada_w, ada_b.reshape(depth, 1, d3))


def _lower_kernel(lb_ref, o_ref, *, depth):
    xs = [lb_ref[l] for l in range(depth)]
    m = xs[0]
    for l in range(1, depth):
        m = jnp.maximum(m, xs[l])
    es = [jnp.exp(x - m) for x in xs]
    tot = es[0]
    for l in range(1, depth):
        tot = tot + es[l]
    sm = [e / tot for e in es]
    acc = sm[0]
    o_ref[0] = acc - sm[0]
    for l in range(1, depth):
        acc = acc + sm[l]
        o_ref[l] = acc - sm[0]


def _hg_lower(hg_lb):
    depth = hg_lb.shape[0]
    lb = hg_lb.astype(F32).reshape(depth, 2, HG_HEADS * HG_DK)
    return pl.pallas_call(
        functools.partial(_lower_kernel, depth=depth),
        out_shape=jax.ShapeDtypeStruct(lb.shape, F32),
        name="hg_lower",
    )(lb)


def _proj_kernel(x_ref, mod_ref, nw_ref, w_ref, wab_ref, p_ref, pab_ref, hb_ref, *, tc, tm, tn, n_b):
    b = pl.program_id(0)
    i = pl.program_id(1)
    j = pl.program_id(2)
    d = D_MODEL

    @pl.when(j == 0)
    def _():
        x = x_ref[0]
        ms = jnp.mean(x * x, axis=-1, keepdims=True)
        y = x * lax.rsqrt(ms + NORM_EPS) * nw_ref[...]
        row = i * tm + _iota((tm, 1), 0)
        is_ctx = row < tc
        sh = jnp.where(is_ctx, mod_ref[n_b:n_b + 1, 0:d], mod_ref[pl.ds(b, 1), 0:d])
        sc = jnp.where(is_ctx, mod_ref[n_b:n_b + 1, d:2 * d], mod_ref[pl.ds(b, 1), d:2 * d])
        hb = (y * (1.0 + sc) + sh).astype(BF16)
        hb_ref[...] = hb
        pab_ref[0] = jnp.dot(hb, wab_ref[...], preferred_element_type=F32)

    sub = PROJ_SUB
    for pattern in sorted(set(PROJ_ACT)):
        tiles = [t for t, pat in enumerate(PROJ_ACT) if pat == pattern]
        cond = j == tiles[0]
        for t in tiles[1:]:
            cond = jnp.logical_or(cond, j == t)

        @pl.when(cond)
        def _(pattern=pattern):
            for s in range(tn // sub):
                r = jnp.dot(hb_ref[...], w_ref[:, s * sub:(s + 1) * sub], preferred_element_type=F32)
                if pattern[s] == "S":
                    r = _silu(r)
                elif pattern[s] == "G":
                    r = _sigmoid(r)
                p_ref[0, :, s * sub:(s + 1) * sub] = r.astype(BF16)


def _row_tile(ta):
    best = 16
    for t in range(16, 1101, 16):
        if ta % t == 0:
            best = t
    return best


def _project(xa, mod_l, nw, wp, wab, tc):
    n_b, ta, d = xa.shape
    tm = _row_tile(ta)
    tn = PROJ_TILE
    assert len(PROJ_ACT) * tn == P_WIDTH and all(len(pat) * PROJ_SUB == tn for pat in PROJ_ACT)
    return pl.pallas_call(
        functools.partial(_proj_kernel, tc=tc, tm=tm, tn=tn, n_b=n_b),
        grid=(n_b, ta // tm, P_WIDTH // tn),
        in_specs=[pl.BlockSpec((1, tm, d), lambda b, i, j: (b, i, 0)),
                  pl.BlockSpec((8, 3 * d), lambda b, i, j: (0, 0)),
                  pl.BlockSpec((1, d), lambda b, i, j: (0, 0)),
                  pl.BlockSpec((d, tn), lambda b, i, j: (0, j)),
                  pl.BlockSpec((d, LANES), lambda b, i, j: (0, 0))],
        out_specs=[pl.BlockSpec((1, tm, tn), lambda b, i, j: (b, i, j)),
                   pl.BlockSpec((1, tm, LANES), lambda b, i, j: (b, i, 0))],
        out_shape=[jax.ShapeDtypeStruct((n_b, ta, P_WIDTH), BF16),
                   jax.ShapeDtypeStruct((n_b, ta, LANES), F32)],
        scratch_shapes=[pltpu.VMEM((tm, d), BF16)],
        compiler_params=_cparams(("arbitrary", "arbitrary", "arbitrary")),
        name="in_proj",
    )(xa, mod_l, nw, wp, wab)


def _prep_kernel(main_ref, prev_ref, next_ref, rqk_ref, pab_ref, cw_ref, cos_ref, sin_ref, alog_ref, dtb_ref,
                 dq_ref, dk_ref, dkt_ref, dv_ref, rq_ref, rkt_ref, gcol_ref, grow_ref, dktf_ref, growf_ref,
                 xs_ref, *, tc, n_blk):
    i = pl.program_id(1)
    tb = TOK_BLK
    nctx = tc // tb
    has_prev = jnp.logical_and(i != 0, i != nctx)
    has_next = jnp.logical_and(i != nctx - 1, i != n_blk - 1)
    pm = jnp.where(has_prev, 1.0, 0.0)
    nm = jnp.where(has_next, 1.0, 0.0)
    n_ch = tb // DN_CHUNK

    for g in range(16):
        ls = slice(g * LANES, (g + 1) * LANES)
        xs_ref[8:8 + tb, :] = main_ref[0, :, ls].astype(F32)
        xs_ref[0:8, :] = prev_ref[0, 8:16, ls].astype(F32) * pm
        xs_ref[8 + tb:16 + tb, :] = next_ref[0, 0:8, ls].astype(F32) * nm
        acc = cw_ref[0:1, ls] * xs_ref[6:6 + tb, :]
        for t in range(1, CONV_W):
            acc = acc + cw_ref[t:t + 1, ls] * xs_ref[6 + t:6 + t + tb, :]
        y = _silu(acc)
        if g < 8:
            y = y * lax.rsqrt(jnp.sum(y * y, axis=-1, keepdims=True) + NORM_EPS)
        if g < 4:
            dq_ref[0, :, ls] = (y * (DN_DK ** -0.5)).astype(BF16)
        elif g < 8:
            h = g - 4
            hs = slice(h * LANES, (h + 1) * LANES)
            dk_ref[0, :, hs] = y.astype(BF16)
            dktf_ref[0, h] = y.T.astype(BF16)
            for c in range(n_ch):
                dkt_ref[0, h, c] = y[c * DN_CHUNK:(c + 1) * DN_CHUNK, :].T.astype(BF16)
        else:
            vs = slice((g - 8) * LANES, (g - 7) * LANES)
            dv_ref[0, :, vs] = y.astype(BF16)

    pab = pab_ref[0]
    g_log = -jnp.exp(alog_ref[...]) * _softplus(pab + dtb_ref[...])
    beta = _sigmoid(pab)
    r = _iota((tb, tb), 0)
    c_ = _iota((tb, tb), 1)
    same = _shr(r, DN_CHUNK) == _shr(c_, DN_CHUNK)
    m_f = jnp.where(jnp.logical_and(same, r >= c_), 1.0, 0.0)
    m_b = jnp.where(jnp.logical_and(same, r <= c_), 1.0, 0.0)
    cs_f = jnp.dot(m_f, g_log, precision=lax.Precision.HIGHEST, preferred_element_type=F32)
    cs_b = jnp.dot(m_b, g_log, precision=lax.Precision.HIGHEST, preferred_element_type=F32)
    lane = _iota((tb, LANES), 1)
    gcol = jnp.where(lane < 4, cs_f, jnp.where(lane < 8, cs_b, beta))
    gcol_ref[0] = gcol
    growf_ref[0] = gcol.T[0:8, :]
    for c in range(n_ch):
        grow_ref[0, c] = gcol[c * DN_CHUNK:(c + 1) * DN_CHUNK, :].T[0:8, :]

    cosv = cos_ref[...]
    sinv = sin_ref[...]
    even_lane = (_iota((tb, LANES), 1) & 1) == 0
    for g in range(8):
        ls = slice(g * LANES, (g + 1) * LANES)
        x = rqk_ref[0, :, ls].astype(F32)
        partner = jnp.where(even_lane, pltpu.roll(x, LANES - 1, 1), pltpu.roll(x, 1, 1))
        y = x * cosv + partner * sinv
        if g < 4:
            rq_ref[0, :, ls] = y.astype(BF16)
        else:
            rkt_ref[0, g - 4] = (y * (RET_DK ** -0.5)).T.astype(BF16)


def _prep(p, pab, conv_w, cos_t, sin_t, alog_row, dtb_row, tc):
    n_b, ta, _ = p.shape
    tb = TOK_BLK
    n_blk = ta // tb
    n_ch = tb // DN_CHUNK
    hpb = tb // 16
    n16 = ta // 16
    out_shape = [
        jax.ShapeDtypeStruct((n_b, ta, 512), BF16),
        jax.ShapeDtypeStruct((n_b, ta, 512), BF16),
        jax.ShapeDtypeStruct((n_b, DN_HEADS, ta // DN_CHUNK, DN_DK, DN_CHUNK), BF16),
        jax.ShapeDtypeStruct((n_b, ta, 1024), BF16),
        jax.ShapeDtypeStruct((n_b, ta, 512), BF16),
        jax.ShapeDtypeStruct((n_b, RET_HEADS, RET_DK, ta), BF16),
        jax.ShapeDtypeStruct((n_b, ta, LANES), F32),
        jax.ShapeDtypeStruct((n_b, ta // DN_CHUNK, 8, DN_CHUNK), F32),
        jax.ShapeDtypeStruct((n_b, DN_HEADS, DN_DK, ta), BF16),
        jax.ShapeDtypeStruct((n_b, 8, ta), F32),
    ]
    out_specs = [
        pl.BlockSpec((1, tb, 512), lambda b, i: (b, i, 0)),
        pl.BlockSpec((1, tb, 512), lambda b, i: (b, i, 0)),
        pl.BlockSpec((1, DN_HEADS, n_ch, DN_DK, DN_CHUNK), lambda b, i: (b, 0, i, 0, 0)),
        pl.BlockSpec((1, tb, 1024), lambda b, i: (b, i, 0)),
        pl.BlockSpec((1, tb, 512), lambda b, i: (b, i, 0)),
        pl.BlockSpec((1, RET_HEADS, RET_DK, tb), lambda b, i: (b, 0, 0, i)),
        pl.BlockSpec((1, tb, LANES), lambda b, i: (b, i, 0)),
        pl.BlockSpec((1, n_ch, 8, DN_CHUNK), lambda b, i: (b, i, 0, 0)),
        pl.BlockSpec((1, DN_HEADS, DN_DK, tb), lambda b, i: (b, 0, 0, i)),
        pl.BlockSpec((1, 8, tb), lambda b, i: (b, 0, i)),
    ]
    in_specs = [
        pl.BlockSpec((1, tb, 2048), lambda b, i: (b, i, 0)),
        pl.BlockSpec((1, 16, 2048), lambda b, i: (b, jnp.maximum(i * hpb - 1, 0), 0)),
        pl.BlockSpec((1, 16, 2048), lambda b, i: (b, jnp.minimum((i + 1) * hpb, n16 - 1), 0)),
        pl.BlockSpec((1, tb, 1024), lambda b, i: (b, i, C_RQ // 1024)),
        pl.BlockSpec((1, tb, LANES), lambda b, i: (b, i, 0)),
        pl.BlockSpec((CONV_W, 2048), lambda b, i: (0, 0)),
        pl.BlockSpec((tb, LANES), lambda b, i: (i, 0)),
        pl.BlockSpec((tb, LANES), lambda b, i: (i, 0)),
        pl.BlockSpec((1, LANES), lambda b, i: (0, 0)),
        pl.BlockSpec((1, LANES), lambda b, i: (0, 0)),
    ]
    return pl.pallas_call(
        functools.partial(_prep_kernel, tc=tc, n_blk=n_blk),
        grid=(n_b, n_blk),
        in_specs=in_specs,
        out_specs=out_specs,
        out_shape=out_shape,
        scratch_shapes=[pltpu.VMEM((tb + 16, LANES), F32)],
        compiler_params=_cparams(("arbitrary", "arbitrary")),
        name="prep",
    )(p, p, p, p, pab, conv_w, cos_t, sin_t, alog_row, dtb_row)


def _blk_index(n, n_blk, nctx, reverse):
    if not reverse:
        return n
    return jnp.where(n < nctx, nctx - 1 - n, n_blk - 1 - (n - nctx))


DN_MASK_NEG, DN_MASK_STRICT, DN_MASK_B8, DN_MASK_EYE, DN_MASK_OFF0 = 0, 1, 2, 3, 4
DN_MERGE_SIZES = (8, 16, 32)


DN_PAIR = 2 * DN_CHUNK


def _dn_masks(reverse):
    r = np.arange(DN_PAIR)[:, None]
    c = np.arange(DN_PAIR)[None, :]
    same = (r // DN_CHUNK) == (c // DN_CHUNK)
    incl = same & ((r <= c) if reverse else (r >= c))
    strict = incl & (r != c)
    ms = [np.where(incl, 0.0, NEG_BIG), strict, (r // 8) == (c // 8), r == c]
    for s in DN_MERGE_SIZES:
        ms.append(((r // (2 * s)) == (c // (2 * s))) & ((r // s) != (c // s)))
    return jnp.asarray(np.stack([np.asarray(m, np.float32) for m in ms]))


def _dn_streams(n, q_ref, k_ref, ktf_ref, ktc_ref, v_ref, gcol_ref, growf_ref, growc_ref, m_ref, o_ref, s_ref,
                wq_s, u_s, attn_s, kdt_s, egl_s, reverse):
    c_len = DN_CHUNK
    n_ch = TOK_BLK // c_len
    d_off = 4 if reverse else 0
    heads = range(DN_HEADS)
    last = 0 if reverse else c_len - 1
    pairs = [slice(g * DN_PAIR, (g + 1) * DN_PAIR) for g in range(TOK_BLK // DN_PAIR)]
    units = [(h, g) for h in heads for g in range(len(pairs))]

    wr = lax.rem(n, 2)
    rd = 1 - wr

    @pl.when(n == 0)
    def _():
        s_ref[...] = jnp.zeros_like(s_ref)
        wq_s[...] = jnp.zeros_like(wq_s)
        u_s[...] = jnp.zeros_like(u_s)
        attn_s[...] = jnp.zeros_like(attn_s)
        kdt_s[...] = jnp.zeros_like(kdt_s)
        egl_s[...] = jnp.zeros_like(egl_s)

    us = range(len(units))
    pa = {}

    def a_load():
        gcol = gcol_ref[0]
        growf = growf_ref[0]
        pa["gc"] = [gcol[:, d_off + h:d_off + h + 1] for h in heads]
        pa["bc"] = [gcol[:, 8 + d_off + h:9 + d_off + h] for h in heads]
        pa["egc"] = [jnp.exp(x) for x in pa["gc"]]
        pa["kb"] = [k_ref[0, :, h * DN_DK:(h + 1) * DN_DK].astype(F32) * pa["bc"][h] for h in heads]
        dec = [jnp.exp((pa["gc"][h][pairs[g]] - growf[d_off + h:d_off + h + 1, pairs[g]]) + m_ref[DN_MASK_NEG])
               for h, g in units]
        pa["a"] = [_dot(pa["kb"][h][pairs[g]], ktf_ref[0, h, :, pairs[g]]) * dec[u] * m_ref[DN_MASK_STRICT]
                   for u, (h, g) in enumerate(units)]

    def a_sq():
        pa["d"] = [(pa["a"][u] * m_ref[DN_MASK_B8]).astype(BF16) for u in us]
        pa["d2"] = [jnp.dot(pa["d"][u], pa["d"][u], preferred_element_type=F32) for u in us]

    def a_pow():
        d2b = [x.astype(BF16) for x in pa["d2"]]
        pa["d4"] = [jnp.dot(d2b[u], d2b[u], preferred_element_type=F32) for u in us]
        pa["d3"] = [jnp.dot(pa["d"][u], d2b[u], preferred_element_type=F32) for u in us]

    def a_base():
        p1 = [m_ref[DN_MASK_EYE] - pa["d"][u].astype(F32) + pa["d2"][u] - pa["d3"][u] for u in us]
        pa["t"] = [p1[u] + _dot(p1[u], pa["d4"][u]) for u in us]

    def a_merge_x(lvl):
        def run():
            pa["tb"] = [x.astype(BF16) for x in pa["t"]]
            pa["x"] = [_dot(pa["a"][u] * m_ref[DN_MASK_OFF0 + lvl], pa["tb"][u]) for u in us]
        return run

    def a_merge_t():
        pa["t"] = [pa["t"][u] - _dot(pa["tb"][u], pa["x"][u]) for u in us]

    def a_store():
        neg64 = m_ref[DN_MASK_NEG, 0:c_len, 0:c_len]
        for u, (h, g) in enumerate(units):
            rows = pairs[g]
            rhs = jnp.concatenate([pa["kb"][h][rows] * pa["egc"][h][rows],
                                   v_ref[0, rows, h * DN_DV:(h + 1) * DN_DV].astype(F32) * pa["bc"][h][rows]], axis=1)
            wu = _dot(pa["t"][u], rhs)
            qd = q_ref[0, rows, h * DN_DK:(h + 1) * DN_DK].astype(F32) * pa["egc"][h][rows]
            u_s[wr, h, rows, :] = wu[:, DN_DK:DN_DK + DN_DV]
            for ci in range(DN_PAIR // c_len):
                c = g * (DN_PAIR // c_len) + ci
                cs = slice(ci * c_len, (ci + 1) * c_len)
                wq_s[wr, h, c] = jnp.concatenate([wu[cs, 0:DN_DK], qd[cs]], axis=0).astype(BF16)
        for c in range(n_ch):
            rs = slice(c * c_len, (c + 1) * c_len)
            growc = growc_ref[0, c]
            for h in heads:
                gr = growc[d_off + h:d_off + h + 1, :]
                gl = gr[:, last:last + 1]
                kt = ktc_ref[0, h, c]
                attn = _dot(q_ref[0, rs, h * DN_DK:(h + 1) * DN_DK], kt) * jnp.exp((pa["gc"][h][rs] - gr) + neg64)
                attn_s[wr, h, c] = attn.astype(BF16)
                kdt_s[wr, h, c] = (kt.astype(F32) * jnp.exp(gl - gr)).astype(BF16)
                egl_s[wr, h, c] = jnp.broadcast_to(jnp.exp(gl), (8, LANES))

    pb = {"state": [s_ref[h] for h in heads]}

    def b_first(c):
        def run():
            pb["ws"] = [jnp.dot(wq_s[rd, h, c], pb["state"][h].astype(BF16), preferred_element_type=F32)
                        for h in heads]
        return run

    def b_second(c):
        def run():
            rs = slice(c * c_len, (c + 1) * c_len)
            vn = [(u_s[rd, h, rs, :] - pb["ws"][h][0:c_len]).astype(BF16) for h in heads]
            o = [pb["ws"][h][c_len:2 * c_len] + jnp.dot(attn_s[rd, h, c], vn[h], preferred_element_type=F32)
                 for h in heads]
            pb["state"] = [pb["state"][h] * egl_s[rd, h, c][0:1, 0:1]
                           + jnp.dot(kdt_s[rd, h, c], vn[h], preferred_element_type=F32) for h in heads]
            for h in heads:
                o_ref[0, rs, h * DN_DV:(h + 1) * DN_DV] = o[h].astype(BF16)
        return run

    a_stages = [a_load, a_sq, a_pow, a_base]
    for lvl in range(len(DN_MERGE_SIZES)):
        a_stages += [a_merge_x(lvl), a_merge_t]
    a_stages.append(a_store)
    b_stages = []
    for cc in range(n_ch):
        c = (n_ch - 1 - cc) if reverse else cc
        b_stages += [b_first(c), b_second(c)]

    def finish():
        for h in heads:
            s_ref[h] = pb["state"][h]

    return a_stages, b_stages, finish


DN_N_IN = 9
DN_N_SCRATCH = 6


def _dn_kernel(*refs):
    n = pl.program_id(1)
    ins_f, ins_b = refs[0:DN_N_IN], refs[DN_N_IN:2 * DN_N_IN]
    o_f, o_b = refs[2 * DN_N_IN], refs[2 * DN_N_IN + 1]
    scr = refs[2 * DN_N_IN + 2:]
    streams = [_dn_streams(n, *ins_f, o_f, *scr[0:DN_N_SCRATCH], reverse=False),
               _dn_streams(n, *ins_b, o_b, *scr[DN_N_SCRATCH:], reverse=True)]
    n_stage = max(max(len(a), len(b)) for a, b, _ in streams)
    for i in range(n_stage):
        for a_stages, _, _ in streams:
            if i < len(a_stages):
                a_stages[i]()
        for _, b_stages, _ in streams:
            if i < len(b_stages):
                b_stages[i]()
    for _, _, finish in streams:
        finish()


def _dn_scan(dq, dk, dktf, dktc, dv, gcol, growf, growc, tc):
    n_b, ta, _ = dq.shape
    tb = TOK_BLK
    n_blk = ta // tb
    nctx = tc // tb
    n_ch = tb // DN_CHUNK
    in_specs, args, out_specs, scratch = [], [], [], []
    for reverse in (False, True):
        masks = _dn_masks(reverse)
        blk = functools.partial(_blk_index, n_blk=n_blk, nctx=nctx, reverse=reverse)
        bi = lambda n, blk=blk: blk(jnp.minimum(n, n_blk - 1))
        bo = lambda n, blk=blk: blk(jnp.maximum(n - 1, 0))
        in_specs += [pl.BlockSpec((1, tb, 512), lambda b, n, bi=bi: (b, bi(n), 0)),
                     pl.BlockSpec((1, tb, 512), lambda b, n, bi=bi: (b, bi(n), 0)),
                     pl.BlockSpec((1, DN_HEADS, DN_DK, tb), lambda b, n, bi=bi: (b, 0, 0, bi(n))),
                     pl.BlockSpec((1, DN_HEADS, n_ch, DN_DK, DN_CHUNK), lambda b, n, bi=bi: (b, 0, bi(n), 0, 0)),
                     pl.BlockSpec((1, tb, 1024), lambda b, n, bi=bi: (b, bi(n), 0)),
                     pl.BlockSpec((1, tb, LANES), lambda b, n, bi=bi: (b, bi(n), 0)),
                     pl.BlockSpec((1, 8, tb), lambda b, n, bi=bi: (b, 0, bi(n))),
                     pl.BlockSpec((1, n_ch, 8, DN_CHUNK), lambda b, n, bi=bi: (b, bi(n), 0, 0)),
                     pl.BlockSpec(masks.shape, lambda b, n: (0, 0, 0))]
        args += [dq, dk, dktf, dktc, dv, gcol, growf, growc, masks]
        out_specs.append(pl.BlockSpec((1, tb, 1024), lambda b, n, bo=bo: (b, bo(n), 0)))
        scratch += [pltpu.VMEM((DN_HEADS, DN_DK, DN_DV), F32),
                    pltpu.VMEM((2, DN_HEADS, n_ch, 2 * DN_CHUNK, DN_DK), BF16),
                    pltpu.VMEM((2, DN_HEADS, tb, DN_DV), F32),
                    pltpu.VMEM((2, DN_HEADS, n_ch, DN_CHUNK, DN_CHUNK), BF16),
                    pltpu.VMEM((2, DN_HEADS, n_ch, DN_DK, DN_CHUNK), BF16),
                    pltpu.VMEM((2, DN_HEADS, n_ch, 8, LANES), F32)]
    assert len(args) == 2 * DN_N_IN and len(scratch) == 2 * DN_N_SCRATCH
    return pl.pallas_call(
        _dn_kernel,
        grid=(n_b, n_blk + 1),
        in_specs=in_specs,
        out_specs=out_specs,
        out_shape=[jax.ShapeDtypeStruct((n_b, ta, 1024), BF16)] * 2,
        scratch_shapes=scratch,
        compiler_params=_cparams(("arbitrary", "arbitrary")),
        name="dn_scan",
    )(*args)


def _ret_kernel(qf_ref, ktf_ref, vf_ref, qb_ref, ktb_ref, vb_ref, lam_ref, of_ref, ob_ref,
                s_ref, dec_ref, eq_ref, ek_ref, egl_ref):
    n = pl.program_id(1)
    c_len = TOK_BLK
    dirs = ((0, qf_ref, ktf_ref, vf_ref, of_ref), (1, qb_ref, ktb_ref, vb_ref, ob_ref))
    streams = [(d, h) for d in range(2) for h in range(RET_HEADS)]

    @pl.when(n == 0)
    def _():
        s_ref[...] = jnp.zeros_like(s_ref)
        ri = _iota((c_len, c_len), 0)
        ci = _iota((c_len, c_len), 1)
        rowpos = _iota((c_len, LANES), 0)
        colpos = _iota((RET_DK, c_len), 1)
        for d, h in streams:
            reverse = d == 1
            dist = (ci - ri) if reverse else (ri - ci)
            if reverse:
                qexp = (c_len - rowpos).astype(F32)
                kexp = colpos.astype(F32)
            else:
                qexp = (rowpos + 1).astype(F32)
                kexp = (c_len - 1 - colpos).astype(F32)
            x = lam_ref[0:1, 4 * d + h:4 * d + h + 1]
            lam = jnp.minimum(x, 0.0) - jnp.log(1.0 + jnp.exp(-jnp.abs(x)))
            dec_ref[d, h] = jnp.exp(jnp.where(dist >= 0, dist.astype(F32) * lam, NEG_BIG))
            eq_ref[d, h] = jnp.exp(qexp * lam)
            ek_ref[d, h] = jnp.exp(kexp * lam)
            egl_ref[d, h] = jnp.broadcast_to(jnp.exp(lam * float(c_len)), (8, LANES))

    def q_of(d, h):
        return dirs[d][1][0, :, h * RET_DK:(h + 1) * RET_DK]

    def v_of(d, h):
        return dirs[d][3][0, :, h * RET_DV:(h + 1) * RET_DV]

    s_old = {u: s_ref[u[0], u[1]] for u in streams}
    qk = {u: _dot(q_of(*u), dirs[u[0]][2][0, u[1]]) for u in streams}
    kv = {u: _dot(dirs[u[0]][2][0, u[1]].astype(F32) * ek_ref[u[0], u[1]], v_of(*u)) for u in streams}
    qs = {u: _dot(q_of(*u).astype(F32) * eq_ref[u[0], u[1]], s_old[u]) for u in streams}
    av = {u: _dot(qk[u] * dec_ref[u[0], u[1]], v_of(*u)) for u in streams}
    for u in streams:
        d, h = u
        s_ref[d, h] = s_old[u] * egl_ref[d, h][0:1, 0:1] + kv[u]
        dirs[d][4][0, :, h * RET_DV:(h + 1) * RET_DV] = (av[u] + qs[u]).astype(BF16)


def _ret_scan(rq, rkt, p, lam_row, tc):
    n_b, ta, _ = rq.shape
    tb = TOK_BLK
    n_blk = ta // tb
    nctx = tc // tb
    in_specs, out_specs = [], []
    for reverse in (False, True):
        bi = functools.partial(_blk_index, n_blk=n_blk, nctx=nctx, reverse=reverse)
        in_specs += [pl.BlockSpec((1, tb, 512), lambda b, n, bi=bi: (b, bi(n), 0)),
                     pl.BlockSpec((1, RET_HEADS, RET_DK, tb), lambda b, n, bi=bi: (b, 0, 0, bi(n))),
                     pl.BlockSpec((1, tb, 1024), lambda b, n, bi=bi: (b, bi(n), C_RV // 1024))]
        out_specs.append(pl.BlockSpec((1, tb, 1024), lambda b, n, bi=bi: (b, bi(n), 0)))
    in_specs.append(pl.BlockSpec((1, LANES), lambda b, n: (0, 0)))
    return pl.pallas_call(
        _ret_kernel,
        grid=(n_b, n_blk),
        in_specs=in_specs,
        out_specs=out_specs,
        out_shape=[jax.ShapeDtypeStruct((n_b, ta, 1024), BF16)] * 2,
        scratch_shapes=[pltpu.VMEM((2, RET_HEADS, RET_DK, RET_DV), F32),
                        pltpu.VMEM((2, RET_HEADS, tb, tb), F32),
                        pltpu.VMEM((2, RET_HEADS, tb, LANES), F32),
                        pltpu.VMEM((2, RET_HEADS, RET_DK, tb), F32),
                        pltpu.VMEM((2, RET_HEADS, 8, LANES), F32)],
        compiler_params=_cparams(("arbitrary", "arbitrary")),
        name="ret_scan",
    )(rq, rkt, p, rq, rkt, p, lam_row)


HG_LEVELS = (1, 2, 4, 8, 16, 32)


def _neg_abs(x):
    bits = lax.bitcast_convert_type(x, jnp.uint32) | jnp.uint32(0x80000000)
    return lax.bitcast_convert_type(bits, F32)


def _hg_pair_masks(reverse):
    i = np.arange(HG_CHUNK)[:, None]
    j = np.arange(HG_CHUNK)[None, :]
    ms = []
    for s in HG_LEVELS:
        q_half = 0 if reverse else 1
        ms.append(((i // (2 * s)) == (j // (2 * s))) & (((i // s) & 1) == q_half) & (((j // s) & 1) == 1 - q_half))
    ms.append(i == j)
    return jnp.asarray(np.stack(ms).astype(np.float32))


def _hg_boundary(gc_s, gc, base, hs, s, row8, reverse):
    c_len = HG_CHUNK
    off = s - 1 if reverse else s
    if s == 1:
        rowi = _iota((c_len, HG_DK), 0)
        if reverse:
            return jnp.where((rowi & 1) == 1, pltpu.roll(gc, 1, 0), gc)
        return jnp.where((rowi & 1) == 0, pltpu.roll(gc, c_len - 1, 0), gc)
    parts = []
    for vi in range(c_len // 8):
        r0 = base + 8 * vi
        if s == 2:
            lo = jnp.broadcast_to(gc_s[r0 + off:r0 + off + 1, hs], (8, HG_DK))
            hi = jnp.broadcast_to(gc_s[r0 + 4 + off:r0 + 5 + off, hs], (8, HG_DK))
            parts.append(jnp.where(row8 < 4, lo, hi))
        else:
            m = base + ((8 * vi) // (2 * s)) * (2 * s) + off
            parts.append(jnp.broadcast_to(gc_s[m:m + 1, hs], (8, HG_DK)))
    return jnp.concatenate(parts, axis=0)


HG_N_IN = 5
HG_N_SCRATCH = 3


def _hg_kernel(*refs):
    n = pl.program_id(1)
    o_f, o_b = refs[2 * HG_N_IN], refs[2 * HG_N_IN + 1]
    scr = refs[2 * HG_N_IN + 2:]
    _hg_direction(n, *refs[0:HG_N_IN], o_f, *scr[0:HG_N_SCRATCH], reverse=False)
    _hg_direction(n, *refs[HG_N_IN:2 * HG_N_IN], o_b, *scr[HG_N_SCRATCH:], reverse=True)


def _hg_direction(n, xq_ref, xf_ref, xi_ref, low_ref, pm_ref, o_ref, st_ref, k_s, gc_s, *, reverse):
    tb = TOK_BLK
    c_len = HG_CHUNK
    n_ch = tb // c_len

    @pl.when(n == 0)
    def _():
        st_ref[...] = jnp.zeros_like(st_ref)

    low = low_ref[...]
    f = low + (1.0 - low) * xf_ref[0].astype(F32)
    g = jnp.log2(f)
    r = _iota((tb, tb), 0)
    c_ = _iota((tb, tb), 1)
    same = _shr(r, c_len) == _shr(c_, c_len)
    tri = (r <= c_) if reverse else (r >= c_)
    m = jnp.where(jnp.logical_and(same, tri), 1.0, 0.0).astype(BF16)
    g0 = g.astype(BF16)
    r1 = g - g0.astype(F32)
    g1 = r1.astype(BF16)
    g2 = (r1 - g1.astype(F32)).astype(BF16)
    gc_s[...] = (jnp.dot(m, g0, preferred_element_type=F32) + jnp.dot(m, g1, preferred_element_type=F32)
                 + jnp.dot(m, g2, preferred_element_type=F32))
    k_s[...] = (1.0 - f).astype(BF16)

    row8 = _iota((8, HG_DK), 0)
    last = 0 if reverse else c_len - 1
    hsl = [slice(h * HG_DK, (h + 1) * HG_DK) for h in range(HG_HEADS)]
    heads = range(HG_HEADS)

    def scores(c):
        base = c * c_len
        rs = slice(base, base + c_len)
        q = {h: xq_ref[0, rs, hsl[h]].astype(F32) for h in heads}
        k = {h: k_s[rs, hsl[h]].astype(F32) for h in heads}
        gc = {h: gc_s[rs, hsl[h]] for h in heads}
        tiles = range(c_len // 8)
        t8 = lambda x, v: x[8 * v:8 * v + 8]
        diag = pm_ref[len(HG_LEVELS)]
        a = {h: [t8(x, v) for v in tiles] for h, x in ((h, _dot_nt(q[h], k[h]) * diag) for h in heads)}
        q_half = 0 if reverse else 1
        for lvl, s in enumerate(HG_LEVELS):
            pm = pm_ref[lvl]
            if s < 8:
                is_q = (_shr(row8, s) & 1) == q_half
                pick = lambda h, v: jnp.where(is_q, t8(q[h], v), t8(k[h], v))
                q_tiles = list(tiles)
            else:
                pick = lambda h, v: t8(q[h], v) if ((8 * v) // s) & 1 == q_half else t8(k[h], v)
                q_tiles = [v for v in tiles if ((8 * v) // s) & 1 == q_half]
            z = {}
            for h in heads:
                e = jnp.exp2(_neg_abs(gc[h] - _hg_boundary(gc_s, gc[h], base, hsl[h], s, row8, reverse)))
                z[h] = jnp.concatenate([pick(h, v) * t8(e, v) for v in tiles], axis=0)
            pr = {h: _dot_nt(z[h], z[h]) for h in heads}
            for h in heads:
                for v in q_tiles:
                    a[h][v] = a[h][v] + t8(pr[h], v) * t8(pm, v)
        return {h: jnp.concatenate(a[h], axis=0) for h in heads}

    def outputs(c, a):
        base = c * c_len
        rs = slice(base, base + c_len)
        q = {h: xq_ref[0, rs, hsl[h]].astype(F32) for h in heads}
        k = {h: k_s[rs, hsl[h]].astype(F32) for h in heads}
        gc = {h: gc_s[rs, hsl[h]] for h in heads}
        gl = {h: gc[h][last:last + 1, :] for h in heads}
        st = {h: st_ref[h] for h in heads}
        o = {h: _dot(a[h], xi_ref[0, rs, hsl[h]]) + _dot_nt(q[h] * jnp.exp2(gc[h]), st[h]) for h in heads}
        for h in heads:
            kd = k[h] * jnp.exp2(gl[h] - gc[h])
            st_ref[h] = st[h] * jnp.exp2(gl[h]) + _dot_tn(xi_ref[0, rs, hsl[h]], kd)
            o_ref[0, rs, hsl[h]] = o[h].astype(BF16)

    order = [(n_ch - 1 - cc) if reverse else cc for cc in range(n_ch)]
    pending = None
    for c in order:
        a = scores(c)
        if pending is not None:
            outputs(*pending)
        pending = (c, a)
    outputs(*pending)


def _hg_scan(p, low_rows, tc):
    n_b, ta, _ = p.shape
    tb = TOK_BLK
    n_blk = ta // tb
    nctx = tc // tb
    in_specs, args, out_specs, scratch = [], [], [], []
    for d_i, reverse in enumerate((False, True)):
        masks = _hg_pair_masks(reverse)
        bi = functools.partial(_blk_index, n_blk=n_blk, nctx=nctx, reverse=reverse)
        in_specs += [pl.BlockSpec((1, tb, 1024), lambda b, n, bi=bi: (b, bi(n), C_HQ // 1024)),
                     pl.BlockSpec((1, tb, 1024), lambda b, n, bi=bi, d_i=d_i: (b, bi(n), C_HF // 1024 + d_i)),
                     pl.BlockSpec((1, tb, 1024), lambda b, n, bi=bi: (b, bi(n), C_HI // 1024)),
                     pl.BlockSpec((1, 1024), lambda b, n: (0, 0)),
                     pl.BlockSpec(masks.shape, lambda b, n: (0, 0, 0))]
        args += [p, p, p, low_rows[d_i], masks]
        out_specs.append(pl.BlockSpec((1, tb, 1024), lambda b, n, bi=bi: (b, bi(n), 0)))
        scratch += [pltpu.VMEM((HG_HEADS, HG_DV, HG_DK), F32),
                    pltpu.VMEM((tb, 1024), BF16),
                    pltpu.VMEM((tb, 1024), F32)]
    assert len(args) == 2 * HG_N_IN and len(scratch) == 2 * HG_N_SCRATCH
    return pl.pallas_call(
        _hg_kernel,
        grid=(n_b, n_blk),
        in_specs=in_specs,
        out_specs=out_specs,
        out_shape=[jax.ShapeDtypeStruct((n_b, ta, 1024), BF16)] * 2,
        scratch_shapes=scratch,
        compiler_params=_cparams(("arbitrary", "arbitrary")),
        name="hg_scan",
    )(*args)


def _branch(of_ref, ob_ref, gate_ref, nw_ref, dv):
    o = of_ref[0].astype(F32) + ob_ref[0].astype(F32)
    gate = gate_ref[0].astype(F32)
    nw = nw_ref[...]
    parts = []
    for h in range(1024 // dv):
        x = o[:, h * dv:(h + 1) * dv]
        y = x * lax.rsqrt(jnp.mean(x * x, axis=-1, keepdims=True) + NORM_EPS) * nw
        parts.append(y)
    return (jnp.concatenate(parts, axis=-1) * gate).astype(BF16)


def _merge_kernel(dnf, dnb, rtf, rtb, hgf, hgb, g_dn, g_rt, g_hg, m0, m1, m2, x_ref, mod_ref,
                  nw_dn, nw_rt, nw_hg, wb_ref, wo_ref, fnw_ref, o_ref, *, tc, blk_off, n_b, final):
    b = pl.program_id(0)
    i = pl.program_id(1) + blk_off
    d = D_MODEL
    brs = (_branch(dnf, dnb, g_dn, nw_dn, DN_DV),
           _branch(rtf, rtb, g_rt, nw_rt, RET_DV),
           _branch(hgf, hgb, g_hg, nw_hg, HG_DV))
    y = None
    for idx, (br, mg) in enumerate(zip(brs, (m0, m1, m2))):
        t = jnp.dot(br, wb_ref[idx], preferred_element_type=F32) * mg[0].astype(F32)
        y = t if y is None else y + t
    out = jnp.dot(y.astype(BF16), wo_ref[...], preferred_element_type=F32)
    is_ctx = (i * TOK_BLK) < tc
    gt = jnp.where(is_ctx, mod_ref[n_b:n_b + 1, 2 * d:3 * d], mod_ref[pl.ds(b, 1), 2 * d:3 * d])
    xn = x_ref[0] + gt * out
    if final:
        xn = xn * lax.rsqrt(jnp.mean(xn * xn, axis=-1, keepdims=True) + NORM_EPS) * fnw_ref[...]
    o_ref[0] = xn


def _merge(outs, p, xa, mod_l, nws, wb, wo, fnw, tc, final):
    n_b, ta, d = xa.shape
    tb = TOK_BLK
    blk_off = tc // tb if final else 0
    n_blk = ta // tb - blk_off
    rows = lambda b, i: (b, i + blk_off, 0)
    pcol = lambda cb: (lambda b, i: (b, i + blk_off, cb))
    const2 = lambda b, i: (0, 0)
    in_specs = ([pl.BlockSpec((1, tb, 1024), rows)] * 6
                + [pl.BlockSpec((1, tb, 1024), pcol(C_DNGATE // 1024)),
                   pl.BlockSpec((1, tb, 1024), pcol(C_RGATE // 1024)),
                   pl.BlockSpec((1, tb, 1024), pcol(C_HGATE // 1024)),
                   pl.BlockSpec((1, tb, 1024), pcol(C_MERGE0 // 1024)),
                   pl.BlockSpec((1, tb, 1024), pcol(C_MERGE1 // 1024)),
                   pl.BlockSpec((1, tb, 1024), pcol(C_MERGE2 // 1024)),
                   pl.BlockSpec((1, tb, d), rows),
                   pl.BlockSpec((8, 3 * d), const2),
                   pl.BlockSpec((1, DN_DV), const2),
                   pl.BlockSpec((1, RET_DV), const2),
                   pl.BlockSpec((1, HG_DV), const2),
                   pl.BlockSpec((N_BRANCH, 1024, d), lambda b, i: (0, 0, 0)),
                   pl.BlockSpec((d, d), const2),
                   pl.BlockSpec((1, d), const2)])
    out_rows = ta - blk_off * tb
    return pl.pallas_call(
        functools.partial(_merge_kernel, tc=tc, blk_off=blk_off, n_b=n_b, final=final),
        grid=(n_b, n_blk),
        in_specs=in_specs,
        out_specs=pl.BlockSpec((1, tb, d), lambda b, i: (b, i, 0)),
        out_shape=jax.ShapeDtypeStruct((n_b, out_rows, d), F32),
        compiler_params=_cparams(("arbitrary", "arbitrary")),
        name="merge_final" if final else "merge",
    )(*outs, p, p, p, p, p, p, xa, mod_l, *nws, wb, wo, fnw)


def _permute_w_in(w):
    d = w.shape[0]
    ab0 = 2048 + 1024
    wp = jnp.concatenate([w[:, :ab0], w[:, ab0 + 16:]], axis=1).astype(BF16)
    wab = jnp.pad(w[:, ab0:ab0 + 16], ((0, 0), (0, LANES - 16))).astype(BF16)
    return wp, wab


def _rope_tables(n_lat, tc):
    half = RET_DK // 2
    inv = ROPE_BASE ** (-np.arange(0, half, 2, dtype=np.float64) / half)
    t = np.arange(n_lat)
    row = (t // GRID_W).astype(np.float64)
    col = (t % GRID_W).astype(np.float64)
    ang = np.concatenate([row[:, None] * inv, col[:, None] * inv], axis=-1)
    cos, sin = np.cos(ang), np.sin(ang)
    cos_l = np.repeat(cos, 2, axis=-1)
    sin_l = np.stack([-sin, sin], axis=-1).reshape(n_lat, LANES)
    cos_all = np.concatenate([np.ones((tc, LANES)), cos_l], axis=0).astype(np.float32)
    sin_all = np.concatenate([np.zeros((tc, LANES)), sin_l], axis=0).astype(np.float32)
    return jnp.asarray(cos_all), jnp.asarray(sin_all)


def _pad_row(v, width=LANES):
    v = v.reshape(1, -1).astype(F32)
    return jnp.pad(v, ((0, 0), (0, width - v.shape[1])))


def kernel(x, c, ctx, c_ctx, norm_w, ada_w, ada_b, w_in, dn_conv, dn_a_log, dn_dt_bias, dn_norm_w,
           ret_decay, ret_norm_w, hg_lb, hg_norm_w, w_branch, w_out, final_norm_w):
    n_b, n_lat, d = x.shape
    tc = ctx.shape[1]
    depth = norm_w.shape[0]
    assert d == D_MODEL and tc % TOK_BLK == 0 and n_lat % TOK_BLK == 0 and n_b + 1 <= 8
    assert n_lat % GRID_W == 0

    xa = jnp.concatenate([ctx, x], axis=1).astype(F32)
    cin = jnp.concatenate([c, c_ctx[None, :], jnp.zeros((8 - n_b - 1, d), c.dtype)], axis=0).astype(F32)
    mod = _modulation(cin, ada_w.astype(F32), ada_b.astype(F32))
    lower = _hg_lower(hg_lb)
    cos_t, sin_t = _rope_tables(n_lat, tc)
    fnw = final_norm_w.reshape(1, d).astype(F32)

    for l in range(depth):
        final = l == depth - 1
        wp, wab = _permute_w_in(w_in[l])
        p, pab = _project(xa, mod[l], norm_w[l].reshape(1, d).astype(F32), wp, wab, tc)
        dq, dk, dkt, dv, rq, rkt, gcol, grow, dktf, growf = _prep(
            p, pab, dn_conv[l].astype(F32), cos_t, sin_t, _pad_row(dn_a_log[l]), _pad_row(dn_dt_bias[l]), tc)
        lam_row = _pad_row(ret_decay[l])
        outs = []
        outs += _dn_scan(dq, dk, dktf, dkt, dv, gcol, growf, grow, tc)
        outs += _ret_scan(rq, rkt, p, lam_row, tc)
        outs += _hg_scan(p, (lower[l, 0].reshape(1, -1), lower[l, 1].reshape(1, -1)), tc)
        nws = (dn_norm_w[l].reshape(1, -1).astype(F32), ret_norm_w[l].reshape(1, -1).astype(F32),
               hg_norm_w[l].reshape(1, -1).astype(F32))
        xa = _merge(outs, p, xa, mod[l], nws, w_branch[l].astype(BF16), w_out[l].astype(BF16), fnw, tc, final)
    return xa
```

```python
import functools

import jax
import jax.numpy as jnp
import numpy as np
from jax import lax
from jax.experimental import pallas as pl
from jax.experimental.pallas import tpu as pltpu

F32 = jnp.float32
BF16 = jnp.bfloat16

D_MODEL = 1024
GRID_W = 64
DN_HEADS, DN_DK, DN_DV = 4, 128, 256
RET_HEADS, RET_DK, RET_DV = 4, 128, 256
HG_HEADS, HG_DK, HG_DV = 8, 128, 128
CONV_W = 5
ROPE_BASE = 10000.0
NORM_EPS = 1e-6
N_BRANCH = 3
BRANCH_W = 1024
QK_W = 512
DN_CONV_CH = 2 * QK_W + BRANCH_W

LANES = 128
TOK_BLK = 256
DN_CHUNK = 64
HG_CHUNK = 64
NEG_BIG = -1e30

C_DNQKV = 0
C_DNGATE = 2048
C_RQ = 3072
C_RK = 3584
C_RV = 4096
C_RGATE = 5120
C_HQ = 6144
C_HF = 7168
C_HI = 9216
C_HGATE = 10240
C_MERGE0 = 11264
C_MERGE1 = 12288
C_MERGE2 = 13312
P_WIDTH = 14336
PROJ_TILE = 3584
PROJ_SUB = 512
_PROJ_ACT_SLICES = "NNNN" "SSNN" "NNSS" "SSGG" "GGNN" "SSGG" "GGGG"
PROJ_ACT = tuple(_PROJ_ACT_SLICES[i:i + PROJ_TILE // PROJ_SUB]
                 for i in range(0, len(_PROJ_ACT_SLICES), PROJ_TILE // PROJ_SUB))
VMEM_LIMIT = 56 * 1024 * 1024


def _dot(a, b):
    return jnp.dot(a.astype(BF16), b.astype(BF16), preferred_element_type=F32)


def _dot_nt(a, b):
    return lax.dot_general(a.astype(BF16), b.astype(BF16), (((1,), (1,)), ((), ())),
                           preferred_element_type=F32)


def _dot_tn(a, b):
    return lax.dot_general(a.astype(BF16), b.astype(BF16), (((0,), (0,)), ((), ())),
                           preferred_element_type=F32)


def _dot_split3(m01, x, pieces=3):
    m = m01.astype(BF16)
    acc = None
    for _ in range(pieces):
        xp = x.astype(BF16)
        d = jnp.dot(m, xp, preferred_element_type=F32)
        acc = d if acc is None else acc + d
        x = x - xp.astype(F32)
    return acc


def _sigmoid(x):
    return 0.5 * jnp.tanh(0.5 * x) + 0.5


def _silu(x):
    h = 0.5 * x
    return h + h * jnp.tanh(h)


def _softplus(x):
    return jnp.maximum(x, 0.0) + jnp.log(1.0 + jnp.exp(-jnp.abs(x)))


def _iota(shape, dim):
    return lax.broadcasted_iota(jnp.int32, shape, dim)


def _shr(x, s):
    return lax.shift_right_logical(x, int(s).bit_length() - 1)


def _cparams(sem):
    return pltpu.CompilerParams(dimension_semantics=sem, vmem_limit_bytes=VMEM_LIMIT)


def _mod_kernel(cin_ref, w_ref, b_ref, o_ref):
    s = _silu(cin_ref[...])
    o_ref[0] = _dot(s, w_ref[0]) + b_ref[0]


def _modulation(cin, ada_w, ada_b):
    depth, d, d3 = ada_w.shape
    tn = 1024
    return pl.pallas_call(
        _mod_kernel,
        grid=(depth, d3 // tn),
        in_specs=[pl.BlockSpec((8, d), lambda l, j: (0, 0)),
                  pl.BlockSpec((1, d, tn), lambda l, j: (l, 0, j)),
                  pl.BlockSpec((1, 1, tn), lambda l, j: (l, 0, j))],
        out_specs=pl.BlockSpec((1, 8, tn), lambda l, j: (l, 0, j)),
        out_shape=jax.ShapeDtypeStruct((depth, 8, d3), F32),
        compiler_params=_cparams(("arbitrary", "arbitrary")),
        name="ada_mod",
    )(cin, ada_w, ada_b.reshape(depth, 1, d3))


def _lower_kernel(lb_ref, o_ref, *, depth):
    xs = [lb_ref[l] for l in range(depth)]
    m = xs[0]
    for l in range(1, depth):
        m = jnp.maximum(m, xs[l])
    es = [jnp.exp(x - m) for x in xs]
    tot = es[0]
    for l in range(1, depth):
        tot = tot + es[l]
    sm = [e / tot for e in es]
    acc = sm[0]
    o_ref[0] = acc - sm[0]
    for l in range(1, depth):
        acc = acc + sm[l]
        o_ref[l] = acc - sm[0]


def _hg_lower(hg_lb):
    depth = hg_lb.shape[0]
    lb = hg_lb.astype(F32).reshape(depth, 2, HG_HEADS * HG_DK)
    return pl.pallas_call(
        functools.partial(_lower_kernel, depth=depth),
        out_shape=jax.ShapeDtypeStruct(lb.shape, F32),
        name="hg_lower",
    )(lb)


def _proj_kernel(x_ref, mod_ref, nw_ref, w_ref, wab_ref, p_ref, pab_ref, hb_ref, *, tc, tm, tn, n_b):
    b = pl.program_id(0)
    i = pl.program_id(1)
    j = pl.program_id(2)
    d = D_MODEL

    @pl.when(j == 0)
    def _():
        x = x_ref[0]
        ms = jnp.mean(x * x, axis=-1, keepdims=True)
        y = x * lax.rsqrt(ms + NORM_EPS) * nw_ref[...]
        row = i * tm + _iota((tm, 1), 0)
        is_ctx = row < tc
        sh = jnp.where(is_ctx, mod_ref[n_b:n_b + 1, 0:d], mod_ref[pl.ds(b, 1), 0:d])
        sc = jnp.where(is_ctx, mod_ref[n_b:n_b + 1, d:2 * d], mod_ref[pl.ds(b, 1), d:2 * d])
        hb = (y * (1.0 + sc) + sh).astype(BF16)
        hb_ref[...] = hb
        pab_ref[0] = jnp.dot(hb, wab_ref[...], preferred_element_type=F32)

    sub = PROJ_SUB
    for pattern in sorted(set(PROJ_ACT)):
        tiles = [t for t, pat in enumerate(PROJ_ACT) if pat == pattern]
        cond = j == tiles[0]
        for t in tiles[1:]:
            cond = jnp.logical_or(cond, j == t)

        @pl.when(cond)
        def _(pattern=pattern):
            for s in range(tn // sub):
                r = jnp.dot(hb_ref[...], w_ref[:, s * sub:(s + 1) * sub], preferred_element_type=F32)
                if pattern[s] == "S":
                    r = _silu(r)
                elif pattern[s] == "G":
                    r = _sigmoid(r)
                p_ref[0, :, s * sub:(s + 1) * sub] = r.astype(BF16)


def _row_tile(ta):
    best = 16
    for t in range(16, 1101, 16):
        if ta % t == 0:
            best = t
    return best


def _project(xa, mod_l, nw, wp, wab, tc):
    n_b, ta, d = xa.shape
    tm = _row_tile(ta)
    tn = PROJ_TILE
    assert len(PROJ_ACT) * tn == P_WIDTH and all(len(pat) * PROJ_SUB == tn for pat in PROJ_ACT)
    return pl.pallas_call(
        functools.partial(_proj_kernel, tc=tc, tm=tm, tn=tn, n_b=n_b),
        grid=(n_b, ta // tm, P_WIDTH // tn),
        in_specs=[pl.BlockSpec((1, tm, d), lambda b, i, j: (b, i, 0)),
                  pl.BlockSpec((8, 3 * d), lambda b, i, j: (0, 0)),
                  pl.BlockSpec((1, d), lambda b, i, j: (0, 0)),
                  pl.BlockSpec((d, tn), lambda b, i, j: (0, j)),
                  pl.BlockSpec((d, LANES), lambda b, i, j: (0, 0))],
        out_specs=[pl.BlockSpec((1, tm, tn), lambda b, i, j: (b, i, j)),
                   pl.BlockSpec((1, tm, LANES), lambda b, i, j: (b, i, 0))],
        out_shape=[jax.ShapeDtypeStruct((n_b, ta, P_WIDTH), BF16),
                   jax.ShapeDtypeStruct((n_b, ta, LANES), F32)],
        scratch_shapes=[pltpu.VMEM((tm, d), BF16)],
        compiler_params=_cparams(("arbitrary", "arbitrary", "arbitrary")),
        name="in_proj",
    )(xa, mod_l, nw, wp, wab)


def _prep_kernel(main_ref, prev_ref, next_ref, rqk_ref, pab_ref, cw_ref, cos_ref, sin_ref, alog_ref, dtb_ref,
                 dq_ref, dk_ref, dkt_ref, dv_ref, rq_ref, rkt_ref, gcol_ref, grow_ref, dktf_ref, growf_ref,
                 xs_ref, *, tc, n_blk):
    i = pl.program_id(1)
    tb = TOK_BLK
    nctx = tc // tb
    has_prev = jnp.logical_and(i != 0, i != nctx)
    has_next = jnp.logical_and(i != nctx - 1, i != n_blk - 1)
    pm = jnp.where(has_prev, 1.0, 0.0)
    nm = jnp.where(has_next, 1.0, 0.0)
    n_ch = tb // DN_CHUNK

    for g in range(16):
        ls = slice(g * LANES, (g + 1) * LANES)
        xs_ref[8:8 + tb, :] = main_ref[0, :, ls].astype(F32)
        xs_ref[0:8, :] = prev_ref[0, 8:16, ls].astype(F32) * pm
        xs_ref[8 + tb:16 + tb, :] = next_ref[0, 0:8, ls].astype(F32) * nm
        acc = cw_ref[0:1, ls] * xs_ref[6:6 + tb, :]
        for t in range(1, CONV_W):
            acc = acc + cw_ref[t:t + 1, ls] * xs_ref[6 + t:6 + t + tb, :]
        y = _silu(acc)
        if g < 8:
            y = y * lax.rsqrt(jnp.sum(y * y, axis=-1, keepdims=True) + NORM_EPS)
        if g < 4:
            dq_ref[0, :, ls] = (y * (DN_DK ** -0.5)).astype(BF16)
        elif g < 8:
            h = g - 4
            hs = slice(h * LANES, (h + 1) * LANES)
            dk_ref[0, :, hs] = y.astype(BF16)
            dktf_ref[0, h] = y.T.astype(BF16)
            for c in range(n_ch):
                dkt_ref[0, h, c] = y[c * DN_CHUNK:(c + 1) * DN_CHUNK, :].T.astype(BF16)
        else:
            vs = slice((g - 8) * LANES, (g - 7) * LANES)
            dv_ref[0, :, vs] = y.astype(BF16)

    pab = pab_ref[0]
    g_log = -jnp.exp(alog_ref[...]) * _softplus(pab + dtb_ref[...])
    beta = _sigmoid(pab)
    r = _iota((tb, tb), 0)
    c_ = _iota((tb, tb), 1)
    same = _shr(r, DN_CHUNK) == _shr(c_, DN_CHUNK)
    m_f = jnp.where(jnp.logical_and(same, r >= c_), 1.0, 0.0)
    m_b = jnp.where(jnp.logical_and(same, r <= c_), 1.0, 0.0)
    cs_f = _dot_split3(m_f, g_log)
    cs_b = _dot_split3(m_b, g_log)
    lane = _iota((tb, LANES), 1)
    gcol = jnp.where(lane < 4, cs_f, jnp.where(lane < 8, cs_b, beta))
    gcol_ref[0] = gcol
    growf_ref[0] = gcol.T[0:8, :]
    for c in range(n_ch):
        grow_ref[0, c] = gcol[c * DN_CHUNK:(c + 1) * DN_CHUNK, :].T[0:8, :]

    cosv = cos_ref[...]
    sinv = sin_ref[...]
    even_lane = (_iota((tb, LANES), 1) & 1) == 0
    for g in range(8):
        ls = slice(g * LANES, (g + 1) * LANES)
        x = rqk_ref[0, :, ls].astype(F32)
        partner = jnp.where(even_lane, pltpu.roll(x, LANES - 1, 1), pltpu.roll(x, 1, 1))
        y = x * cosv + partner * sinv
        if g < 4:
            rq_ref[0, :, ls] = y.astype(BF16)
        else:
            rkt_ref[0, g - 4] = (y * (RET_DK ** -0.5)).T.astype(BF16)


def _prep(p, pab, conv_w, cos_t, sin_t, alog_row, dtb_row, tc):
    n_b, ta, _ = p.shape
    tb = TOK_BLK
    n_blk = ta // tb
    n_ch = tb // DN_CHUNK
    hpb = tb // 16
    n16 = ta // 16
    out_shape = [
        jax.ShapeDtypeStruct((n_b, ta, QK_W), BF16),
        jax.ShapeDtypeStruct((n_b, ta, QK_W), BF16),
        jax.ShapeDtypeStruct((n_b, DN_HEADS, ta // DN_CHUNK, DN_DK, DN_CHUNK), BF16),
        jax.ShapeDtypeStruct((n_b, ta, BRANCH_W), BF16),
        jax.ShapeDtypeStruct((n_b, ta, QK_W), BF16),
        jax.ShapeDtypeStruct((n_b, RET_HEADS, RET_DK, ta), BF16),
        jax.ShapeDtypeStruct((n_b, ta, LANES), F32),
        jax.ShapeDtypeStruct((n_b, ta // DN_CHUNK, 8, DN_CHUNK), F32),
        jax.ShapeDtypeStruct((n_b, DN_HEADS, DN_DK, ta), BF16),
        jax.ShapeDtypeStruct((n_b, 8, ta), F32),
    ]
    out_specs = [
        pl.BlockSpec((1, tb, QK_W), lambda b, i: (b, i, 0)),
        pl.BlockSpec((1, tb, QK_W), lambda b, i: (b, i, 0)),
        pl.BlockSpec((1, DN_HEADS, n_ch, DN_DK, DN_CHUNK), lambda b, i: (b, 0, i, 0, 0)),
        pl.BlockSpec((1, tb, BRANCH_W), lambda b, i: (b, i, 0)),
        pl.BlockSpec((1, tb, QK_W), lambda b, i: (b, i, 0)),
        pl.BlockSpec((1, RET_HEADS, RET_DK, tb), lambda b, i: (b, 0, 0, i)),
        pl.BlockSpec((1, tb, LANES), lambda b, i: (b, i, 0)),
        pl.BlockSpec((1, n_ch, 8, DN_CHUNK), lambda b, i: (b, i, 0, 0)),
        pl.BlockSpec((1, DN_HEADS, DN_DK, tb), lambda b, i: (b, 0, 0, i)),
        pl.BlockSpec((1, 8, tb), lambda b, i: (b, 0, i)),
    ]
    in_specs = [
        pl.BlockSpec((1, tb, DN_CONV_CH), lambda b, i: (b, i, 0)),
        pl.BlockSpec((1, 16, DN_CONV_CH), lambda b, i: (b, jnp.maximum(i * hpb - 1, 0), 0)),
        pl.BlockSpec((1, 16, DN_CONV_CH), lambda b, i: (b, jnp.minimum((i + 1) * hpb, n16 - 1), 0)),
        pl.BlockSpec((1, tb, BRANCH_W), lambda b, i: (b, i, C_RQ // BRANCH_W)),
        pl.BlockSpec((1, tb, LANES), lambda b, i: (b, i, 0)),
        pl.BlockSpec((CONV_W, DN_CONV_CH), lambda b, i: (0, 0)),
        pl.BlockSpec((tb, LANES), lambda b, i: (i, 0)),
        pl.BlockSpec((tb, LANES), lambda b, i: (i, 0)),
        pl.BlockSpec((1, LANES), lambda b, i: (0, 0)),
        pl.BlockSpec((1, LANES), lambda b, i: (0, 0)),
    ]
    return pl.pallas_call(
        functools.partial(_prep_kernel, tc=tc, n_blk=n_blk),
        grid=(n_b, n_blk),
        in_specs=in_specs,
        out_specs=out_specs,
        out_shape=out_shape,
        scratch_shapes=[pltpu.VMEM((tb + 16, LANES), F32)],
        compiler_params=_cparams(("arbitrary", "arbitrary")),
        name="prep",
    )(p, p, p, p, pab, conv_w, cos_t, sin_t, alog_row, dtb_row)


def _blk_index(n, n_blk, nctx, reverse):
    if not reverse:
        return n
    return jnp.where(n < nctx, nctx - 1 - n, n_blk - 1 - (n - nctx))


DN_MASK_NEG, DN_MASK_STRICT, DN_MASK_B8, DN_MASK_EYE, DN_MASK_OFF0 = 0, 1, 2, 3, 4
DN_MERGE_SIZES = (8, 16, 32)


DN_PAIR = 2 * DN_CHUNK


def _dn_masks(reverse):
    r = np.arange(DN_PAIR)[:, None]
    c = np.arange(DN_PAIR)[None, :]
    same = (r // DN_CHUNK) == (c // DN_CHUNK)
    incl = same & ((r <= c) if reverse else (r >= c))
    strict = incl & (r != c)
    ms = [np.where(incl, 0.0, NEG_BIG), strict, (r // 8) == (c // 8), r == c]
    for s in DN_MERGE_SIZES:
        ms.append(((r // (2 * s)) == (c // (2 * s))) & ((r // s) != (c // s)))
    return jnp.asarray(np.stack([np.asarray(m, np.float32) for m in ms]))


def _dn_streams(n, q_ref, k_ref, ktf_ref, ktc_ref, v_ref, gcol_ref, growf_ref, growc_ref, m_ref, o_ref, s_ref,
                wq_s, u_s, attn_s, kdt_s, egl_s, reverse):
    c_len = DN_CHUNK
    n_ch = TOK_BLK // c_len
    d_off = 4 if reverse else 0
    heads = range(DN_HEADS)
    last = 0 if reverse else c_len - 1
    pairs = [slice(g * DN_PAIR, (g + 1) * DN_PAIR) for g in range(TOK_BLK // DN_PAIR)]
    units = [(h, g) for h in heads for g in range(len(pairs))]

    wr = lax.rem(n, 2)
    rd = 1 - wr

    @pl.when(n == 0)
    def _():
        s_ref[...] = jnp.zeros_like(s_ref)
        wq_s[...] = jnp.zeros_like(wq_s)
        u_s[...] = jnp.zeros_like(u_s)
        attn_s[...] = jnp.zeros_like(attn_s)
        kdt_s[...] = jnp.zeros_like(kdt_s)
        egl_s[...] = jnp.zeros_like(egl_s)

    us = range(len(units))
    pa = {}

    def a_load():
        gcol = gcol_ref[0]
        growf = growf_ref[0]
        pa["gc"] = [gcol[:, d_off + h:d_off + h + 1] for h in heads]
        pa["bc"] = [gcol[:, 8 + d_off + h:9 + d_off + h] for h in heads]
        pa["egc"] = [jnp.exp(x) for x in pa["gc"]]
        pa["kb"] = [k_ref[0, :, h * DN_DK:(h + 1) * DN_DK].astype(F32) * pa["bc"][h] for h in heads]
        dec = [jnp.exp((pa["gc"][h][pairs[g]] - growf[d_off + h:d_off + h + 1, pairs[g]]) + m_ref[DN_MASK_NEG])
               for h, g in units]
        pa["a"] = [_dot(pa["kb"][h][pairs[g]], ktf_ref[0, h, :, pairs[g]]) * dec[u] * m_ref[DN_MASK_STRICT]
                   for u, (h, g) in enumerate(units)]

    def a_sq():
        pa["d"] = [(pa["a"][u] * m_ref[DN_MASK_B8]).astype(BF16) for u in us]
        pa["d2"] = [jnp.dot(pa["d"][u], pa["d"][u], preferred_element_type=F32) for u in us]

    def a_pow():
        d2b = [x.astype(BF16) for x in pa["d2"]]
        pa["d4"] = [jnp.dot(d2b[u], d2b[u], preferred_element_type=F32) for u in us]
        pa["d3"] = [jnp.dot(pa["d"][u], d2b[u], preferred_element_type=F32) for u in us]

    def a_base():
        p1 = [m_ref[DN_MASK_EYE] - pa["d"][u].astype(F32) + pa["d2"][u] - pa["d3"][u] for u in us]
        pa["tb"] = [(p1[u] + _dot(p1[u], pa["d4"][u])).astype(BF16) for u in us]

    def a_merge_x(lvl):
        def run():
            pa["x"] = [_dot(pa["a"][u] * m_ref[DN_MASK_OFF0 + lvl], pa["tb"][u]) for u in us]
        return run

    def a_merge_t():
        pa["tb"] = [(pa["tb"][u].astype(F32) - _dot(pa["tb"][u], pa["x"][u])).astype(BF16) for u in us]

    def a_store():
        neg64 = m_ref[DN_MASK_NEG, 0:c_len, 0:c_len]
        for u, (h, g) in enumerate(units):
            rows = pairs[g]
            rhs = jnp.concatenate([pa["kb"][h][rows] * pa["egc"][h][rows],
                                   v_ref[0, rows, h * DN_DV:(h + 1) * DN_DV].astype(F32) * pa["bc"][h][rows]], axis=1)
            wu = _dot(pa["tb"][u], rhs)
            qd = q_ref[0, rows, h * DN_DK:(h + 1) * DN_DK].astype(F32) * pa["egc"][h][rows]
            u_s[wr, h, rows, :] = wu[:, DN_DK:DN_DK + DN_DV]
            for ci in range(DN_PAIR // c_len):
                c = g * (DN_PAIR // c_len) + ci
                cs = slice(ci * c_len, (ci + 1) * c_len)
                wq_s[wr, h, c] = jnp.concatenate([wu[cs, 0:DN_DK], qd[cs]], axis=0).astype(BF16)
        for c in range(n_ch):
            rs = slice(c * c_len, (c + 1) * c_len)
            growc = growc_ref[0, c]
            for h in heads:
                gr = growc[d_off + h:d_off + h + 1, :]
                gl = gr[:, last:last + 1]
                kt = ktc_ref[0, h, c]
                attn = _dot(q_ref[0, rs, h * DN_DK:(h + 1) * DN_DK], kt) * jnp.exp((pa["gc"][h][rs] - gr) + neg64)
                attn_s[wr, h, c] = attn.astype(BF16)
                kdt_s[wr, h, c] = (kt.astype(F32) * jnp.exp(gl - gr)).astype(BF16)
                egl_s[wr, h, c] = jnp.broadcast_to(jnp.exp(gl), (8, LANES))

    pb = {"state": [s_ref[h] for h in heads]}

    def b_first(c):
        def run():
            pb["ws"] = [jnp.dot(wq_s[rd, h, c], pb["state"][h].astype(BF16), preferred_element_type=F32)
                        for h in heads]
        return run

    def b_second(c):
        def run():
            rs = slice(c * c_len, (c + 1) * c_len)
            vn = [(u_s[rd, h, rs, :] - pb["ws"][h][0:c_len]).astype(BF16) for h in heads]
            o = [pb["ws"][h][c_len:2 * c_len] + jnp.dot(attn_s[rd, h, c], vn[h], preferred_element_type=F32)
                 for h in heads]
            pb["state"] = [pb["state"][h] * egl_s[rd, h, c][0:1, 0:1]
                           + jnp.dot(kdt_s[rd, h, c], vn[h], preferred_element_type=F32) for h in heads]
            for h in heads:
                o_ref[0, rs, h * DN_DV:(h + 1) * DN_DV] = o[h].astype(BF16)
        return run

    a_stages = [a_load, a_sq, a_pow, a_base]
    for lvl in range(len(DN_MERGE_SIZES)):
        a_stages += [a_merge_x(lvl), a_merge_t]
    a_stages.append(a_store)
    b_stages = []
    for cc in range(n_ch):
        c = (n_ch - 1 - cc) if reverse else cc
        b_stages += [b_first(c), b_second(c)]

    def finish():
        for h in heads:
            s_ref[h] = pb["state"][h]

    return a_stages, b_stages, finish


DN_N_IN = 9
DN_N_SCRATCH = 6


def _dn_kernel(*refs):
    n = pl.program_id(1)
    ins_f, ins_b = refs[0:DN_N_IN], refs[DN_N_IN:2 * DN_N_IN]
    o_f, o_b = refs[2 * DN_N_IN], refs[2 * DN_N_IN + 1]
    scr = refs[2 * DN_N_IN + 2:]
    streams = [_dn_streams(n, *ins_f, o_f, *scr[0:DN_N_SCRATCH], reverse=False),
               _dn_streams(n, *ins_b, o_b, *scr[DN_N_SCRATCH:], reverse=True)]
    n_stage = max(max(len(a), len(b)) for a, b, _ in streams)
    for i in range(n_stage):
        for a_stages, _, _ in streams:
            if i < len(a_stages):
                a_stages[i]()
        for _, b_stages, _ in streams:
            if i < len(b_stages):
                b_stages[i]()
    for _, _, finish in streams:
        finish()


def _dn_scan(dq, dk, dktf, dktc, dv, gcol, growf, growc, tc):
    n_b, ta, _ = dq.shape
    tb = TOK_BLK
    n_blk = ta // tb
    nctx = tc // tb
    n_ch = tb // DN_CHUNK
    in_specs, args, out_specs, scratch = [], [], [], []
    for reverse in (False, True):
        masks = _dn_masks(reverse)
        blk = functools.partial(_blk_index, n_blk=n_blk, nctx=nctx, reverse=reverse)
        bi = lambda n, blk=blk: blk(jnp.minimum(n, n_blk - 1))
        bo = lambda n, blk=blk: blk(jnp.maximum(n - 1, 0))
        in_specs += [pl.BlockSpec((1, tb, QK_W), lambda b, n, bi=bi: (b, bi(n), 0)),
                     pl.BlockSpec((1, tb, QK_W), lambda b, n, bi=bi: (b, bi(n), 0)),
                     pl.BlockSpec((1, DN_HEADS, DN_DK, tb), lambda b, n, bi=bi: (b, 0, 0, bi(n))),
                     pl.BlockSpec((1, DN_HEADS, n_ch, DN_DK, DN_CHUNK), lambda b, n, bi=bi: (b, 0, bi(n), 0, 0)),
                     pl.BlockSpec((1, tb, BRANCH_W), lambda b, n, bi=bi: (b, bi(n), 0)),
                     pl.BlockSpec((1, tb, LANES), lambda b, n, bi=bi: (b, bi(n), 0)),
                     pl.BlockSpec((1, 8, tb), lambda b, n, bi=bi: (b, 0, bi(n))),
                     pl.BlockSpec((1, n_ch, 8, DN_CHUNK), lambda b, n, bi=bi: (b, bi(n), 0, 0)),
                     pl.BlockSpec(masks.shape, lambda b, n: (0, 0, 0))]
        args += [dq, dk, dktf, dktc, dv, gcol, growf, growc, masks]
        out_specs.append(pl.BlockSpec((1, tb, BRANCH_W), lambda b, n, bo=bo: (b, bo(n), 0)))
        scratch += [pltpu.VMEM((DN_HEADS, DN_DK, DN_DV), F32),
                    pltpu.VMEM((2, DN_HEADS, n_ch, 2 * DN_CHUNK, DN_DK), BF16),
                    pltpu.VMEM((2, DN_HEADS, tb, DN_DV), F32),
                    pltpu.VMEM((2, DN_HEADS, n_ch, DN_CHUNK, DN_CHUNK), BF16),
                    pltpu.VMEM((2, DN_HEADS, n_ch, DN_DK, DN_CHUNK), BF16),
                    pltpu.VMEM((2, DN_HEADS, n_ch, 8, LANES), F32)]
    assert len(args) == 2 * DN_N_IN and len(scratch) == 2 * DN_N_SCRATCH
    return pl.pallas_call(
        _dn_kernel,
        grid=(n_b, n_blk + 1),
        in_specs=in_specs,
        out_specs=out_specs,
        out_shape=[jax.ShapeDtypeStruct((n_b, ta, BRANCH_W), BF16)] * 2,
        scratch_shapes=scratch,
        compiler_params=_cparams(("arbitrary", "arbitrary")),
        name="dn_scan",
    )(*args)


def _ret_kernel(qf_ref, ktf_ref, vf_ref, qb_ref, ktb_ref, vb_ref, lam_ref, of_ref, ob_ref,
                s_ref, dec_ref, eq_ref, ek_ref, egl_ref):
    n = pl.program_id(1)
    c_len = TOK_BLK
    dirs = ((0, qf_ref, ktf_ref, vf_ref, of_ref), (1, qb_ref, ktb_ref, vb_ref, ob_ref))
    streams = [(d, h) for d in range(2) for h in range(RET_HEADS)]

    @pl.when(n == 0)
    def _():
        s_ref[...] = jnp.zeros_like(s_ref)
        ri = _iota((c_len, c_len), 0)
        ci = _iota((c_len, c_len), 1)
        rowpos = _iota((c_len, LANES), 0)
        colpos = _iota((RET_DK, c_len), 1)
        for d, h in streams:
            reverse = d == 1
            dist = (ci - ri) if reverse else (ri - ci)
            if reverse:
                qexp = (c_len - rowpos).astype(F32)
                kexp = colpos.astype(F32)
            else:
                qexp = (rowpos + 1).astype(F32)
                kexp = (c_len - 1 - colpos).astype(F32)
            x = lam_ref[0:1, 4 * d + h:4 * d + h + 1]
            lam = jnp.minimum(x, 0.0) - jnp.log(1.0 + jnp.exp(-jnp.abs(x)))
            dec_ref[d, h] = jnp.exp(jnp.where(dist >= 0, dist.astype(F32) * lam, NEG_BIG))
            eq_ref[d, h] = jnp.exp(qexp * lam)
            ek_ref[d, h] = jnp.exp(kexp * lam)
            egl_ref[d, h] = jnp.broadcast_to(jnp.exp(lam * float(c_len)), (8, LANES))

    def q_of(d, h):
        return dirs[d][1][0, :, h * RET_DK:(h + 1) * RET_DK]

    def v_of(d, h):
        return dirs[d][3][0, :, h * RET_DV:(h + 1) * RET_DV]

    s_old = {u: s_ref[u[0], u[1]] for u in streams}
    qk = {u: _dot(q_of(*u), dirs[u[0]][2][0, u[1]]) for u in streams}
    kv = {u: _dot(dirs[u[0]][2][0, u[1]].astype(F32) * ek_ref[u[0], u[1]], v_of(*u)) for u in streams}
    qs = {u: _dot(q_of(*u).astype(F32) * eq_ref[u[0], u[1]], s_old[u]) for u in streams}
    av = {u: _dot(qk[u] * dec_ref[u[0], u[1]], v_of(*u)) for u in streams}
    for u in streams:
        d, h = u
        s_ref[d, h] = s_old[u] * egl_ref[d, h][0:1, 0:1] + kv[u]
        dirs[d][4][0, :, h * RET_DV:(h + 1) * RET_DV] = (av[u] + qs[u]).astype(BF16)


def _ret_scan(rq, rkt, p, lam_row, tc):
    n_b, ta, _ = rq.shape
    tb = TOK_BLK
    n_blk = ta // tb
    nctx = tc // tb
    in_specs, out_specs = [], []
    for reverse in (False, True):
        bi = functools.partial(_blk_index, n_blk=n_blk, nctx=nctx, reverse=reverse)
        in_specs += [pl.BlockSpec((1, tb, QK_W), lambda b, n, bi=bi: (b, bi(n), 0)),
                     pl.BlockSpec((1, RET_HEADS, RET_DK, tb), lambda b, n, bi=bi: (b, 0, 0, bi(n))),
                     pl.BlockSpec((1, tb, BRANCH_W), lambda b, n, bi=bi: (b, bi(n), C_RV // BRANCH_W))]
        out_specs.append(pl.BlockSpec((1, tb, BRANCH_W), lambda b, n, bi=bi: (b, bi(n), 0)))
    in_specs.append(pl.BlockSpec((1, LANES), lambda b, n: (0, 0)))
    return pl.pallas_call(
        _ret_kernel,
        grid=(n_b, n_blk),
        in_specs=in_specs,
        out_specs=out_specs,
        out_shape=[jax.ShapeDtypeStruct((n_b, ta, BRANCH_W), BF16)] * 2,
        scratch_shapes=[pltpu.VMEM((2, RET_HEADS, RET_DK, RET_DV), F32),
                        pltpu.VMEM((2, RET_HEADS, tb, tb), F32),
                        pltpu.VMEM((2, RET_HEADS, tb, LANES), F32),
                        pltpu.VMEM((2, RET_HEADS, RET_DK, tb), F32),
                        pltpu.VMEM((2, RET_HEADS, 8, LANES), F32)],
        compiler_params=_cparams(("arbitrary", "arbitrary")),
        name="ret_scan",
    )(rq, rkt, p, rq, rkt, p, lam_row)


HG_LEVELS = (1, 2, 4, 8, 16, 32)


def _neg_abs(x):
    bits = lax.bitcast_convert_type(x, jnp.uint32) | jnp.uint32(0x80000000)
    return lax.bitcast_convert_type(bits, F32)


def _hg_pair_masks(reverse):
    i = np.arange(HG_CHUNK)[:, None]
    j = np.arange(HG_CHUNK)[None, :]
    ms = []
    for s in HG_LEVELS:
        q_half = 0 if reverse else 1
        ms.append(((i // (2 * s)) == (j // (2 * s))) & (((i // s) & 1) == q_half) & (((j // s) & 1) == 1 - q_half))
    ms.append(i == j)
    return jnp.asarray(np.stack(ms).astype(np.float32))


def _hg_boundary(gc_s, gc, base, hs, s, row8, reverse):
    c_len = HG_CHUNK
    off = s - 1 if reverse else s
    if s == 1:
        rowi = _iota((c_len, HG_DK), 0)
        if reverse:
            return jnp.where((rowi & 1) == 1, pltpu.roll(gc, 1, 0), gc)
        return jnp.where((rowi & 1) == 0, pltpu.roll(gc, c_len - 1, 0), gc)
    parts = []
    for vi in range(c_len // 8):
        r0 = base + 8 * vi
        if s == 2:
            lo = jnp.broadcast_to(gc_s[r0 + off:r0 + off + 1, hs], (8, HG_DK))
            hi = jnp.broadcast_to(gc_s[r0 + 4 + off:r0 + 5 + off, hs], (8, HG_DK))
            parts.append(jnp.where(row8 < 4, lo, hi))
        else:
            m = base + ((8 * vi) // (2 * s)) * (2 * s) + off
            parts.append(jnp.broadcast_to(gc_s[m:m + 1, hs], (8, HG_DK)))
    return jnp.concatenate(parts, axis=0)


HG_N_IN = 5
HG_N_SCRATCH = 3


def _hg_kernel(*refs):
    n = pl.program_id(1)
    o_f, o_b = refs[2 * HG_N_IN], refs[2 * HG_N_IN + 1]
    scr = refs[2 * HG_N_IN + 2:]
    _hg_direction(n, *refs[0:HG_N_IN], o_f, *scr[0:HG_N_SCRATCH], reverse=False)
    _hg_direction(n, *refs[HG_N_IN:2 * HG_N_IN], o_b, *scr[HG_N_SCRATCH:], reverse=True)


def _hg_direction(n, xq_ref, xf_ref, xi_ref, low_ref, pm_ref, o_ref, st_ref, k_s, gc_s, *, reverse):
    tb = TOK_BLK
    c_len = HG_CHUNK
    n_ch = tb // c_len

    @pl.when(n == 0)
    def _():
        st_ref[...] = jnp.zeros_like(st_ref)

    low = low_ref[...]
    f = low + (1.0 - low) * xf_ref[0].astype(F32)
    g = jnp.log2(f)
    r = _iota((tb, tb), 0)
    c_ = _iota((tb, tb), 1)
    same = _shr(r, c_len) == _shr(c_, c_len)
    tri = (r <= c_) if reverse else (r >= c_)
    gc_s[...] = _dot_split3(jnp.where(jnp.logical_and(same, tri), 1.0, 0.0), g, pieces=2)
    k_s[...] = (1.0 - f).astype(BF16)

    row8 = _iota((8, HG_DK), 0)
    last = 0 if reverse else c_len - 1
    hsl = [slice(h * HG_DK, (h + 1) * HG_DK) for h in range(HG_HEADS)]
    heads = range(HG_HEADS)

    def scores(c):
        base = c * c_len
        rs = slice(base, base + c_len)
        q = {h: xq_ref[0, rs, hsl[h]].astype(F32) for h in heads}
        k = {h: k_s[rs, hsl[h]].astype(F32) for h in heads}
        gc = {h: gc_s[rs, hsl[h]] for h in heads}
        tiles = range(c_len // 8)
        t8 = lambda x, v: x[8 * v:8 * v + 8]
        diag = pm_ref[len(HG_LEVELS)]
        a = {h: [t8(x, v) for v in tiles] for h, x in ((h, _dot_nt(q[h], k[h]) * diag) for h in heads)}
        q_half = 0 if reverse else 1
        for lvl, s in enumerate(HG_LEVELS):
            pm = pm_ref[lvl]
            if s < 8:
                is_q = (_shr(row8, s) & 1) == q_half
                pick = lambda h, v: jnp.where(is_q, t8(q[h], v), t8(k[h], v))
                q_tiles = list(tiles)
            else:
                pick = lambda h, v: t8(q[h], v) if ((8 * v) // s) & 1 == q_half else t8(k[h], v)
                q_tiles = [v for v in tiles if ((8 * v) // s) & 1 == q_half]
            z = {}
            for h in heads:
                e = jnp.exp2(_neg_abs(gc[h] - _hg_boundary(gc_s, gc[h], base, hsl[h], s, row8, reverse)))
                z[h] = jnp.concatenate([pick(h, v) * t8(e, v) for v in tiles], axis=0)
            pr = {h: _dot_nt(z[h], z[h]) for h in heads}
            for h in heads:
                for v in q_tiles:
                    a[h][v] = a[h][v] + t8(pr[h], v) * t8(pm, v)
        return {h: jnp.concatenate(a[h], axis=0) for h in heads}

    def outputs(c, a):
        base = c * c_len
        rs = slice(base, base + c_len)
        q = {h: xq_ref[0, rs, hsl[h]].astype(F32) for h in heads}
        k = {h: k_s[rs, hsl[h]].astype(F32) for h in heads}
        gc = {h: gc_s[rs, hsl[h]] for h in heads}
        gl = {h: gc[h][last:last + 1, :] for h in heads}
        st = {h: st_ref[h] for h in heads}
        o = {h: _dot(a[h], xi_ref[0, rs, hsl[h]]) + _dot_nt(q[h] * jnp.exp2(gc[h]), st[h]) for h in heads}
        for h in heads:
            kd = k[h] * jnp.exp2(gl[h] - gc[h])
            st_ref[h] = st[h] * jnp.exp2(gl[h]) + _dot_tn(xi_ref[0, rs, hsl[h]], kd)
            o_ref[0, rs, hsl[h]] = o[h].astype(BF16)

    order = [(n_ch - 1 - cc) if reverse else cc for cc in range(n_ch)]
    pending = None
    for c in order:
        a = scores(c)
        if pending is not None:
            outputs(*pending)
        pending = (c, a)
    outputs(*pending)


def _hg_scan(p, low_rows, tc):
    n_b, ta, _ = p.shape
    tb = TOK_BLK
    n_blk = ta // tb
    nctx = tc // tb
    in_specs, args, out_specs, scratch = [], [], [], []
    for d_i, reverse in enumerate((False, True)):
        masks = _hg_pair_masks(reverse)
        bi = functools.partial(_blk_index, n_blk=n_blk, nctx=nctx, reverse=reverse)
        in_specs += [pl.BlockSpec((1, tb, BRANCH_W), lambda b, n, bi=bi: (b, bi(n), C_HQ // BRANCH_W)),
                     pl.BlockSpec((1, tb, BRANCH_W), lambda b, n, bi=bi, d_i=d_i: (b, bi(n), C_HF // BRANCH_W + d_i)),
                     pl.BlockSpec((1, tb, BRANCH_W), lambda b, n, bi=bi: (b, bi(n), C_HI // BRANCH_W)),
                     pl.BlockSpec((1, BRANCH_W), lambda b, n: (0, 0)),
                     pl.BlockSpec(masks.shape, lambda b, n: (0, 0, 0))]
        args += [p, p, p, low_rows[d_i], masks]
        out_specs.append(pl.BlockSpec((1, tb, BRANCH_W), lambda b, n, bi=bi: (b, bi(n), 0)))
        scratch += [pltpu.VMEM((HG_HEADS, HG_DV, HG_DK), F32),
                    pltpu.VMEM((tb, BRANCH_W), BF16),
                    pltpu.VMEM((tb, BRANCH_W), F32)]
    assert len(args) == 2 * HG_N_IN and len(scratch) == 2 * HG_N_SCRATCH
    return pl.pallas_call(
        _hg_kernel,
        grid=(n_b, n_blk),
        in_specs=in_specs,
        out_specs=out_specs,
        out_shape=[jax.ShapeDtypeStruct((n_b, ta, BRANCH_W), BF16)] * 2,
        scratch_shapes=scratch,
        compiler_params=_cparams(("arbitrary", "arbitrary")),
        name="hg_scan",
    )(*args)


def _branch(of_ref, ob_ref, gate_ref, nw_ref, dv):
    o = of_ref[0].astype(F32) + ob_ref[0].astype(F32)
    gate = gate_ref[0].astype(F32)
    nw = nw_ref[...]
    parts = []
    for h in range(BRANCH_W // dv):
        x = o[:, h * dv:(h + 1) * dv]
        y = x * lax.rsqrt(jnp.mean(x * x, axis=-1, keepdims=True) + NORM_EPS) * nw
        parts.append(y)
    return (jnp.concatenate(parts, axis=-1) * gate).astype(BF16)


def _merge_kernel(dnf, dnb, rtf, rtb, hgf, hgb, g_dn, g_rt, g_hg, m0, m1, m2, x_ref, mod_ref,
                  nw_dn, nw_rt, nw_hg, wb_ref, wo_ref, fnw_ref, o_ref, *, tc, blk_off, n_b, final):
    b = pl.program_id(0)
    i = pl.program_id(1) + blk_off
    d = D_MODEL
    brs = (_branch(dnf, dnb, g_dn, nw_dn, DN_DV),
           _branch(rtf, rtb, g_rt, nw_rt, RET_DV),
           _branch(hgf, hgb, g_hg, nw_hg, HG_DV))
    y = None
    for idx, (br, mg) in enumerate(zip(brs, (m0, m1, m2))):
        t = jnp.dot(br, wb_ref[idx], preferred_element_type=F32) * mg[0].astype(F32)
        y = t if y is None else y + t
    out = jnp.dot(y.astype(BF16), wo_ref[...], preferred_element_type=F32)
    is_ctx = (i * TOK_BLK) < tc
    gt = jnp.where(is_ctx, mod_ref[n_b:n_b + 1, 2 * d:3 * d], mod_ref[pl.ds(b, 1), 2 * d:3 * d])
    xn = x_ref[0] + gt * out
    if final:
        xn = xn * lax.rsqrt(jnp.mean(xn * xn, axis=-1, keepdims=True) + NORM_EPS) * fnw_ref[...]
    o_ref[0] = xn


def _merge(outs, p, xa, mod_l, nws, wb, wo, fnw, tc, final):
    n_b, ta, d = xa.shape
    tb = TOK_BLK
    blk_off = tc // tb if final else 0
    n_blk = ta // tb - blk_off
    rows = lambda b, i: (b, i + blk_off, 0)
    pcol = lambda cb: (lambda b, i: (b, i + blk_off, cb))
    const2 = lambda b, i: (0, 0)
    in_specs = ([pl.BlockSpec((1, tb, BRANCH_W), rows)] * 6
                + [pl.BlockSpec((1, tb, BRANCH_W), pcol(C_DNGATE // BRANCH_W)),
                   pl.BlockSpec((1, tb, BRANCH_W), pcol(C_RGATE // BRANCH_W)),
                   pl.BlockSpec((1, tb, BRANCH_W), pcol(C_HGATE // BRANCH_W)),
                   pl.BlockSpec((1, tb, BRANCH_W), pcol(C_MERGE0 // BRANCH_W)),
                   pl.BlockSpec((1, tb, BRANCH_W), pcol(C_MERGE1 // BRANCH_W)),
                   pl.BlockSpec((1, tb, BRANCH_W), pcol(C_MERGE2 // BRANCH_W)),
                   pl.BlockSpec((1, tb, d), rows),
                   pl.BlockSpec((8, 3 * d), const2),
                   pl.BlockSpec((1, DN_DV), const2),
                   pl.BlockSpec((1, RET_DV), const2),
                   pl.BlockSpec((1, HG_DV), const2),
                   pl.BlockSpec((N_BRANCH, BRANCH_W, d), lambda b, i: (0, 0, 0)),
                   pl.BlockSpec((d, d), const2),
                   pl.BlockSpec((1, d), const2)])
    out_rows = ta - blk_off * tb
    return pl.pallas_call(
        functools.partial(_merge_kernel, tc=tc, blk_off=blk_off, n_b=n_b, final=final),
        grid=(n_b, n_blk),
        in_specs=in_specs,
        out_specs=pl.BlockSpec((1, tb, d), lambda b, i: (b, i, 0)),
        out_shape=jax.ShapeDtypeStruct((n_b, out_rows, d), F32),
        compiler_params=_cparams(("arbitrary", "arbitrary")),
        name="merge_final" if final else "merge",
    )(*outs, p, p, p, p, p, p, xa, mod_l, *nws, wb, wo, fnw)


def _permute_w_in(w):
    d = w.shape[0]
    ab0 = DN_CONV_CH + BRANCH_W
    wp = jnp.concatenate([w[:, :ab0], w[:, ab0 + 16:]], axis=1).astype(BF16)
    wab = jnp.pad(w[:, ab0:ab0 + 16], ((0, 0), (0, LANES - 16))).astype(BF16)
    return wp, wab


def _rope_tables(n_lat, tc):
    half = RET_DK // 2
    inv = ROPE_BASE ** (-np.arange(0, half, 2, dtype=np.float64) / half)
    t = np.arange(n_lat)
    row = (t // GRID_W).astype(np.float64)
    col = (t % GRID_W).astype(np.float64)
    ang = np.concatenate([row[:, None] * inv, col[:, None] * inv], axis=-1)
    cos, sin = np.cos(ang), np.sin(ang)
    cos_l = np.repeat(cos, 2, axis=-1)
    sin_l = np.stack([-sin, sin], axis=-1).reshape(n_lat, LANES)
    cos_all = np.concatenate([np.ones((tc, LANES)), cos_l], axis=0).astype(np.float32)
    sin_all = np.concatenate([np.zeros((tc, LANES)), sin_l], axis=0).astype(np.float32)
    return jnp.asarray(cos_all), jnp.asarray(sin_all)


def _pad_row(v, width=LANES):
    v = v.reshape(1, -1).astype(F32)
    return jnp.pad(v, ((0, 0), (0, width - v.shape[1])))


def kernel(x, c, ctx, c_ctx, norm_w, ada_w, ada_b, w_in, dn_conv, dn_a_log, dn_dt_bias, dn_norm_w,
           ret_decay, ret_norm_w, hg_lb, hg_norm_w, w_branch, w_out, final_norm_w):
    n_b, n_lat, d = x.shape
    tc = ctx.shape[1]
    depth = norm_w.shape[0]
    assert d == D_MODEL and tc % TOK_BLK == 0 and n_lat % TOK_BLK == 0 and n_b + 1 <= 8
    assert n_lat % GRID_W == 0

    xa = jnp.concatenate([ctx, x], axis=1).astype(F32)
    cin = jnp.concatenate([c, c_ctx[None, :], jnp.zeros((8 - n_b - 1, d), c.dtype)], axis=0).astype(F32)
    mod = _modulation(cin, ada_w.astype(F32), ada_b.astype(F32))
    lower = _hg_lower(hg_lb)
    cos_t, sin_t = _rope_tables(n_lat, tc)
    fnw = final_norm_w.reshape(1, d).astype(F32)

    for l in range(depth):
        final = l == depth - 1
        wp, wab = _permute_w_in(w_in[l])
        p, pab = _project(xa, mod[l], norm_w[l].reshape(1, d).astype(F32), wp, wab, tc)
        dq, dk, dkt, dv, rq, rkt, gcol, grow, dktf, growf = _prep(
            p, pab, dn_conv[l].astype(F32), cos_t, sin_t, _pad_row(dn_a_log[l]), _pad_row(dn_dt_bias[l]), tc)
        lam_row = _pad_row(ret_decay[l])
        outs = []
        outs += _dn_scan(dq, dk, dktf, dkt, dv, gcol, growf, grow, tc)
        outs += _ret_scan(rq, rkt, p, lam_row, tc)
        outs += _hg_scan(p, (lower[l, 0].reshape(1, -1), lower[l, 1].reshape(1, -1)), tc)
        nws = (dn_norm_w[l].reshape(1, -1).astype(F32), ret_norm_w[l].reshape(1, -1).astype(F32),
               hg_norm_w[l].reshape(1, -1).astype(F32))
        xa = _merge(outs, p, xa, mod[l], nws, w_branch[l].astype(BF16), w_out[l].astype(BF16), fnw, tc, final)
    return xa
```

```python
import functools

import jax
import jax.numpy as jnp
import numpy as np
from jax import lax
from jax.experimental import pallas as pl
from jax.experimental.pallas import tpu as pltpu

F32 = jnp.float32
BF16 = jnp.bfloat16

D_MODEL = 1024
GRID_W = 64
DN_HEADS, DN_DK, DN_DV = 4, 128, 256
RET_HEADS, RET_DK, RET_DV = 4, 128, 256
HG_HEADS, HG_DK, HG_DV = 8, 128, 128
CONV_W = 5
ROPE_BASE = 10000.0
NORM_EPS = 1e-6
N_BRANCH = 3
BRANCH_W = 1024
QK_W = 512
DN_CONV_CH = 2 * QK_W + BRANCH_W

LANES = 128
TOK_BLK = 256
DN_CHUNK = 64
HG_CHUNK = 64
NEG_BIG = -1e30

C_DNQKV = 0
C_DNGATE = 2048
C_RQ = 3072
C_RK = 3584
C_RV = 4096
C_RGATE = 5120
C_HQ = 6144
C_HF = 7168
C_HI = 9216
C_HGATE = 10240
C_MERGE0 = 11264
C_MERGE1 = 12288
C_MERGE2 = 13312
P_WIDTH = 14336
PROJ_TILE = 3584
PROJ_SUB = 512
_PROJ_ACT_SLICES = "NNNN" "SSNN" "NNSS" "SSGG" "GGNN" "SSGG" "GGGG"
PROJ_ACT = tuple(_PROJ_ACT_SLICES[i:i + PROJ_TILE // PROJ_SUB]
                 for i in range(0, len(_PROJ_ACT_SLICES), PROJ_TILE // PROJ_SUB))
VMEM_LIMIT = 56 * 1024 * 1024


def _dot(a, b):
    return jnp.dot(a.astype(BF16), b.astype(BF16), preferred_element_type=F32)


def _dot_nt(a, b):
    return lax.dot_general(a.astype(BF16), b.astype(BF16), (((1,), (1,)), ((), ())),
                           preferred_element_type=F32)


def _dot_tn(a, b):
    return lax.dot_general(a.astype(BF16), b.astype(BF16), (((0,), (0,)), ((), ())),
                           preferred_element_type=F32)


def _dot_split3(m01, x, pieces=3):
    m = m01.astype(BF16)
    acc = None
    for _ in range(pieces):
        xp = x.astype(BF16)
        d = jnp.dot(m, xp, preferred_element_type=F32)
        acc = d if acc is None else acc + d
        x = x - xp.astype(F32)
    return acc


def _sigmoid(x):
    return 0.5 * jnp.tanh(0.5 * x) + 0.5


def _silu(x):
    h = 0.5 * x
    return h + h * jnp.tanh(h)


def _softplus(x):
    return jnp.maximum(x, 0.0) + jnp.log(1.0 + jnp.exp(-jnp.abs(x)))


def _iota(shape, dim):
    return lax.broadcasted_iota(jnp.int32, shape, dim)


def _shr(x, s):
    return lax.shift_right_logical(x, int(s).bit_length() - 1)


def _cparams(sem):
    return pltpu.CompilerParams(dimension_semantics=sem, vmem_limit_bytes=VMEM_LIMIT)


def _mod_kernel(cin_ref, w_ref, b_ref, o_ref):
    s = _silu(cin_ref[...])
    o_ref[0] = _dot(s, w_ref[0]) + b_ref[0]


def _modulation(cin, ada_w, ada_b):
    depth, d, d3 = ada_w.shape
    tn = 1024
    return pl.pallas_call(
        _mod_kernel,
        grid=(depth, d3 // tn),
        in_specs=[pl.BlockSpec((8, d), lambda l, j: (0, 0)),
                  pl.BlockSpec((1, d, tn), lambda l, j: (l, 0, j)),
                  pl.BlockSpec((1, 1, tn), lambda l, j: (l, 0, j))],
        out_specs=pl.BlockSpec((1, 8, tn), lambda l, j: (l, 0, j)),
        out_shape=jax.ShapeDtypeStruct((depth, 8, d3), F32),
        compiler_params=_cparams(("arbitrary", "arbitrary")),
        name="ada_mod",
    )(cin, ada_w, ada_b.reshape(depth, 1, d3))


def _lower_kernel(lb_ref, o_ref, *, depth):
    xs = [lb_ref[l] for l in range(depth)]
    m = xs[0]
    for l in range(1, depth):
        m = jnp.maximum(m, xs[l])
    es = [jnp.exp(x - m) for x in xs]
    tot = es[0]
    for l in range(1, depth):
        tot = tot + es[l]
    sm = [e / tot for e in es]
    acc = sm[0]
    o_ref[0] = acc - sm[0]
    for l in range(1, depth):
        acc = acc + sm[l]
        o_ref[l] = acc - sm[0]


def _hg_lower(hg_lb):
    depth = hg_lb.shape[0]
    lb = hg_lb.astype(F32).reshape(depth, 2, HG_HEADS * HG_DK)
    return pl.pallas_call(
        functools.partial(_lower_kernel, depth=depth),
        out_shape=jax.ShapeDtypeStruct(lb.shape, F32),
        name="hg_lower",
    )(lb)


def _proj_kernel(x_ref, mod_ref, nw_ref, w_ref, wab_ref, p_ref, pab_ref, hb_ref, *, tc, tm, tn, n_b):
    b = pl.program_id(0)
    i = pl.program_id(1)
    j = pl.program_id(2)
    d = D_MODEL

    @pl.when(j == 0)
    def _():
        x = x_ref[0]
        ms = jnp.mean(x * x, axis=-1, keepdims=True)
        y = x * lax.rsqrt(ms + NORM_EPS) * nw_ref[...]
        row = i * tm + _iota((tm, 1), 0)
        is_ctx = row < tc
        sh = jnp.where(is_ctx, mod_ref[n_b:n_b + 1, 0:d], mod_ref[pl.ds(b, 1), 0:d])
        sc = jnp.where(is_ctx, mod_ref[n_b:n_b + 1, d:2 * d], mod_ref[pl.ds(b, 1), d:2 * d])
        hb = (y * (1.0 + sc) + sh).astype(BF16)
        hb_ref[...] = hb
        pab_ref[0] = jnp.dot(hb, wab_ref[...], preferred_element_type=F32)

    sub = PROJ_SUB
    for pattern in sorted(set(PROJ_ACT)):
        tiles = [t for t, pat in enumerate(PROJ_ACT) if pat == pattern]
        cond = j == tiles[0]
        for t in tiles[1:]:
            cond = jnp.logical_or(cond, j == t)

        @pl.when(cond)
        def _(pattern=pattern):
            for s in range(tn // sub):
                r = jnp.dot(hb_ref[...], w_ref[:, s * sub:(s + 1) * sub], preferred_element_type=F32)
                if pattern[s] == "S":
                    r = _silu(r)
                elif pattern[s] == "G":
                    r = _sigmoid(r)
                p_ref[0, :, s * sub:(s + 1) * sub] = r.astype(BF16)


def _row_tile(ta):
    best = 16
    for t in range(16, 1101, 16):
        if ta % t == 0:
            best = t
    return best


def _project(xa, mod_l, nw, wp, wab, tc):
    n_b, ta, d = xa.shape
    tm = _row_tile(ta)
    tn = PROJ_TILE
    assert len(PROJ_ACT) * tn == P_WIDTH and all(len(pat) * PROJ_SUB == tn for pat in PROJ_ACT)
    return pl.pallas_call(
        functools.partial(_proj_kernel, tc=tc, tm=tm, tn=tn, n_b=n_b),
        grid=(n_b, ta // tm, P_WIDTH // tn),
        in_specs=[pl.BlockSpec((1, tm, d), lambda b, i, j: (b, i, 0)),
                  pl.BlockSpec((8, 3 * d), lambda b, i, j: (0, 0)),
                  pl.BlockSpec((1, d), lambda b, i, j: (0, 0)),
                  pl.BlockSpec((d, tn), lambda b, i, j: (0, j)),
                  pl.BlockSpec((d, LANES), lambda b, i, j: (0, 0))],
        out_specs=[pl.BlockSpec((1, tm, tn), lambda b, i, j: (b, i, j)),
                   pl.BlockSpec((1, tm, LANES), lambda b, i, j: (b, i, 0))],
        out_shape=[jax.ShapeDtypeStruct((n_b, ta, P_WIDTH), BF16),
                   jax.ShapeDtypeStruct((n_b, ta, LANES), F32)],
        scratch_shapes=[pltpu.VMEM((tm, d), BF16)],
        compiler_params=_cparams(("arbitrary", "arbitrary", "arbitrary")),
        name="in_proj",
    )(xa, mod_l, nw, wp, wab)


def _prep_kernel(main_ref, prev_ref, next_ref, rqk_ref, pab_ref, cw_ref, cos_ref, sin_ref, alog_ref, dtb_ref,
                 dq_ref, dk_ref, dkt_ref, dv_ref, rq_ref, rkt_ref, gcol_ref, grow_ref, dktf_ref, growf_ref,
                 xs_ref, *, tc, n_blk):
    i = pl.program_id(1)
    tb = TOK_BLK
    nctx = tc // tb
    has_prev = jnp.logical_and(i != 0, i != nctx)
    has_next = jnp.logical_and(i != nctx - 1, i != n_blk - 1)
    pm = jnp.where(has_prev, 1.0, 0.0)
    nm = jnp.where(has_next, 1.0, 0.0)
    n_ch = tb // DN_CHUNK

    for g in range(16):
        ls = slice(g * LANES, (g + 1) * LANES)
        xs_ref[8:8 + tb, :] = main_ref[0, :, ls].astype(F32)
        xs_ref[0:8, :] = prev_ref[0, 8:16, ls].astype(F32) * pm
        xs_ref[8 + tb:16 + tb, :] = next_ref[0, 0:8, ls].astype(F32) * nm
        acc = cw_ref[0:1, ls] * xs_ref[6:6 + tb, :]
        for t in range(1, CONV_W):
            acc = acc + cw_ref[t:t + 1, ls] * xs_ref[6 + t:6 + t + tb, :]
        y = _silu(acc)
        if g < 8:
            y = y * lax.rsqrt(jnp.sum(y * y, axis=-1, keepdims=True) + NORM_EPS)
        if g < 4:
            dq_ref[0, :, ls] = (y * (DN_DK ** -0.5)).astype(BF16)
        elif g < 8:
            h = g - 4
            hs = slice(h * LANES, (h + 1) * LANES)
            dk_ref[0, :, hs] = y.astype(BF16)
            dktf_ref[0, h] = y.T.astype(BF16)
            for c in range(n_ch):
                dkt_ref[0, h, c] = y[c * DN_CHUNK:(c + 1) * DN_CHUNK, :].T.astype(BF16)
        else:
            vs = slice((g - 8) * LANES, (g - 7) * LANES)
            dv_ref[0, :, vs] = y.astype(BF16)

    pab = pab_ref[0]
    g_log = -jnp.exp(alog_ref[...]) * _softplus(pab + dtb_ref[...])
    beta = _sigmoid(pab)
    r = _iota((tb, tb), 0)
    c_ = _iota((tb, tb), 1)
    same = _shr(r, DN_CHUNK) == _shr(c_, DN_CHUNK)
    m_f = jnp.where(jnp.logical_and(same, r >= c_), 1.0, 0.0)
    m_b = jnp.where(jnp.logical_and(same, r <= c_), 1.0, 0.0)
    cs_f = _dot_split3(m_f, g_log)
    cs_b = _dot_split3(m_b, g_log)
    lane = _iota((tb, LANES), 1)
    gcol = jnp.where(lane < 4, cs_f, jnp.where(lane < 8, cs_b, beta))
    gcol_ref[0] = gcol
    growf_ref[0] = gcol.T[0:8, :]
    for c in range(n_ch):
        grow_ref[0, c] = gcol[c * DN_CHUNK:(c + 1) * DN_CHUNK, :].T[0:8, :]

    cosv = cos_ref[...]
    sinv = sin_ref[...]
    even_lane = (_iota((tb, LANES), 1) & 1) == 0
    for g in range(8):
        ls = slice(g * LANES, (g + 1) * LANES)
        x = rqk_ref[0, :, ls].astype(F32)
        partner = jnp.where(even_lane, pltpu.roll(x, LANES - 1, 1), pltpu.roll(x, 1, 1))
        y = x * cosv + partner * sinv
        if g < 4:
            rq_ref[0, :, ls] = y.astype(BF16)
        else:
            rkt_ref[0, g - 4] = (y * (RET_DK ** -0.5)).T.astype(BF16)


def _prep(p, pab, conv_w, cos_t, sin_t, alog_row, dtb_row, tc):
    n_b, ta, _ = p.shape
    tb = TOK_BLK
    n_blk = ta // tb
    n_ch = tb // DN_CHUNK
    hpb = tb // 16
    n16 = ta // 16
    out_shape = [
        jax.ShapeDtypeStruct((n_b, ta, QK_W), BF16),
        jax.ShapeDtypeStruct((n_b, ta, QK_W), BF16),
        jax.ShapeDtypeStruct((n_b, DN_HEADS, ta // DN_CHUNK, DN_DK, DN_CHUNK), BF16),
        jax.ShapeDtypeStruct((n_b, ta, BRANCH_W), BF16),
        jax.ShapeDtypeStruct((n_b, ta, QK_W), BF16),
        jax.ShapeDtypeStruct((n_b, RET_HEADS, RET_DK, ta), BF16),
        jax.ShapeDtypeStruct((n_b, ta, LANES), F32),
        jax.ShapeDtypeStruct((n_b, ta // DN_CHUNK, 8, DN_CHUNK), F32),
        jax.ShapeDtypeStruct((n_b, DN_HEADS, DN_DK, ta), BF16),
        jax.ShapeDtypeStruct((n_b, 8, ta), F32),
    ]
    out_specs = [
        pl.BlockSpec((1, tb, QK_W), lambda b, i: (b, i, 0)),
        pl.BlockSpec((1, tb, QK_W), lambda b, i: (b, i, 0)),
        pl.BlockSpec((1, DN_HEADS, n_ch, DN_DK, DN_CHUNK), lambda b, i: (b, 0, i, 0, 0)),
        pl.BlockSpec((1, tb, BRANCH_W), lambda b, i: (b, i, 0)),
        pl.BlockSpec((1, tb, QK_W), lambda b, i: (b, i, 0)),
        pl.BlockSpec((1, RET_HEADS, RET_DK, tb), lambda b, i: (b, 0, 0, i)),
        pl.BlockSpec((1, tb, LANES), lambda b, i: (b, i, 0)),
        pl.BlockSpec((1, n_ch, 8, DN_CHUNK), lambda b, i: (b, i, 0, 0)),
        pl.BlockSpec((1, DN_HEADS, DN_DK, tb), lambda b, i: (b, 0, 0, i)),
        pl.BlockSpec((1, 8, tb), lambda b, i: (b, 0, i)),
    ]
    in_specs = [
        pl.BlockSpec((1, tb, DN_CONV_CH), lambda b, i: (b, i, 0)),
        pl.BlockSpec((1, 16, DN_CONV_CH), lambda b, i: (b, jnp.maximum(i * hpb - 1, 0), 0)),
        pl.BlockSpec((1, 16, DN_CONV_CH), lambda b, i: (b, jnp.minimum((i + 1) * hpb, n16 - 1), 0)),
        pl.BlockSpec((1, tb, BRANCH_W), lambda b, i: (b, i, C_RQ // BRANCH_W)),
        pl.BlockSpec((1, tb, LANES), lambda b, i: (b, i, 0)),
        pl.BlockSpec((CONV_W, DN_CONV_CH), lambda b, i: (0, 0)),
        pl.BlockSpec((tb, LANES), lambda b, i: (i, 0)),
        pl.BlockSpec((tb, LANES), lambda b, i: (i, 0)),
        pl.BlockSpec((1, LANES), lambda b, i: (0, 0)),
        pl.BlockSpec((1, LANES), lambda b, i: (0, 0)),
    ]
    return pl.pallas_call(
        functools.partial(_prep_kernel, tc=tc, n_blk=n_blk),
        grid=(n_b, n_blk),
        in_specs=in_specs,
        out_specs=out_specs,
        out_shape=out_shape,
        scratch_shapes=[pltpu.VMEM((tb + 16, LANES), F32)],
        compiler_params=_cparams(("arbitrary", "arbitrary")),
        name="prep",
    )(p, p, p, p, pab, conv_w, cos_t, sin_t, alog_row, dtb_row)


def _blk_index(n, n_blk, nctx, reverse):
    if not reverse:
        return n
    return jnp.where(n < nctx, nctx - 1 - n, n_blk - 1 - (n - nctx))


DN_MASK_NEG, DN_MASK_STRICT, DN_MASK_B8, DN_MASK_EYE, DN_MASK_OFF0 = 0, 1, 2, 3, 4
DN_MERGE_SIZES = (8, 16, 32)


DN_PAIR = 2 * DN_CHUNK


def _dn_masks(reverse):
    r = np.arange(DN_PAIR)[:, None]
    c = np.arange(DN_PAIR)[None, :]
    same = (r // DN_CHUNK) == (c // DN_CHUNK)
    incl = same & ((r <= c) if reverse else (r >= c))
    strict = incl & (r != c)
    ms = [np.where(incl, 0.0, NEG_BIG), strict, (r // 8) == (c // 8), r == c]
    for s in DN_MERGE_SIZES:
        ms.append(((r // (2 * s)) == (c // (2 * s))) & ((r // s) != (c // s)))
    return jnp.asarray(np.stack([np.asarray(m, np.float32) for m in ms]))


def _dn_streams(n, q_ref, k_ref, ktf_ref, ktc_ref, v_ref, gcol_ref, growf_ref, growc_ref, m_ref, o_ref, s_ref,
                wq_s, u_s, attn_s, kdt_s, egl_s, reverse):
    c_len = DN_CHUNK
    n_ch = TOK_BLK // c_len
    d_off = 4 if reverse else 0
    heads = range(DN_HEADS)
    last = 0 if reverse else c_len - 1
    pairs = [slice(g * DN_PAIR, (g + 1) * DN_PAIR) for g in range(TOK_BLK // DN_PAIR)]
    units = [(h, g) for h in heads for g in range(len(pairs))]

    wr = lax.rem(n, 2)
    rd = 1 - wr

    @pl.when(n == 0)
    def _():
        s_ref[...] = jnp.zeros_like(s_ref)
        wq_s[...] = jnp.zeros_like(wq_s)
        u_s[...] = jnp.zeros_like(u_s)
        attn_s[...] = jnp.zeros_like(attn_s)
        kdt_s[...] = jnp.zeros_like(kdt_s)
        egl_s[...] = jnp.zeros_like(egl_s)

    us = range(len(units))
    pa = {}

    def a_load():
        gcol = gcol_ref[0]
        growf = growf_ref[0]
        pa["gc"] = [gcol[:, d_off + h:d_off + h + 1] for h in heads]
        pa["bc"] = [gcol[:, 8 + d_off + h:9 + d_off + h] for h in heads]
        pa["egc"] = [jnp.exp(x) for x in pa["gc"]]
        pa["kb"] = [k_ref[0, :, h * DN_DK:(h + 1) * DN_DK].astype(F32) * pa["bc"][h] for h in heads]
        dec = [jnp.exp((pa["gc"][h][pairs[g]] - growf[d_off + h:d_off + h + 1, pairs[g]]) + m_ref[DN_MASK_NEG])
               for h, g in units]
        pa["a"] = [_dot(pa["kb"][h][pairs[g]], ktf_ref[0, h, :, pairs[g]]) * dec[u] * m_ref[DN_MASK_STRICT]
                   for u, (h, g) in enumerate(units)]

    def a_sq():
        pa["ab"] = [x.astype(BF16) for x in pa["a"]]
        pa["d"] = [(pa["a"][u] * m_ref[DN_MASK_B8]).astype(BF16) for u in us]
        pa["d2"] = [jnp.dot(pa["d"][u], pa["d"][u], preferred_element_type=F32) for u in us]

    def a_pow():
        d2b = [x.astype(BF16) for x in pa["d2"]]
        pa["d4"] = [jnp.dot(d2b[u], d2b[u], preferred_element_type=F32) for u in us]
        pa["d3"] = [jnp.dot(pa["d"][u], d2b[u], preferred_element_type=F32) for u in us]

    def a_base():
        p1 = [m_ref[DN_MASK_EYE] - pa["d"][u].astype(F32) + pa["d2"][u] - pa["d3"][u] for u in us]
        pa["tb"] = [(p1[u] + _dot(p1[u], pa["d4"][u])).astype(BF16) for u in us]

    def a_merge_x(lvl):
        def run():
            off = m_ref[DN_MASK_OFF0 + lvl].astype(BF16)
            pa["x"] = [jnp.dot(pa["ab"][u] * off, pa["tb"][u], preferred_element_type=F32) for u in us]
        return run

    def a_merge_t():
        pa["tb"] = [(pa["tb"][u].astype(F32) - _dot(pa["tb"][u], pa["x"][u])).astype(BF16) for u in us]

    def a_store():
        neg64 = m_ref[DN_MASK_NEG, 0:c_len, 0:c_len]
        for u, (h, g) in enumerate(units):
            rows = pairs[g]
            rhs = jnp.concatenate([pa["kb"][h][rows] * pa["egc"][h][rows],
                                   v_ref[0, rows, h * DN_DV:(h + 1) * DN_DV].astype(F32) * pa["bc"][h][rows]], axis=1)
            wu = _dot(pa["tb"][u], rhs)
            qd = q_ref[0, rows, h * DN_DK:(h + 1) * DN_DK].astype(F32) * pa["egc"][h][rows]
            u_s[wr, h, rows, :] = wu[:, DN_DK:DN_DK + DN_DV]
            for ci in range(DN_PAIR // c_len):
                c = g * (DN_PAIR // c_len) + ci
                cs = slice(ci * c_len, (ci + 1) * c_len)
                wq_s[wr, h, c] = jnp.concatenate([wu[cs, 0:DN_DK], qd[cs]], axis=0).astype(BF16)
        for c in range(n_ch):
            rs = slice(c * c_len, (c + 1) * c_len)
            growc = growc_ref[0, c]
            for h in heads:
                gr = growc[d_off + h:d_off + h + 1, :]
                gl = gr[:, last:last + 1]
                kt = ktc_ref[0, h, c]
                attn = _dot(q_ref[0, rs, h * DN_DK:(h + 1) * DN_DK], kt) * jnp.exp((pa["gc"][h][rs] - gr) + neg64)
                attn_s[wr, h, c] = attn.astype(BF16)
                kdt_s[wr, h, c] = (kt.astype(F32) * jnp.exp(gl - gr)).astype(BF16)
                egl_s[wr, h, c] = jnp.broadcast_to(jnp.exp(gl), (8, LANES))

    pb = {"state": [s_ref[h] for h in heads]}

    def b_first(c):
        def run():
            pb["ws"] = [jnp.dot(wq_s[rd, h, c], pb["state"][h].astype(BF16), preferred_element_type=F32)
                        for h in heads]
        return run

    def b_second(c):
        def run():
            rs = slice(c * c_len, (c + 1) * c_len)
            vn = [(u_s[rd, h, rs, :] - pb["ws"][h][0:c_len]).astype(BF16) for h in heads]
            o = [pb["ws"][h][c_len:2 * c_len] + jnp.dot(attn_s[rd, h, c], vn[h], preferred_element_type=F32)
                 for h in heads]
            pb["state"] = [pb["state"][h] * egl_s[rd, h, c][0:1, 0:1]
                           + jnp.dot(kdt_s[rd, h, c], vn[h], preferred_element_type=F32) for h in heads]
            for h in heads:
                o_ref[0, rs, h * DN_DV:(h + 1) * DN_DV] = o[h].astype(BF16)
        return run

    a_stages = [a_load, a_sq, a_pow, a_base]
    for lvl in range(len(DN_MERGE_SIZES)):
        a_stages += [a_merge_x(lvl), a_merge_t]
    a_stages.append(a_store)
    b_stages = []
    for cc in range(n_ch):
        c = (n_ch - 1 - cc) if reverse else cc
        b_stages += [b_first(c), b_second(c)]

    def finish():
        for h in heads:
            s_ref[h] = pb["state"][h]

    return a_stages, b_stages, finish


DN_N_IN = 9
DN_N_SCRATCH = 6


def _dn_kernel(*refs):
    n = pl.program_id(1)
    ins_f, ins_b = refs[0:DN_N_IN], refs[DN_N_IN:2 * DN_N_IN]
    o_f, o_b = refs[2 * DN_N_IN], refs[2 * DN_N_IN + 1]
    scr = refs[2 * DN_N_IN + 2:]
    streams = [_dn_streams(n, *ins_f, o_f, *scr[0:DN_N_SCRATCH], reverse=False),
               _dn_streams(n, *ins_b, o_b, *scr[DN_N_SCRATCH:], reverse=True)]
    n_stage = max(max(len(a), len(b)) for a, b, _ in streams)
    for i in range(n_stage):
        for a_stages, _, _ in streams:
            if i < len(a_stages):
                a_stages[i]()
        for _, b_stages, _ in streams:
            if i < len(b_stages):
                b_stages[i]()
    for _, _, finish in streams:
        finish()


def _dn_scan(dq, dk, dktf, dktc, dv, gcol, growf, growc, tc):
    n_b, ta, _ = dq.shape
    tb = TOK_BLK
    n_blk = ta // tb
    nctx = tc // tb
    n_ch = tb // DN_CHUNK
    in_specs, args, out_specs, scratch = [], [], [], []
    for reverse in (False, True):
        masks = _dn_masks(reverse)
        blk = functools.partial(_blk_index, n_blk=n_blk, nctx=nctx, reverse=reverse)
        bi = lambda n, blk=blk: blk(jnp.minimum(n, n_blk - 1))
        bo = lambda n, blk=blk: blk(jnp.maximum(n - 1, 0))
        in_specs += [pl.BlockSpec((1, tb, QK_W), lambda b, n, bi=bi: (b, bi(n), 0)),
                     pl.BlockSpec((1, tb, QK_W), lambda b, n, bi=bi: (b, bi(n), 0)),
                     pl.BlockSpec((1, DN_HEADS, DN_DK, tb), lambda b, n, bi=bi: (b, 0, 0, bi(n))),
                     pl.BlockSpec((1, DN_HEADS, n_ch, DN_DK, DN_CHUNK), lambda b, n, bi=bi: (b, 0, bi(n), 0, 0)),
                     pl.BlockSpec((1, tb, BRANCH_W), lambda b, n, bi=bi: (b, bi(n), 0)),
                     pl.BlockSpec((1, tb, LANES), lambda b, n, bi=bi: (b, bi(n), 0)),
                     pl.BlockSpec((1, 8, tb), lambda b, n, bi=bi: (b, 0, bi(n))),
                     pl.BlockSpec((1, n_ch, 8, DN_CHUNK), lambda b, n, bi=bi: (b, bi(n), 0, 0)),
                     pl.BlockSpec(masks.shape, lambda b, n: (0, 0, 0))]
        args += [dq, dk, dktf, dktc, dv, gcol, growf, growc, masks]
        out_specs.append(pl.BlockSpec((1, tb, BRANCH_W), lambda b, n, bo=bo: (b, bo(n), 0)))
        scratch += [pltpu.VMEM((DN_HEADS, DN_DK, DN_DV), F32),
                    pltpu.VMEM((2, DN_HEADS, n_ch, 2 * DN_CHUNK, DN_DK), BF16),
                    pltpu.VMEM((2, DN_HEADS, tb, DN_DV), F32),
                    pltpu.VMEM((2, DN_HEADS, n_ch, DN_CHUNK, DN_CHUNK), BF16),
                    pltpu.VMEM((2, DN_HEADS, n_ch, DN_DK, DN_CHUNK), BF16),
                    pltpu.VMEM((2, DN_HEADS, n_ch, 8, LANES), F32)]
    assert len(args) == 2 * DN_N_IN and len(scratch) == 2 * DN_N_SCRATCH
    return pl.pallas_call(
        _dn_kernel,
        grid=(n_b, n_blk + 1),
        in_specs=in_specs,
        out_specs=out_specs,
        out_shape=[jax.ShapeDtypeStruct((n_b, ta, BRANCH_W), BF16)] * 2,
        scratch_shapes=scratch,
        compiler_params=_cparams(("arbitrary", "arbitrary")),
        name="dn_scan",
    )(*args)


def _ret_kernel(qf_ref, ktf_ref, vf_ref, qb_ref, ktb_ref, vb_ref, lam_ref, of_ref, ob_ref,
                s_ref, dec_ref, eq_ref, ek_ref, egl_ref):
    n = pl.program_id(1)
    c_len = TOK_BLK
    dirs = ((0, qf_ref, ktf_ref, vf_ref, of_ref), (1, qb_ref, ktb_ref, vb_ref, ob_ref))
    streams = [(d, h) for d in range(2) for h in range(RET_HEADS)]

    @pl.when(n == 0)
    def _():
        s_ref[...] = jnp.zeros_like(s_ref)
        ri = _iota((c_len, c_len), 0)
        ci = _iota((c_len, c_len), 1)
        rowpos = _iota((c_len, LANES), 0)
        colpos = _iota((RET_DK, c_len), 1)
        for d, h in streams:
            reverse = d == 1
            dist = (ci - ri) if reverse else (ri - ci)
            if reverse:
                qexp = (c_len - rowpos).astype(F32)
                kexp = colpos.astype(F32)
            else:
                qexp = (rowpos + 1).astype(F32)
                kexp = (c_len - 1 - colpos).astype(F32)
            x = lam_ref[0:1, 4 * d + h:4 * d + h + 1]
            lam = jnp.minimum(x, 0.0) - jnp.log(1.0 + jnp.exp(-jnp.abs(x)))
            dec_ref[d, h] = jnp.exp(jnp.where(dist >= 0, dist.astype(F32) * lam, NEG_BIG))
            eq_ref[d, h] = jnp.exp(qexp * lam)
            ek_ref[d, h] = jnp.exp(kexp * lam)
            egl_ref[d, h] = jnp.broadcast_to(jnp.exp(lam * float(c_len)), (8, LANES))

    def q_of(d, h):
        return dirs[d][1][0, :, h * RET_DK:(h + 1) * RET_DK]

    def v_of(d, h):
        return dirs[d][3][0, :, h * RET_DV:(h + 1) * RET_DV]

    s_old = {u: s_ref[u[0], u[1]] for u in streams}
    qk = {u: _dot(q_of(*u), dirs[u[0]][2][0, u[1]]) for u in streams}
    kv = {u: _dot(dirs[u[0]][2][0, u[1]].astype(F32) * ek_ref[u[0], u[1]], v_of(*u)) for u in streams}
    qs = {u: _dot(q_of(*u).astype(F32) * eq_ref[u[0], u[1]], s_old[u]) for u in streams}
    av = {u: _dot(qk[u] * dec_ref[u[0], u[1]], v_of(*u)) for u in streams}
    for u in streams:
        d, h = u
        s_ref[d, h] = s_old[u] * egl_ref[d, h][0:1, 0:1] + kv[u]
        dirs[d][4][0, :, h * RET_DV:(h + 1) * RET_DV] = (av[u] + qs[u]).astype(BF16)


def _ret_scan(rq, rkt, p, lam_row, tc):
    n_b, ta, _ = rq.shape
    tb = TOK_BLK
    n_blk = ta // tb
    nctx = tc // tb
    in_specs, out_specs = [], []
    for reverse in (False, True):
        bi = functools.partial(_blk_index, n_blk=n_blk, nctx=nctx, reverse=reverse)
        in_specs += [pl.BlockSpec((1, tb, QK_W), lambda b, n, bi=bi: (b, bi(n), 0)),
                     pl.BlockSpec((1, RET_HEADS, RET_DK, tb), lambda b, n, bi=bi: (b, 0, 0, bi(n))),
                     pl.BlockSpec((1, tb, BRANCH_W), lambda b, n, bi=bi: (b, bi(n), C_RV // BRANCH_W))]
        out_specs.append(pl.BlockSpec((1, tb, BRANCH_W), lambda b, n, bi=bi: (b, bi(n), 0)))
    in_specs.append(pl.BlockSpec((1, LANES), lambda b, n: (0, 0)))
    return pl.pallas_call(
        _ret_kernel,
        grid=(n_b, n_blk),
        in_specs=in_specs,
        out_specs=out_specs,
        out_shape=[jax.ShapeDtypeStruct((n_b, ta, BRANCH_W), BF16)] * 2,
        scratch_shapes=[pltpu.VMEM((2, RET_HEADS, RET_DK, RET_DV), F32),
                        pltpu.VMEM((2, RET_HEADS, tb, tb), F32),
                        pltpu.VMEM((2, RET_HEADS, tb, LANES), F32),
                        pltpu.VMEM((2, RET_HEADS, RET_DK, tb), F32),
                        pltpu.VMEM((2, RET_HEADS, 8, LANES), F32)],
        compiler_params=_cparams(("arbitrary", "arbitrary")),
        name="ret_scan",
    )(rq, rkt, p, rq, rkt, p, lam_row)


HG_LEVELS = (1, 2, 4, 8, 16, 32)


def _neg_abs(x):
    bits = lax.bitcast_convert_type(x, jnp.uint32) | jnp.uint32(0x80000000)
    return lax.bitcast_convert_type(bits, F32)


def _hg_pair_masks(reverse):
    i = np.arange(HG_CHUNK)[:, None]
    j = np.arange(HG_CHUNK)[None, :]
    ms = []
    for s in HG_LEVELS:
        q_half = 0 if reverse else 1
        ms.append(((i // (2 * s)) == (j // (2 * s))) & (((i // s) & 1) == q_half) & (((j // s) & 1) == 1 - q_half))
    ms.append(i == j)
    return jnp.asarray(np.stack(ms).astype(np.float32))


def _hg_boundary(gc_s, gc, base, hs, s, row8, reverse):
    c_len = HG_CHUNK
    off = s - 1 if reverse else s
    if s == 1:
        rowi = _iota((c_len, HG_DK), 0)
        if reverse:
            return jnp.where((rowi & 1) == 1, pltpu.roll(gc, 1, 0), gc)
        return jnp.where((rowi & 1) == 0, pltpu.roll(gc, c_len - 1, 0), gc)
    parts = []
    for vi in range(c_len // 8):
        r0 = base + 8 * vi
        if s == 2:
            lo = jnp.broadcast_to(gc_s[r0 + off:r0 + off + 1, hs], (8, HG_DK))
            hi = jnp.broadcast_to(gc_s[r0 + 4 + off:r0 + 5 + off, hs], (8, HG_DK))
            parts.append(jnp.where(row8 < 4, lo, hi))
        else:
            m = base + ((8 * vi) // (2 * s)) * (2 * s) + off
            parts.append(jnp.broadcast_to(gc_s[m:m + 1, hs], (8, HG_DK)))
    return jnp.concatenate(parts, axis=0)


HG_N_IN = 5
HG_N_SCRATCH = 3


def _hg_kernel(*refs):
    n = pl.program_id(1)
    o_f, o_b = refs[2 * HG_N_IN], refs[2 * HG_N_IN + 1]
    scr = refs[2 * HG_N_IN + 2:]
    _hg_direction(n, *refs[0:HG_N_IN], o_f, *scr[0:HG_N_SCRATCH], reverse=False)
    _hg_direction(n, *refs[HG_N_IN:2 * HG_N_IN], o_b, *scr[HG_N_SCRATCH:], reverse=True)


def _hg_direction(n, xq_ref, xf_ref, xi_ref, low_ref, pm_ref, o_ref, st_ref, k_s, gc_s, *, reverse):
    tb = TOK_BLK
    c_len = HG_CHUNK
    n_ch = tb // c_len

    @pl.when(n == 0)
    def _():
        st_ref[...] = jnp.zeros_like(st_ref)

    low = low_ref[...]
    f = low + (1.0 - low) * xf_ref[0].astype(F32)
    g = jnp.log2(f)
    r = _iota((tb, tb), 0)
    c_ = _iota((tb, tb), 1)
    same = _shr(r, c_len) == _shr(c_, c_len)
    tri = (r <= c_) if reverse else (r >= c_)
    gc_s[...] = _dot_split3(jnp.where(jnp.logical_and(same, tri), 1.0, 0.0), g, pieces=2)
    k_s[...] = (1.0 - f).astype(BF16)

    row8 = _iota((8, HG_DK), 0)
    last = 0 if reverse else c_len - 1
    hsl = [slice(h * HG_DK, (h + 1) * HG_DK) for h in range(HG_HEADS)]
    heads = range(HG_HEADS)

    def scores(c):
        base = c * c_len
        rs = slice(base, base + c_len)
        q = {h: xq_ref[0, rs, hsl[h]].astype(F32) for h in heads}
        k = {h: k_s[rs, hsl[h]].astype(F32) for h in heads}
        gc = {h: gc_s[rs, hsl[h]] for h in heads}
        tiles = range(c_len // 8)
        t8 = lambda x, v: x[8 * v:8 * v + 8]
        diag = pm_ref[len(HG_LEVELS)]
        a = {h: [t8(x, v) for v in tiles] for h, x in ((h, _dot_nt(q[h], k[h]) * diag) for h in heads)}
        q_half = 0 if reverse else 1
        for lvl, s in enumerate(HG_LEVELS):
            pm = pm_ref[lvl]
            if s < 8:
                is_q = (_shr(row8, s) & 1) == q_half
                pick = lambda h, v: jnp.where(is_q, t8(q[h], v), t8(k[h], v))
                q_tiles = list(tiles)
            else:
                pick = lambda h, v: t8(q[h], v) if ((8 * v) // s) & 1 == q_half else t8(k[h], v)
                q_tiles = [v for v in tiles if ((8 * v) // s) & 1 == q_half]
            z = {}
            for h in heads:
                e = jnp.exp2(_neg_abs(gc[h] - _hg_boundary(gc_s, gc[h], base, hsl[h], s, row8, reverse)))
                z[h] = jnp.concatenate([pick(h, v) * t8(e, v) for v in tiles], axis=0)
            pr = {h: _dot_nt(z[h], z[h]) for h in heads}
            for h in heads:
                for v in q_tiles:
                    a[h][v] = a[h][v] + t8(pr[h], v) * t8(pm, v)
        return {h: jnp.concatenate(a[h], axis=0) for h in heads}

    def outputs(c, a):
        base = c * c_len
        rs = slice(base, base + c_len)
        q = {h: xq_ref[0, rs, hsl[h]].astype(F32) for h in heads}
        k = {h: k_s[rs, hsl[h]].astype(F32) for h in heads}
        gc = {h: gc_s[rs, hsl[h]] for h in heads}
        gl = {h: gc[h][last:last + 1, :] for h in heads}
        st = {h: st_ref[h] for h in heads}
        o = {h: _dot(a[h], xi_ref[0, rs, hsl[h]]) + _dot_nt(q[h] * jnp.exp2(gc[h]), st[h]) for h in heads}
        for h in heads:
            kd = k[h] * jnp.exp2(gl[h] - gc[h])
            st_ref[h] = st[h] * jnp.exp2(gl[h]) + _dot_tn(xi_ref[0, rs, hsl[h]], kd)
            o_ref[0, rs, hsl[h]] = o[h].astype(BF16)

    order = [(n_ch - 1 - cc) if reverse else cc for cc in range(n_ch)]
    pending = None
    for c in order:
        a = scores(c)
        if pending is not None:
            outputs(*pending)
        pending = (c, a)
    outputs(*pending)


def _hg_scan(p, low_rows, tc):
    n_b, ta, _ = p.shape
    tb = TOK_BLK
    n_blk = ta // tb
    nctx = tc // tb
    in_specs, args, out_specs, scratch = [], [], [], []
    for d_i, reverse in enumerate((False, True)):
        masks = _hg_pair_masks(reverse)
        bi = functools.partial(_blk_index, n_blk=n_blk, nctx=nctx, reverse=reverse)
        in_specs += [pl.BlockSpec((1, tb, BRANCH_W), lambda b, n, bi=bi: (b, bi(n), C_HQ // BRANCH_W)),
                     pl.BlockSpec((1, tb, BRANCH_W), lambda b, n, bi=bi, d_i=d_i: (b, bi(n), C_HF // BRANCH_W + d_i)),
                     pl.BlockSpec((1, tb, BRANCH_W), lambda b, n, bi=bi: (b, bi(n), C_HI // BRANCH_W)),
                     pl.BlockSpec((1, BRANCH_W), lambda b, n: (0, 0)),
                     pl.BlockSpec(masks.shape, lambda b, n: (0, 0, 0))]
        args += [p, p, p, low_rows[d_i], masks]
        out_specs.append(pl.BlockSpec((1, tb, BRANCH_W), lambda b, n, bi=bi: (b, bi(n), 0)))
        scratch += [pltpu.VMEM((HG_HEADS, HG_DV, HG_DK), F32),
                    pltpu.VMEM((tb, BRANCH_W), BF16),
                    pltpu.VMEM((tb, BRANCH_W), F32)]
    assert len(args) == 2 * HG_N_IN and len(scratch) == 2 * HG_N_SCRATCH
    return pl.pallas_call(
        _hg_kernel,
        grid=(n_b, n_blk),
        in_specs=in_specs,
        out_specs=out_specs,
        out_shape=[jax.ShapeDtypeStruct((n_b, ta, BRANCH_W), BF16)] * 2,
        scratch_shapes=scratch,
        compiler_params=_cparams(("arbitrary", "arbitrary")),
        name="hg_scan",
    )(*args)


def _branch(of_ref, ob_ref, gate_ref, nw_ref, dv):
    o = of_ref[0].astype(F32) + ob_ref[0].astype(F32)
    gate = gate_ref[0].astype(F32)
    nw = nw_ref[...]
    parts = []
    for h in range(BRANCH_W // dv):
        x = o[:, h * dv:(h + 1) * dv]
        y = x * lax.rsqrt(jnp.mean(x * x, axis=-1, keepdims=True) + NORM_EPS) * nw
        parts.append(y)
    return (jnp.concatenate(parts, axis=-1) * gate).astype(BF16)


def _merge_kernel(dnf, dnb, rtf, rtb, hgf, hgb, g_dn, g_rt, g_hg, m0, m1, m2, x_ref, mod_ref,
                  nw_dn, nw_rt, nw_hg, wb_ref, wo_ref, fnw_ref, o_ref, *, tc, blk_off, n_b, final):
    b = pl.program_id(0)
    i = pl.program_id(1) + blk_off
    d = D_MODEL
    brs = (_branch(dnf, dnb, g_dn, nw_dn, DN_DV),
           _branch(rtf, rtb, g_rt, nw_rt, RET_DV),
           _branch(hgf, hgb, g_hg, nw_hg, HG_DV))
    y = None
    for idx, (br, mg) in enumerate(zip(brs, (m0, m1, m2))):
        t = jnp.dot(br, wb_ref[idx], preferred_element_type=F32) * mg[0].astype(F32)
        y = t if y is None else y + t
    out = jnp.dot(y.astype(BF16), wo_ref[...], preferred_element_type=F32)
    is_ctx = (i * TOK_BLK) < tc
    gt = jnp.where(is_ctx, mod_ref[n_b:n_b + 1, 2 * d:3 * d], mod_ref[pl.ds(b, 1), 2 * d:3 * d])
    xn = x_ref[0] + gt * out
    if final:
        xn = xn * lax.rsqrt(jnp.mean(xn * xn, axis=-1, keepdims=True) + NORM_EPS) * fnw_ref[...]
    o_ref[0] = xn


def _merge(outs, p, xa, mod_l, nws, wb, wo, fnw, tc, final):
    n_b, ta, d = xa.shape
    tb = TOK_BLK
    blk_off = tc // tb if final else 0
    n_blk = ta // tb - blk_off
    rows = lambda b, i: (b, i + blk_off, 0)
    pcol = lambda cb: (lambda b, i: (b, i + blk_off, cb))
    const2 = lambda b, i: (0, 0)
    in_specs = ([pl.BlockSpec((1, tb, BRANCH_W), rows)] * 6
                + [pl.BlockSpec((1, tb, BRANCH_W), pcol(C_DNGATE // BRANCH_W)),
                   pl.BlockSpec((1, tb, BRANCH_W), pcol(C_RGATE // BRANCH_W)),
                   pl.BlockSpec((1, tb, BRANCH_W), pcol(C_HGATE // BRANCH_W)),
                   pl.BlockSpec((1, tb, BRANCH_W), pcol(C_MERGE0 // BRANCH_W)),
                   pl.BlockSpec((1, tb, BRANCH_W), pcol(C_MERGE1 // BRANCH_W)),
                   pl.BlockSpec((1, tb, BRANCH_W), pcol(C_MERGE2 // BRANCH_W)),
                   pl.BlockSpec((1, tb, d), rows),
                   pl.BlockSpec((8, 3 * d), const2),
                   pl.BlockSpec((1, DN_DV), const2),
                   pl.BlockSpec((1, RET_DV), const2),
                   pl.BlockSpec((1, HG_DV), const2),
                   pl.BlockSpec((N_BRANCH, BRANCH_W, d), lambda b, i: (0, 0, 0)),
                   pl.BlockSpec((d, d), const2),
                   pl.BlockSpec((1, d), const2)])
    out_rows = ta - blk_off * tb
    return pl.pallas_call(
        functools.partial(_merge_kernel, tc=tc, blk_off=blk_off, n_b=n_b, final=final),
        grid=(n_b, n_blk),
        in_specs=in_specs,
        out_specs=pl.BlockSpec((1, tb, d), lambda b, i: (b, i, 0)),
        out_shape=jax.ShapeDtypeStruct((n_b, out_rows, d), F32),
        compiler_params=_cparams(("arbitrary", "arbitrary")),
        name="merge_final" if final else "merge",
    )(*outs, p, p, p, p, p, p, xa, mod_l, *nws, wb, wo, fnw)


def _permute_w_in(w):
    d = w.shape[0]
    ab0 = DN_CONV_CH + BRANCH_W
    wp = jnp.concatenate([w[:, :ab0], w[:, ab0 + 16:]], axis=1).astype(BF16)
    wab = jnp.pad(w[:, ab0:ab0 + 16], ((0, 0), (0, LANES - 16))).astype(BF16)
    return wp, wab


def _rope_tables(n_lat, tc):
    half = RET_DK // 2
    inv = ROPE_BASE ** (-np.arange(0, half, 2, dtype=np.float64) / half)
    t = np.arange(n_lat)
    row = (t // GRID_W).astype(np.float64)
    col = (t % GRID_W).astype(np.float64)
    ang = np.concatenate([row[:, None] * inv, col[:, None] * inv], axis=-1)
    cos, sin = np.cos(ang), np.sin(ang)
    cos_l = np.repeat(cos, 2, axis=-1)
    sin_l = np.stack([-sin, sin], axis=-1).reshape(n_lat, LANES)
    cos_all = np.concatenate([np.ones((tc, LANES)), cos_l], axis=0).astype(np.float32)
    sin_all = np.concatenate([np.zeros((tc, LANES)), sin_l], axis=0).astype(np.float32)
    return jnp.asarray(cos_all), jnp.asarray(sin_all)


def _pad_row(v, width=LANES):
    v = v.reshape(1, -1).astype(F32)
    return jnp.pad(v, ((0, 0), (0, width - v.shape[1])))


def kernel(x, c, ctx, c_ctx, norm_w, ada_w, ada_b, w_in, dn_conv, dn_a_log, dn_dt_bias, dn_norm_w,
           ret_decay, ret_norm_w, hg_lb, hg_norm_w, w_branch, w_out, final_norm_w):
    n_b, n_lat, d = x.shape
    tc = ctx.shape[1]
    depth = norm_w.shape[0]
    assert d == D_MODEL and tc % TOK_BLK == 0 and n_lat % TOK_BLK == 0 and n_b + 1 <= 8
    assert n_lat % GRID_W == 0

    xa = jnp.concatenate([ctx, x], axis=1).astype(F32)
    cin = jnp.concatenate([c, c_ctx[None, :], jnp.zeros((8 - n_b - 1, d), c.dtype)], axis=0).astype(F32)
    mod = _modulation(cin, ada_w.astype(F32), ada_b.astype(F32))
    lower = _hg_lower(hg_lb)
    cos_t, sin_t = _rope_tables(n_lat, tc)
    fnw = final_norm_w.reshape(1, d).astype(F32)

    for l in range(depth):
        final = l == depth - 1
        wp, wab = _permute_w_in(w_in[l])
        p, pab = _project(xa, mod[l], norm_w[l].reshape(1, d).astype(F32), wp, wab, tc)
        dq, dk, dkt, dv, rq, rkt, gcol, grow, dktf, growf = _prep(
            p, pab, dn_conv[l].astype(F32), cos_t, sin_t, _pad_row(dn_a_log[l]), _pad_row(dn_dt_bias[l]), tc)
        lam_row = _pad_row(ret_decay[l])
        outs = []
        outs += _dn_scan(dq, dk, dktf, dkt, dv, gcol, growf, grow, tc)
        outs += _ret_scan(rq, rkt, p, lam_row, tc)
        outs += _hg_scan(p, (lower[l, 0].reshape(1, -1), lower[l, 1].reshape(1, -1)), tc)
        nws = (dn_norm_w[l].reshape(1, -1).astype(F32), ret_norm_w[l].reshape(1, -1).astype(F32),
               hg_norm_w[l].reshape(1, -1).astype(F32))
        xa = _merge(outs, p, xa, mod[l], nws, w_branch[l].astype(BF16), w_out[l].astype(BF16), fnw, tc, final)
    return xa
```

```python
import functools

import jax
import jax.numpy as jnp
import numpy as np
from jax import lax
from jax.experimental import pallas as pl
from jax.experimental.pallas import tpu as pltpu

F32 = jnp.float32
BF16 = jnp.bfloat16

D_MODEL = 1024
GRID_W = 64
DN_HEADS, DN_DK, DN_DV = 4, 128, 256
RET_HEADS, RET_DK, RET_DV = 4, 128, 256
HG_HEADS, HG_DK, HG_DV = 8, 128, 128
CONV_W = 5
ROPE_BASE = 10000.0
NORM_EPS = 1e-6
N_BRANCH = 3
BRANCH_W = 1024
QK_W = 512
DN_CONV_CH = 2 * QK_W + BRANCH_W

LANES = 128
TOK_BLK = 256
DN_CHUNK = 64
HG_CHUNK = 64
NEG_BIG = -1e30

C_DNQKV = 0
C_DNGATE = 2048
C_RQ = 3072
C_RK = 3584
C_RV = 4096
C_RGATE = 5120
C_HQ = 6144
C_HF = 7168
C_HI = 9216
C_HGATE = 10240
C_MERGE0 = 11264
C_MERGE1 = 12288
C_MERGE2 = 13312
P_WIDTH = 14336
PROJ_TILE = 3584
PROJ_SUB = 512
_PROJ_ACT_SLICES = "NNNN" "SSNN" "NNSS" "SSGG" "GGNN" "SSGG" "GGGG"
PROJ_ACT = tuple(_PROJ_ACT_SLICES[i:i + PROJ_TILE // PROJ_SUB]
                 for i in range(0, len(_PROJ_ACT_SLICES), PROJ_TILE // PROJ_SUB))
VMEM_LIMIT = 56 * 1024 * 1024


def _dot(a, b):
    return jnp.dot(a.astype(BF16), b.astype(BF16), preferred_element_type=F32)


def _dot_nt(a, b):
    return lax.dot_general(a.astype(BF16), b.astype(BF16), (((1,), (1,)), ((), ())),
                           preferred_element_type=F32)


def _dot_tn(a, b):
    return lax.dot_general(a.astype(BF16), b.astype(BF16), (((0,), (0,)), ((), ())),
                           preferred_element_type=F32)


def _dot_split3(m01, x, pieces=3):
    m = m01.astype(BF16)
    acc = None
    for _ in range(pieces):
        xp = x.astype(BF16)
        d = jnp.dot(m, xp, preferred_element_type=F32)
        acc = d if acc is None else acc + d
        x = x - xp.astype(F32)
    return acc


def _sigmoid(x):
    return 0.5 * jnp.tanh(0.5 * x) + 0.5


def _silu(x):
    h = 0.5 * x
    return h + h * jnp.tanh(h)


def _softplus(x):
    return jnp.maximum(x, 0.0) + jnp.log(1.0 + jnp.exp(-jnp.abs(x)))


def _iota(shape, dim):
    return lax.broadcasted_iota(jnp.int32, shape, dim)


def _shr(x, s):
    return lax.shift_right_logical(x, int(s).bit_length() - 1)


def _cparams(sem):
    return pltpu.CompilerParams(dimension_semantics=sem, vmem_limit_bytes=VMEM_LIMIT)


def _mod_kernel(cin_ref, w_ref, b_ref, o_ref):
    s = _silu(cin_ref[...])
    o_ref[0] = _dot(s, w_ref[0]) + b_ref[0]


def _modulation(cin, ada_w, ada_b):
    depth, d, d3 = ada_w.shape
    tn = 1024
    return pl.pallas_call(
        _mod_kernel,
        grid=(depth, d3 // tn),
        in_specs=[pl.BlockSpec((8, d), lambda l, j: (0, 0)),
                  pl.BlockSpec((1, d, tn), lambda l, j: (l, 0, j)),
                  pl.BlockSpec((1, 1, tn), lambda l, j: (l, 0, j))],
        out_specs=pl.BlockSpec((1, 8, tn), lambda l, j: (l, 0, j)),
        out_shape=jax.ShapeDtypeStruct((depth, 8, d3), F32),
        compiler_params=_cparams(("arbitrary", "arbitrary")),
        name="ada_mod",
    )(cin, ada_w, ada_b.reshape(depth, 1, d3))


def _lower_kernel(lb_ref, o_ref, *, depth):
    xs = [lb_ref[l] for l in range(depth)]
    m = xs[0]
    for l in range(1, depth):
        m = jnp.maximum(m, xs[l])
    es = [jnp.exp(x - m) for x in xs]
    tot = es[0]
    for l in range(1, depth):
        tot = tot + es[l]
    sm = [e / tot for e in es]
    acc = sm[0]
    o_ref[0] = acc - sm[0]
    for l in range(1, depth):
        acc = acc + sm[l]
        o_ref[l] = acc - sm[0]


def _hg_lower(hg_lb):
    depth = hg_lb.shape[0]
    lb = hg_lb.astype(F32).reshape(depth, 2, HG_HEADS * HG_DK)
    return pl.pallas_call(
        functools.partial(_lower_kernel, depth=depth),
        out_shape=jax.ShapeDtypeStruct(lb.shape, F32),
        name="hg_lower",
    )(lb)


def _proj_kernel(x_ref, mod_ref, nw_ref, w_ref, wab_ref, p_ref, pab_ref, hb_ref, *, tc, tm, tn, n_b):
    b = pl.program_id(0)
    i = pl.program_id(1)
    j = pl.program_id(2)
    d = D_MODEL

    @pl.when(j == 0)
    def _():
        x = x_ref[0]
        ms = jnp.mean(x * x, axis=-1, keepdims=True)
        y = x * lax.rsqrt(ms + NORM_EPS) * nw_ref[...]
        row = i * tm + _iota((tm, 1), 0)
        is_ctx = row < tc
        sh = jnp.where(is_ctx, mod_ref[n_b:n_b + 1, 0:d], mod_ref[pl.ds(b, 1), 0:d])
        sc = jnp.where(is_ctx, mod_ref[n_b:n_b + 1, d:2 * d], mod_ref[pl.ds(b, 1), d:2 * d])
        hb = (y * (1.0 + sc) + sh).astype(BF16)
        hb_ref[...] = hb
        pab_ref[0] = jnp.dot(hb, wab_ref[...], preferred_element_type=F32)

    sub = PROJ_SUB
    for pattern in sorted(set(PROJ_ACT)):
        tiles = [t for t, pat in enumerate(PROJ_ACT) if pat == pattern]
        cond = j == tiles[0]
        for t in tiles[1:]:
            cond = jnp.logical_or(cond, j == t)

        @pl.when(cond)
        def _(pattern=pattern):
            for s in range(tn // sub):
                r = jnp.dot(hb_ref[...], w_ref[:, s * sub:(s + 1) * sub], preferred_element_type=F32)
                if pattern[s] == "S":
                    r = _silu(r)
                elif pattern[s] == "G":
                    r = _sigmoid(r)
                p_ref[0, :, s * sub:(s + 1) * sub] = r.astype(BF16)


def _row_tile(ta):
    best = 16
    for t in range(16, 1101, 16):
        if ta % t == 0:
            best = t
    return best


def _project(xa, mod_l, nw, wp, wab, tc):
    n_b, ta, d = xa.shape
    tm = _row_tile(ta)
    tn = PROJ_TILE
    assert len(PROJ_ACT) * tn == P_WIDTH and all(len(pat) * PROJ_SUB == tn for pat in PROJ_ACT)
    return pl.pallas_call(
        functools.partial(_proj_kernel, tc=tc, tm=tm, tn=tn, n_b=n_b),
        grid=(n_b, ta // tm, P_WIDTH // tn),
        in_specs=[pl.BlockSpec((1, tm, d), lambda b, i, j: (b, i, 0)),
                  pl.BlockSpec((8, 3 * d), lambda b, i, j: (0, 0)),
                  pl.BlockSpec((1, d), lambda b, i, j: (0, 0)),
                  pl.BlockSpec((d, tn), lambda b, i, j: (0, j)),
                  pl.BlockSpec((d, LANES), lambda b, i, j: (0, 0))],
        out_specs=[pl.BlockSpec((1, tm, tn), lambda b, i, j: (b, i, j)),
                   pl.BlockSpec((1, tm, LANES), lambda b, i, j: (b, i, 0))],
        out_shape=[jax.ShapeDtypeStruct((n_b, ta, P_WIDTH), BF16),
                   jax.ShapeDtypeStruct((n_b, ta, LANES), F32)],
        scratch_shapes=[pltpu.VMEM((tm, d), BF16)],
        compiler_params=_cparams(("arbitrary", "arbitrary", "arbitrary")),
        name="in_proj",
    )(xa, mod_l, nw, wp, wab)


def _prep_kernel(main_ref, prev_ref, next_ref, rqk_ref, pab_ref, cw_ref, cos_ref, sin_ref, alog_ref, dtb_ref,
                 dq_ref, dk_ref, dkt_ref, dv_ref, rq_ref, rkt_ref, gcol_ref, grow_ref, dktf_ref, growf_ref,
                 xs_ref, *, tc, n_blk):
    i = pl.program_id(1)
    tb = TOK_BLK
    nctx = tc // tb
    has_prev = jnp.logical_and(i != 0, i != nctx)
    has_next = jnp.logical_and(i != nctx - 1, i != n_blk - 1)
    pm = jnp.where(has_prev, 1.0, 0.0)
    nm = jnp.where(has_next, 1.0, 0.0)
    n_ch = tb // DN_CHUNK

    for g in range(16):
        ls = slice(g * LANES, (g + 1) * LANES)
        xs_ref[8:8 + tb, :] = main_ref[0, :, ls].astype(F32)
        xs_ref[0:8, :] = prev_ref[0, 8:16, ls].astype(F32) * pm
        xs_ref[8 + tb:16 + tb, :] = next_ref[0, 0:8, ls].astype(F32) * nm
        acc = cw_ref[0:1, ls] * xs_ref[6:6 + tb, :]
        for t in range(1, CONV_W):
            acc = acc + cw_ref[t:t + 1, ls] * xs_ref[6 + t:6 + t + tb, :]
        y = _silu(acc)
        if g < 8:
            y = y * lax.rsqrt(jnp.sum(y * y, axis=-1, keepdims=True) + NORM_EPS)
        if g < 4:
            dq_ref[0, :, ls] = (y * (DN_DK ** -0.5)).astype(BF16)
        elif g < 8:
            h = g - 4
            hs = slice(h * LANES, (h + 1) * LANES)
            dk_ref[0, :, hs] = y.astype(BF16)
            dktf_ref[0, h] = y.T.astype(BF16)
            for c in range(n_ch):
                dkt_ref[0, h, c] = y[c * DN_CHUNK:(c + 1) * DN_CHUNK, :].T.astype(BF16)
        else:
            vs = slice((g - 8) * LANES, (g - 7) * LANES)
            dv_ref[0, :, vs] = y.astype(BF16)

    pab = pab_ref[0]
    g_log = -jnp.exp(alog_ref[...]) * _softplus(pab + dtb_ref[...])
    beta = _sigmoid(pab)
    r = _iota((tb, tb), 0)
    c_ = _iota((tb, tb), 1)
    same = _shr(r, DN_CHUNK) == _shr(c_, DN_CHUNK)
    m_f = jnp.where(jnp.logical_and(same, r >= c_), 1.0, 0.0)
    m_b = jnp.where(jnp.logical_and(same, r <= c_), 1.0, 0.0)
    cs_f = _dot_split3(m_f, g_log)
    cs_b = _dot_split3(m_b, g_log)
    lane = _iota((tb, LANES), 1)
    gcol = jnp.where(lane < 4, cs_f, jnp.where(lane < 8, cs_b, beta))
    gcol_ref[0] = gcol
    growf_ref[0] = gcol.T[0:8, :]
    for c in range(n_ch):
        grow_ref[0, c] = gcol[c * DN_CHUNK:(c + 1) * DN_CHUNK, :].T[0:8, :]

    cosv = cos_ref[...]
    sinv = sin_ref[...]
    even_lane = (_iota((tb, LANES), 1) & 1) == 0
    for g in range(8):
        ls = slice(g * LANES, (g + 1) * LANES)
        x = rqk_ref[0, :, ls].astype(F32)
        partner = jnp.where(even_lane, pltpu.roll(x, LANES - 1, 1), pltpu.roll(x, 1, 1))
        y = x * cosv + partner * sinv
        if g < 4:
            rq_ref[0, :, ls] = y.astype(BF16)
        else:
            rkt_ref[0, g - 4] = (y * (RET_DK ** -0.5)).T.astype(BF16)


def _prep(p, pab, conv_w, cos_t, sin_t, alog_row, dtb_row, tc):
    n_b, ta, _ = p.shape
    tb = TOK_BLK
    n_blk = ta // tb
    n_ch = tb // DN_CHUNK
    hpb = tb // 16
    n16 = ta // 16
    out_shape = [
        jax.ShapeDtypeStruct((n_b, ta, QK_W), BF16),
        jax.ShapeDtypeStruct((n_b, ta, QK_W), BF16),
        jax.ShapeDtypeStruct((n_b, DN_HEADS, ta // DN_CHUNK, DN_DK, DN_CHUNK), BF16),
        jax.ShapeDtypeStruct((n_b, ta, BRANCH_W), BF16),
        jax.ShapeDtypeStruct((n_b, ta, QK_W), BF16),
        jax.ShapeDtypeStruct((n_b, RET_HEADS, RET_DK, ta), BF16),
        jax.ShapeDtypeStruct((n_b, ta, LANES), F32),
        jax.ShapeDtypeStruct((n_b, ta // DN_CHUNK, 8, DN_CHUNK), F32),
        jax.ShapeDtypeStruct((n_b, DN_HEADS, DN_DK, ta), BF16),
        jax.ShapeDtypeStruct((n_b, 8, ta), F32),
    ]
    out_specs = [
        pl.BlockSpec((1, tb, QK_W), lambda b, i: (b, i, 0)),
        pl.BlockSpec((1, tb, QK_W), lambda b, i: (b, i, 0)),
        pl.BlockSpec((1, DN_HEADS, n_ch, DN_DK, DN_CHUNK), lambda b, i: (b, 0, i, 0, 0)),
        pl.BlockSpec((1, tb, BRANCH_W), lambda b, i: (b, i, 0)),
        pl.BlockSpec((1, tb, QK_W), lambda b, i: (b, i, 0)),
        pl.BlockSpec((1, RET_HEADS, RET_DK, tb), lambda b, i: (b, 0, 0, i)),
        pl.BlockSpec((1, tb, LANES), lambda b, i: (b, i, 0)),
        pl.BlockSpec((1, n_ch, 8, DN_CHUNK), lambda b, i: (b, i, 0, 0)),
        pl.BlockSpec((1, DN_HEADS, DN_DK, tb), lambda b, i: (b, 0, 0, i)),
        pl.BlockSpec((1, 8, tb), lambda b, i: (b, 0, i)),
    ]
    in_specs = [
        pl.BlockSpec((1, tb, DN_CONV_CH), lambda b, i: (b, i, 0)),
        pl.BlockSpec((1, 16, DN_CONV_CH), lambda b, i: (b, jnp.maximum(i * hpb - 1, 0), 0)),
        pl.BlockSpec((1, 16, DN_CONV_CH), lambda b, i: (b, jnp.minimum((i + 1) * hpb, n16 - 1), 0)),
        pl.BlockSpec((1, tb, BRANCH_W), lambda b, i: (b, i, C_RQ // BRANCH_W)),
        pl.BlockSpec((1, tb, LANES), lambda b, i: (b, i, 0)),
        pl.BlockSpec((CONV_W, DN_CONV_CH), lambda b, i: (0, 0)),
        pl.BlockSpec((tb, LANES), lambda b, i: (i, 0)),
        pl.BlockSpec((tb, LANES), lambda b, i: (i, 0)),
        pl.BlockSpec((1, LANES), lambda b, i: (0, 0)),
        pl.BlockSpec((1, LANES), lambda b, i: (0, 0)),
    ]
    return pl.pallas_call(
        functools.partial(_prep_kernel, tc=tc, n_blk=n_blk),
        grid=(n_b, n_blk),
        in_specs=in_specs,
        out_specs=out_specs,
        out_shape=out_shape,
        scratch_shapes=[pltpu.VMEM((tb + 16, LANES), F32)],
        compiler_params=_cparams(("arbitrary", "arbitrary")),
        name="prep",
    )(p, p, p, p, pab, conv_w, cos_t, sin_t, alog_row, dtb_row)


def _blk_index(n, n_blk, nctx, reverse):
    if not reverse:
        return n
    return jnp.where(n < nctx, nctx - 1 - n, n_blk - 1 - (n - nctx))


DN_MASK_NEG, DN_MASK_STRICT, DN_MASK_B8, DN_MASK_EYE, DN_MASK_OFF0 = 0, 1, 2, 3, 4
DN_MERGE_SIZES = (8, 16, 32)


DN_PAIR = 2 * DN_CHUNK


def _dn_masks(reverse):
    r = np.arange(DN_PAIR)[:, None]
    c = np.arange(DN_PAIR)[None, :]
    same = (r // DN_CHUNK) == (c // DN_CHUNK)
    incl = same & ((r <= c) if reverse else (r >= c))
    strict = incl & (r != c)
    ms = [np.where(incl, 0.0, NEG_BIG), strict, (r // 8) == (c // 8), r == c]
    for s in DN_MERGE_SIZES:
        ms.append(((r // (2 * s)) == (c // (2 * s))) & ((r // s) != (c // s)))
    return jnp.asarray(np.stack([np.asarray(m, np.float32) for m in ms]))


def _dn_streams(n, q_ref, k_ref, ktf_ref, ktc_ref, v_ref, gcol_ref, growf_ref, growc_ref, m_ref, o_ref, s_ref,
                wq_s, u_s, attn_s, kdt_s, egl_s, reverse):
    c_len = DN_CHUNK
    n_ch = TOK_BLK // c_len
    d_off = 4 if reverse else 0
    heads = range(DN_HEADS)
    last = 0 if reverse else c_len - 1
    pairs = [slice(g * DN_PAIR, (g + 1) * DN_PAIR) for g in range(TOK_BLK // DN_PAIR)]
    units = [(h, g) for h in heads for g in range(len(pairs))]

    wr = lax.rem(n, 2)
    rd = 1 - wr

    @pl.when(n == 0)
    def _():
        s_ref[...] = jnp.zeros_like(s_ref)
        wq_s[...] = jnp.zeros_like(wq_s)
        u_s[...] = jnp.zeros_like(u_s)
        attn_s[...] = jnp.zeros_like(attn_s)
        kdt_s[...] = jnp.zeros_like(kdt_s)
        egl_s[...] = jnp.zeros_like(egl_s)

    us = range(len(units))
    pa = {}

    def a_load():
        gcol = gcol_ref[0]
        growf = growf_ref[0]
        pa["gc"] = [gcol[:, d_off + h:d_off + h + 1] for h in heads]
        pa["bc"] = [gcol[:, 8 + d_off + h:9 + d_off + h] for h in heads]
        pa["egc"] = [jnp.exp(x) for x in pa["gc"]]
        pa["kb"] = [k_ref[0, :, h * DN_DK:(h + 1) * DN_DK].astype(F32) * pa["bc"][h] for h in heads]
        dec = [jnp.exp((pa["gc"][h][pairs[g]] - growf[d_off + h:d_off + h + 1, pairs[g]]) + m_ref[DN_MASK_NEG])
               for h, g in units]
        pa["a"] = [_dot(pa["kb"][h][pairs[g]], ktf_ref[0, h, :, pairs[g]]) * dec[u] * m_ref[DN_MASK_STRICT]
                   for u, (h, g) in enumerate(units)]

    def a_sq():
        pa["ab"] = [x.astype(BF16) for x in pa["a"]]
        pa["d"] = [(pa["a"][u] * m_ref[DN_MASK_B8]).astype(BF16) for u in us]
        pa["d2"] = [jnp.dot(pa["d"][u], pa["d"][u], preferred_element_type=F32) for u in us]

    def a_pow():
        d2b = [x.astype(BF16) for x in pa["d2"]]
        pa["d4"] = [jnp.dot(d2b[u], d2b[u], preferred_element_type=F32) for u in us]
        pa["d3"] = [jnp.dot(pa["d"][u], d2b[u], preferred_element_type=F32) for u in us]

    def a_base():
        p1 = [m_ref[DN_MASK_EYE] - pa["d"][u].astype(F32) + pa["d2"][u] - pa["d3"][u] for u in us]
        pa["tb"] = [(p1[u] + _dot(p1[u], pa["d4"][u])).astype(BF16) for u in us]

    def a_merge_x(lvl):
        def run():
            off = m_ref[DN_MASK_OFF0 + lvl].astype(BF16)
            pa["x"] = [jnp.dot(pa["ab"][u] * off, pa["tb"][u], preferred_element_type=F32) for u in us]
        return run

    def a_merge_t():
        pa["tb"] = [(pa["tb"][u].astype(F32) - _dot(pa["tb"][u], pa["x"][u])).astype(BF16) for u in us]

    def a_store():
        neg64 = m_ref[DN_MASK_NEG, 0:c_len, 0:c_len]
        for u, (h, g) in enumerate(units):
            rows = pairs[g]
            rhs = jnp.concatenate([pa["kb"][h][rows] * pa["egc"][h][rows],
                                   v_ref[0, rows, h * DN_DV:(h + 1) * DN_DV].astype(F32) * pa["bc"][h][rows]], axis=1)
            wu = _dot(pa["tb"][u], rhs)
            qd = q_ref[0, rows, h * DN_DK:(h + 1) * DN_DK].astype(F32) * pa["egc"][h][rows]
            u_s[wr, h, rows, :] = wu[:, DN_DK:DN_DK + DN_DV]
            for ci in range(DN_PAIR // c_len):
                c = g * (DN_PAIR // c_len) + ci
                cs = slice(ci * c_len, (ci + 1) * c_len)
                wq_s[wr, h, c] = jnp.concatenate([wu[cs, 0:DN_DK], qd[cs]], axis=0).astype(BF16)
        for c in range(n_ch):
            rs = slice(c * c_len, (c + 1) * c_len)
            growc = growc_ref[0, c]
            for h in heads:
                gr = growc[d_off + h:d_off + h + 1, :]
                gl = gr[:, last:last + 1]
                kt = ktc_ref[0, h, c]
                attn = _dot(q_ref[0, rs, h * DN_DK:(h + 1) * DN_DK], kt) * jnp.exp((pa["gc"][h][rs] - gr) + neg64)
                attn_s[wr, h, c] = attn.astype(BF16)
                kdt_s[wr, h, c] = (kt.astype(F32) * jnp.exp(gl - gr)).astype(BF16)
                egl_s[wr, h, c] = jnp.broadcast_to(jnp.exp(gl), (8, LANES))

    pb = {"state": [s_ref[h] for h in heads]}

    def b_first(c):
        def run():
            pb["ws"] = [jnp.dot(wq_s[rd, h, c], pb["state"][h].astype(BF16), preferred_element_type=F32)
                        for h in heads]
        return run

    def b_second(c):
        def run():
            rs = slice(c * c_len, (c + 1) * c_len)
            vn = [(u_s[rd, h, rs, :] - pb["ws"][h][0:c_len]).astype(BF16) for h in heads]
            o = [pb["ws"][h][c_len:2 * c_len] + jnp.dot(attn_s[rd, h, c], vn[h], preferred_element_type=F32)
                 for h in heads]
            pb["state"] = [pb["state"][h] * egl_s[rd, h, c][0:1, 0:1]
                           + jnp.dot(kdt_s[rd, h, c], vn[h], preferred_element_type=F32) for h in heads]
            for h in heads:
                o_ref[0, rs, h * DN_DV:(h + 1) * DN_DV] = o[h].astype(BF16)
        return run

    a_stages = [a_load, a_sq, a_pow, a_base]
    for lvl in range(len(DN_MERGE_SIZES)):
        a_stages += [a_merge_x(lvl), a_merge_t]
    a_stages.append(a_store)
    b_stages = []
    for cc in range(n_ch):
        c = (n_ch - 1 - cc) if reverse else cc
        b_stages += [b_first(c), b_second(c)]

    def finish():
        for h in heads:
            s_ref[h] = pb["state"][h]

    return a_stages, b_stages, finish


DN_N_IN = 9
DN_N_SCRATCH = 6


def _dn_kernel(*refs):
    n = pl.program_id(1)
    ins_f, ins_b = refs[0:DN_N_IN], refs[DN_N_IN:2 * DN_N_IN]
    o_f, o_b = refs[2 * DN_N_IN], refs[2 * DN_N_IN + 1]
    scr = refs[2 * DN_N_IN + 2:]
    streams = [_dn_streams(n, *ins_f, o_f, *scr[0:DN_N_SCRATCH], reverse=False),
               _dn_streams(n, *ins_b, o_b, *scr[DN_N_SCRATCH:], reverse=True)]
    n_stage = max(max(len(a), len(b)) for a, b, _ in streams)
    for i in range(n_stage):
        for a_stages, b_stages, _ in streams:
            if i < len(a_stages):
                a_stages[i]()
            if i < len(b_stages):
                b_stages[i]()
    for _, _, finish in streams:
        finish()


def _dn_scan(dq, dk, dktf, dktc, dv, gcol, growf, growc, tc):
    n_b, ta, _ = dq.shape
    tb = TOK_BLK
    n_blk = ta // tb
    nctx = tc // tb
    n_ch = tb // DN_CHUNK
    in_specs, args, out_specs, scratch = [], [], [], []
    for reverse in (False, True):
        masks = _dn_masks(reverse)
        blk = functools.partial(_blk_index, n_blk=n_blk, nctx=nctx, reverse=reverse)
        bi = lambda n, blk=blk: blk(jnp.minimum(n, n_blk - 1))
        bo = lambda n, blk=blk: blk(jnp.maximum(n - 1, 0))
        in_specs += [pl.BlockSpec((1, tb, QK_W), lambda b, n, bi=bi: (b, bi(n), 0)),
                     pl.BlockSpec((1, tb, QK_W), lambda b, n, bi=bi: (b, bi(n), 0)),
                     pl.BlockSpec((1, DN_HEADS, DN_DK, tb), lambda b, n, bi=bi: (b, 0, 0, bi(n))),
                     pl.BlockSpec((1, DN_HEADS, n_ch, DN_DK, DN_CHUNK), lambda b, n, bi=bi: (b, 0, bi(n), 0, 0)),
                     pl.BlockSpec((1, tb, BRANCH_W), lambda b, n, bi=bi: (b, bi(n), 0)),
                     pl.BlockSpec((1, tb, LANES), lambda b, n, bi=bi: (b, bi(n), 0)),
                     pl.BlockSpec((1, 8, tb), lambda b, n, bi=bi: (b, 0, bi(n))),
                     pl.BlockSpec((1, n_ch, 8, DN_CHUNK), lambda b, n, bi=bi: (b, bi(n), 0, 0)),
                     pl.BlockSpec(masks.shape, lambda b, n: (0, 0, 0))]
        args += [dq, dk, dktf, dktc, dv, gcol, growf, growc, masks]
        out_specs.append(pl.BlockSpec((1, tb, BRANCH_W), lambda b, n, bo=bo: (b, bo(n), 0)))
        scratch += [pltpu.VMEM((DN_HEADS, DN_DK, DN_DV), F32),
                    pltpu.VMEM((2, DN_HEADS, n_ch, 2 * DN_CHUNK, DN_DK), BF16),
                    pltpu.VMEM((2, DN_HEADS, tb, DN_DV), F32),
                    pltpu.VMEM((2, DN_HEADS, n_ch, DN_CHUNK, DN_CHUNK), BF16),
                    pltpu.VMEM((2, DN_HEADS, n_ch, DN_DK, DN_CHUNK), BF16),
                    pltpu.VMEM((2, DN_HEADS, n_ch, 8, LANES), F32)]
    assert len(args) == 2 * DN_N_IN and len(scratch) == 2 * DN_N_SCRATCH
    return pl.pallas_call(
        _dn_kernel,
        grid=(n_b, n_blk + 1),
        in_specs=in_specs,
        out_specs=out_specs,
        out_shape=[jax.ShapeDtypeStruct((n_b, ta, BRANCH_W), BF16)] * 2,
        scratch_shapes=scratch,
        compiler_params=_cparams(("arbitrary", "arbitrary")),
        name="dn_scan",
    )(*args)


def _ret_kernel(qf_ref, ktf_ref, vf_ref, qb_ref, ktb_ref, vb_ref, lam_ref, of_ref, ob_ref,
                s_ref, dec_ref, eq_ref, ek_ref, egl_ref):
    n = pl.program_id(1)
    c_len = TOK_BLK
    dirs = ((0, qf_ref, ktf_ref, vf_ref, of_ref), (1, qb_ref, ktb_ref, vb_ref, ob_ref))
    streams = [(d, h) for d in range(2) for h in range(RET_HEADS)]

    @pl.when(n == 0)
    def _():
        s_ref[...] = jnp.zeros_like(s_ref)
        ri = _iota((c_len, c_len), 0)
        ci = _iota((c_len, c_len), 1)
        rowpos = _iota((c_len, LANES), 0)
        colpos = _iota((RET_DK, c_len), 1)
        for d, h in streams:
            reverse = d == 1
            dist = (ci - ri) if reverse else (ri - ci)
            if reverse:
                qexp = (c_len - rowpos).astype(F32)
                kexp = colpos.astype(F32)
            else:
                qexp = (rowpos + 1).astype(F32)
                kexp = (c_len - 1 - colpos).astype(F32)
            x = lam_ref[0:1, 4 * d + h:4 * d + h + 1]
            lam = jnp.minimum(x, 0.0) - jnp.log(1.0 + jnp.exp(-jnp.abs(x)))
            dec_ref[d, h] = jnp.exp(jnp.where(dist >= 0, dist.astype(F32) * lam, NEG_BIG))
            eq_ref[d, h] = jnp.exp(qexp * lam)
            ek_ref[d, h] = jnp.exp(kexp * lam)
            egl_ref[d, h] = jnp.broadcast_to(jnp.exp(lam * float(c_len)), (8, LANES))

    def q_of(d, h):
        return dirs[d][1][0, :, h * RET_DK:(h + 1) * RET_DK]

    def v_of(d, h):
        return dirs[d][3][0, :, h * RET_DV:(h + 1) * RET_DV]

    s_old = {u: s_ref[u[0], u[1]] for u in streams}
    qk = {u: _dot(q_of(*u), dirs[u[0]][2][0, u[1]]) for u in streams}
    kv = {u: _dot(dirs[u[0]][2][0, u[1]].astype(F32) * ek_ref[u[0], u[1]], v_of(*u)) for u in streams}
    qs = {u: _dot(q_of(*u).astype(F32) * eq_ref[u[0], u[1]], s_old[u]) for u in streams}
    av = {u: _dot(qk[u] * dec_ref[u[0], u[1]], v_of(*u)) for u in streams}
    for u in streams:
        d, h = u
        s_ref[d, h] = s_old[u] * egl_ref[d, h][0:1, 0:1] + kv[u]
        dirs[d][4][0, :, h * RET_DV:(h + 1) * RET_DV] = (av[u] + qs[u]).astype(BF16)


def _ret_scan(rq, rkt, p, lam_row, tc):
    n_b, ta, _ = rq.shape
    tb = TOK_BLK
    n_blk = ta // tb
    nctx = tc // tb
    in_specs, out_specs = [], []
    for reverse in (False, True):
        bi = functools.partial(_blk_index, n_blk=n_blk, nctx=nctx, reverse=reverse)
        in_specs += [pl.BlockSpec((1, tb, QK_W), lambda b, n, bi=bi: (b, bi(n), 0)),
                     pl.BlockSpec((1, RET_HEADS, RET_DK, tb), lambda b, n, bi=bi: (b, 0, 0, bi(n))),
                     pl.BlockSpec((1, tb, BRANCH_W), lambda b, n, bi=bi: (b, bi(n), C_RV // BRANCH_W))]
        out_specs.append(pl.BlockSpec((1, tb, BRANCH_W), lambda b, n, bi=bi: (b, bi(n), 0)))
    in_specs.append(pl.BlockSpec((1, LANES), lambda b, n: (0, 0)))
    return pl.pallas_call(
        _ret_kernel,
        grid=(n_b, n_blk),
        in_specs=in_specs,
        out_specs=out_specs,
        out_shape=[jax.ShapeDtypeStruct((n_b, ta, BRANCH_W), BF16)] * 2,
        scratch_shapes=[pltpu.VMEM((2, RET_HEADS, RET_DK, RET_DV), F32),
                        pltpu.VMEM((2, RET_HEADS, tb, tb), F32),
                        pltpu.VMEM((2, RET_HEADS, tb, LANES), F32),
                        pltpu.VMEM((2, RET_HEADS, RET_DK, tb), F32),
                        pltpu.VMEM((2, RET_HEADS, 8, LANES), F32)],
        compiler_params=_cparams(("arbitrary", "arbitrary")),
        name="ret_scan",
    )(rq, rkt, p, rq, rkt, p, lam_row)


HG_LEVELS = (1, 2, 4, 8, 16, 32)


def _neg_abs(x):
    bits = lax.bitcast_convert_type(x, jnp.uint32) | jnp.uint32(0x80000000)
    return lax.bitcast_convert_type(bits, F32)


def _hg_pair_masks(reverse):
    i = np.arange(HG_CHUNK)[:, None]
    j = np.arange(HG_CHUNK)[None, :]
    ms = []
    for s in HG_LEVELS:
        q_half = 0 if reverse else 1
        ms.append(((i // (2 * s)) == (j // (2 * s))) & (((i // s) & 1) == q_half) & (((j // s) & 1) == 1 - q_half))
    ms.append(i == j)
    return jnp.asarray(np.stack(ms).astype(np.float32))


def _hg_boundary(gc_s, gc, base, hs, s, row8, reverse):
    c_len = HG_CHUNK
    off = s - 1 if reverse else s
    if s == 1:
        rowi = _iota((c_len, HG_DK), 0)
        if reverse:
            return jnp.where((rowi & 1) == 1, pltpu.roll(gc, 1, 0), gc)
        return jnp.where((rowi & 1) == 0, pltpu.roll(gc, c_len - 1, 0), gc)
    parts = []
    for vi in range(c_len // 8):
        r0 = base + 8 * vi
        if s == 2:
            lo = jnp.broadcast_to(gc_s[r0 + off:r0 + off + 1, hs], (8, HG_DK))
            hi = jnp.broadcast_to(gc_s[r0 + 4 + off:r0 + 5 + off, hs], (8, HG_DK))
            parts.append(jnp.where(row8 < 4, lo, hi))
        else:
            m = base + ((8 * vi) // (2 * s)) * (2 * s) + off
            parts.append(jnp.broadcast_to(gc_s[m:m + 1, hs], (8, HG_DK)))
    return jnp.concatenate(parts, axis=0)


HG_N_IN = 5
HG_N_SCRATCH = 3


def _hg_kernel(*refs):
    n = pl.program_id(1)
    o_f, o_b = refs[2 * HG_N_IN], refs[2 * HG_N_IN + 1]
    scr = refs[2 * HG_N_IN + 2:]
    _hg_direction(n, *refs[0:HG_N_IN], o_f, *scr[0:HG_N_SCRATCH], reverse=False)
    _hg_direction(n, *refs[HG_N_IN:2 * HG_N_IN], o_b, *scr[HG_N_SCRATCH:], reverse=True)


def _hg_direction(n, xq_ref, xf_ref, xi_ref, low_ref, pm_ref, o_ref, st_ref, k_s, gc_s, *, reverse):
    tb = TOK_BLK
    c_len = HG_CHUNK
    n_ch = tb // c_len

    @pl.when(n == 0)
    def _():
        st_ref[...] = jnp.zeros_like(st_ref)

    low = low_ref[...]
    f = low + (1.0 - low) * xf_ref[0].astype(F32)
    g = jnp.log2(f)
    r = _iota((tb, tb), 0)
    c_ = _iota((tb, tb), 1)
    same = _shr(r, c_len) == _shr(c_, c_len)
    tri = (r <= c_) if reverse else (r >= c_)
    gc_s[...] = _dot_split3(jnp.where(jnp.logical_and(same, tri), 1.0, 0.0), g, pieces=2)
    k_s[...] = (1.0 - f).astype(BF16)

    row8 = _iota((8, HG_DK), 0)
    last = 0 if reverse else c_len - 1
    hsl = [slice(h * HG_DK, (h + 1) * HG_DK) for h in range(HG_HEADS)]
    heads = range(HG_HEADS)

    def scores(c):
        base = c * c_len
        rs = slice(base, base + c_len)
        q = {h: xq_ref[0, rs, hsl[h]].astype(F32) for h in heads}
        k = {h: k_s[rs, hsl[h]].astype(F32) for h in heads}
        gc = {h: gc_s[rs, hsl[h]] for h in heads}
        tiles = range(c_len // 8)
        t8 = lambda x, v: x[8 * v:8 * v + 8]
        diag = pm_ref[len(HG_LEVELS)]
        a = {h: [t8(x, v) for v in tiles] for h, x in ((h, _dot_nt(q[h], k[h]) * diag) for h in heads)}
        q_half = 0 if reverse else 1
        for lvl, s in enumerate(HG_LEVELS):
            pm = pm_ref[lvl]
            if s < 8:
                is_q = (_shr(row8, s) & 1) == q_half
                pick = lambda h, v: jnp.where(is_q, t8(q[h], v), t8(k[h], v))
                q_tiles = list(tiles)
            else:
                pick = lambda h, v: t8(q[h], v) if ((8 * v) // s) & 1 == q_half else t8(k[h], v)
                q_tiles = [v for v in tiles if ((8 * v) // s) & 1 == q_half]
            z = {}
            for h in heads:
                e = jnp.exp2(_neg_abs(gc[h] - _hg_boundary(gc_s, gc[h], base, hsl[h], s, row8, reverse)))
                z[h] = jnp.concatenate([pick(h, v) * t8(e, v) for v in tiles], axis=0)
            pr = {h: _dot_nt(z[h], z[h]) for h in heads}
            for h in heads:
                for v in q_tiles:
                    a[h][v] = a[h][v] + t8(pr[h], v) * t8(pm, v)
        return {h: jnp.concatenate(a[h], axis=0) for h in heads}

    def outputs(c, a):
        base = c * c_len
        rs = slice(base, base + c_len)
        q = {h: xq_ref[0, rs, hsl[h]].astype(F32) for h in heads}
        k = {h: k_s[rs, hsl[h]].astype(F32) for h in heads}
        gc = {h: gc_s[rs, hsl[h]] for h in heads}
        gl = {h: gc[h][last:last + 1, :] for h in heads}
        st = {h: st_ref[h] for h in heads}
        o = {h: _dot(a[h], xi_ref[0, rs, hsl[h]]) + _dot_nt(q[h] * jnp.exp2(gc[h]), st[h]) for h in heads}
        for h in heads:
            kd = k[h] * jnp.exp2(gl[h] - gc[h])
            st_ref[h] = st[h] * jnp.exp2(gl[h]) + _dot_tn(xi_ref[0, rs, hsl[h]], kd)
            o_ref[0, rs, hsl[h]] = o[h].astype(BF16)

    order = [(n_ch - 1 - cc) if reverse else cc for cc in range(n_ch)]
    pending = None
    for c in order:
        a = scores(c)
        if pending is not None:
            outputs(*pending)
        pending = (c, a)
    outputs(*pending)


def _hg_scan(p, low_rows, tc):
    n_b, ta, _ = p.shape
    tb = TOK_BLK
    n_blk = ta // tb
    nctx = tc // tb
    in_specs, args, out_specs, scratch = [], [], [], []
    for d_i, reverse in enumerate((False, True)):
        masks = _hg_pair_masks(reverse)
        bi = functools.partial(_blk_index, n_blk=n_blk, nctx=nctx, reverse=reverse)
        in_specs += [pl.BlockSpec((1, tb, BRANCH_W), lambda b, n, bi=bi: (b, bi(n), C_HQ // BRANCH_W)),
                     pl.BlockSpec((1, tb, BRANCH_W), lambda b, n, bi=bi, d_i=d_i: (b, bi(n), C_HF // BRANCH_W + d_i)),
                     pl.BlockSpec((1, tb, BRANCH_W), lambda b, n, bi=bi: (b, bi(n), C_HI // BRANCH_W)),
                     pl.BlockSpec((1, BRANCH_W), lambda b, n: (0, 0)),
                     pl.BlockSpec(masks.shape, lambda b, n: (0, 0, 0))]
        args += [p, p, p, low_rows[d_i], masks]
        out_specs.append(pl.BlockSpec((1, tb, BRANCH_W), lambda b, n, bi=bi: (b, bi(n), 0)))
        scratch += [pltpu.VMEM((HG_HEADS, HG_DV, HG_DK), F32),
                    pltpu.VMEM((tb, BRANCH_W), BF16),
                    pltpu.VMEM((tb, BRANCH_W), F32)]
    assert len(args) == 2 * HG_N_IN and len(scratch) == 2 * HG_N_SCRATCH
    return pl.pallas_call(
        _hg_kernel,
        grid=(n_b, n_blk),
        in_specs=in_specs,
        out_specs=out_specs,
        out_shape=[jax.ShapeDtypeStruct((n_b, ta, BRANCH_W), BF16)] * 2,
        scratch_shapes=scratch,
        compiler_params=_cparams(("arbitrary", "arbitrary")),
        name="hg_scan",
    )(*args)


def _branch(of_ref, ob_ref, gate_ref, nw_ref, dv):
    o = of_ref[0].astype(F32) + ob_ref[0].astype(F32)
    gate = gate_ref[0].astype(F32)
    nw = nw_ref[...]
    parts = []
    for h in range(BRANCH_W // dv):
        x = o[:, h * dv:(h + 1) * dv]
        y = x * lax.rsqrt(jnp.mean(x * x, axis=-1, keepdims=True) + NORM_EPS) * nw
        parts.append(y)
    return (jnp.concatenate(parts, axis=-1) * gate).astype(BF16)


def _merge_kernel(dnf, dnb, rtf, rtb, hgf, hgb, g_dn, g_rt, g_hg, m0, m1, m2, x_ref, mod_ref,
                  nw_dn, nw_rt, nw_hg, wb_ref, wo_ref, fnw_ref, o_ref, *, tc, blk_off, n_b, final):
    b = pl.program_id(0)
    i = pl.program_id(1) + blk_off
    d = D_MODEL
    brs = (_branch(dnf, dnb, g_dn, nw_dn, DN_DV),
           _branch(rtf, rtb, g_rt, nw_rt, RET_DV),
           _branch(hgf, hgb, g_hg, nw_hg, HG_DV))
    y = None
    for idx, (br, mg) in enumerate(zip(brs, (m0, m1, m2))):
        t = jnp.dot(br, wb_ref[idx], preferred_element_type=F32) * mg[0].astype(F32)
        y = t if y is None else y + t
    out = jnp.dot(y.astype(BF16), wo_ref[...], preferred_element_type=F32)
    is_ctx = (i * TOK_BLK) < tc
    gt = jnp.where(is_ctx, mod_ref[n_b:n_b + 1, 2 * d:3 * d], mod_ref[pl.ds(b, 1), 2 * d:3 * d])
    xn = x_ref[0] + gt * out
    if final:
        xn = xn * lax.rsqrt(jnp.mean(xn * xn, axis=-1, keepdims=True) + NORM_EPS) * fnw_ref[...]
    o_ref[0] = xn


def _merge(outs, p, xa, mod_l, nws, wb, wo, fnw, tc, final):
    n_b, ta, d = xa.shape
    tb = TOK_BLK
    blk_off = tc // tb if final else 0
    n_blk = ta // tb - blk_off
    rows = lambda b, i: (b, i + blk_off, 0)
    pcol = lambda cb: (lambda b, i: (b, i + blk_off, cb))
    const2 = lambda b, i: (0, 0)
    in_specs = ([pl.BlockSpec((1, tb, BRANCH_W), rows)] * 6
                + [pl.BlockSpec((1, tb, BRANCH_W), pcol(C_DNGATE // BRANCH_W)),
                   pl.BlockSpec((1, tb, BRANCH_W), pcol(C_RGATE // BRANCH_W)),
                   pl.BlockSpec((1, tb, BRANCH_W), pcol(C_HGATE // BRANCH_W)),
                   pl.BlockSpec((1, tb, BRANCH_W), pcol(C_MERGE0 // BRANCH_W)),
                   pl.BlockSpec((1, tb, BRANCH_W), pcol(C_MERGE1 // BRANCH_W)),
                   pl.BlockSpec((1, tb, BRANCH_W), pcol(C_MERGE2 // BRANCH_W)),
                   pl.BlockSpec((1, tb, d), rows),
                   pl.BlockSpec((8, 3 * d), const2),
                   pl.BlockSpec((1, DN_DV), const2),
                   pl.BlockSpec((1, RET_DV), const2),
                   pl.BlockSpec((1, HG_DV), const2),
                   pl.BlockSpec((N_BRANCH, BRANCH_W, d), lambda b, i: (0, 0, 0)),
                   pl.BlockSpec((d, d), const2),
                   pl.BlockSpec((1, d), const2)])
    out_rows = ta - blk_off * tb
    return pl.pallas_call(
        functools.partial(_merge_kernel, tc=tc, blk_off=blk_off, n_b=n_b, final=final),
        grid=(n_b, n_blk),
        in_specs=in_specs,
        out_specs=pl.BlockSpec((1, tb, d), lambda b, i: (b, i, 0)),
        out_shape=jax.ShapeDtypeStruct((n_b, out_rows, d), F32),
        compiler_params=_cparams(("arbitrary", "arbitrary")),
        name="merge_final" if final else "merge",
    )(*outs, p, p, p, p, p, p, xa, mod_l, *nws, wb, wo, fnw)


def _permute_w_in(w):
    d = w.shape[0]
    ab0 = DN_CONV_CH + BRANCH_W
    wp = jnp.concatenate([w[:, :ab0], w[:, ab0 + 16:]], axis=1).astype(BF16)
    wab = jnp.pad(w[:, ab0:ab0 + 16], ((0, 0), (0, LANES - 16))).astype(BF16)
    return wp, wab


def _rope_tables(n_lat, tc):
    half = RET_DK // 2
    inv = ROPE_BASE ** (-np.arange(0, half, 2, dtype=np.float64) / half)
    t = np.arange(n_lat)
    row = (t // GRID_W).astype(np.float64)
    col = (t % GRID_W).astype(np.float64)
    ang = np.concatenate([row[:, None] * inv, col[:, None] * inv], axis=-1)
    cos, sin = np.cos(ang), np.sin(ang)
    cos_l = np.repeat(cos, 2, axis=-1)
    sin_l = np.stack([-sin, sin], axis=-1).reshape(n_lat, LANES)
    cos_all = np.concatenate([np.ones((tc, LANES)), cos_l], axis=0).astype(np.float32)
    sin_all = np.concatenate([np.zeros((tc, LANES)), sin_l], axis=0).astype(np.float32)
    return jnp.asarray(cos_all), jnp.asarray(sin_all)


def _pad_row(v, width=LANES):
    v = v.reshape(1, -1).astype(F32)
    return jnp.pad(v, ((0, 0), (0, width - v.shape[1])))


def kernel(x, c, ctx, c_ctx, norm_w, ada_w, ada_b, w_in, dn_conv, dn_a_log, dn_dt_bias, dn_norm_w,
           ret_decay, ret_norm_w, hg_lb, hg_norm_w, w_branch, w_out, final_norm_w):
    n_b, n_lat, d = x.shape
    tc = ctx.shape[1]
    depth = norm_w.shape[0]
    assert d == D_MODEL and tc % TOK_BLK == 0 and n_lat % TOK_BLK == 0 and n_b + 1 <= 8
    assert n_lat % GRID_W == 0

    xa = jnp.concatenate([ctx, x], axis=1).astype(F32)
    cin = jnp.concatenate([c, c_ctx[None, :], jnp.zeros((8 - n_b - 1, d), c.dtype)], axis=0).astype(F32)
    mod = _modulation(cin, ada_w.astype(F32), ada_b.astype(F32))
    lower = _hg_lower(hg_lb)
    cos_t, sin_t = _rope_tables(n_lat, tc)
    fnw = final_norm_w.reshape(1, d).astype(F32)

    for l in range(depth):
        final = l == depth - 1
        wp, wab = _permute_w_in(w_in[l])
        p, pab = _project(xa, mod[l], norm_w[l].reshape(1, d).astype(F32), wp, wab, tc)
        dq, dk, dkt, dv, rq, rkt, gcol, grow, dktf, growf = _prep(
            p, pab, dn_conv[l].astype(F32), cos_t, sin_t, _pad_row(dn_a_log[l]), _pad_row(dn_dt_bias[l]), tc)
        lam_row = _pad_row(ret_decay[l])
        outs = []
        outs += _dn_scan(dq, dk, dktf, dkt, dv, gcol, growf, grow, tc)
        outs += _ret_scan(rq, rkt, p, lam_row, tc)
        outs += _hg_scan(p, (lower[l, 0].reshape(1, -1), lower[l, 1].reshape(1, -1)), tc)
        nws = (dn_norm_w[l].reshape(1, -1).astype(F32), ret_norm_w[l].reshape(1, -1).astype(F32),
               hg_norm_w[l].reshape(1, -1).astype(F32))
        xa = _merge(outs, p, xa, mod[l], nws, w_branch[l].astype(BF16), w_out[l].astype(BF16), fnw, tc, final)
    return xa
```
